```python
import numpy as np
import jax
import jax.numpy as jnp
from jax import lax

D_MODEL = 1024
BATCH = 8
SEQ = 4096
DEPTH = 1

GRID_W = 64
CTX_LEN = 256
NA_HEADS = 16
NA_HEAD_DIM = 64
NA_WIDTH = NA_HEADS * NA_HEAD_DIM
NA_MAX_ROWS = 8
NA_COLS = 16
RW_HEADS = 16
RW_HEAD_DIM = 64
RW_WIDTH = RW_HEADS * RW_HEAD_DIM
DECAY_LORA = 64
AAA_LORA = 64
N_DIR = 2
N_BRANCH = 2
D_IN = 4 * NA_WIDTH + 4 * RW_WIDTH + N_DIR * (DECAY_LORA + AAA_LORA) + N_BRANCH * D_MODEL
ROPE_THETA = 10000.0
LN_EPS = 1e-5
GN_EPS = 64e-5
ALPHA = (2 * DEPTH) ** 0.25
BETA = (8 * DEPTH) ** -0.25

kernel_name = 'hybrid_na_rwkv7_dit_block'


def layer_norm(x, gain=None, bias=None, eps=LN_EPS):
    xf = x.astype(jnp.float32)
    mu = jnp.mean(xf, axis=-1, keepdims=True)
    var = jnp.mean(jnp.square(xf - mu), axis=-1, keepdims=True)
    y = (xf - mu) * lax.rsqrt(var + eps)
    if gain is not None:
        y = y * gain + bias
    return y.astype(x.dtype)


def ada_modulation(cond, w_ada, b_ada):
    m = jax.nn.silu(cond) @ w_ada + b_ada
    return jnp.split(m, 3, axis=-1)


def split_projection(u):
    sizes = [NA_WIDTH] * 4 + [RW_WIDTH] * 4 + [N_DIR * DECAY_LORA, N_DIR * AAA_LORA, N_BRANCH * D_MODEL]
    return jnp.split(u, np.cumsum(sizes)[:-1].tolist(), axis=-1)


def to_heads(u, n_heads):
    return u.reshape(u.shape[:-1] + (n_heads, u.shape[-1] // n_heads))


def centred_shift(u, mu):
    pad = jnp.pad(u, ((0, 0), (1, 1), (0, 0)))
    nb = 0.5 * (pad[:, :-2] + pad[:, 2:])
    return u + mu * (nb - u)


def axial_rope(u):
    T, dh = u.shape[1], u.shape[-1]
    half = dh // 2
    n_freq = half // 2
    t = jnp.arange(T)
    inv_freq = ROPE_THETA ** (-jnp.arange(n_freq, dtype=jnp.float32) / n_freq)

    def rotate(v, pos):
        ang = pos.astype(jnp.float32)[:, None] * inv_freq
        cos, sin = jnp.cos(ang)[:, None, :], jnp.sin(ang)[:, None, :]
        v1, v2 = v[..., :n_freq], v[..., n_freq:]
        return jnp.concatenate([v1 * cos - v2 * sin, v1 * sin + v2 * cos], axis=-1)

    out = jnp.concatenate([rotate(u[..., :half], t // GRID_W), rotate(u[..., half:], t % GRID_W)], axis=-1)
    return out.astype(u.dtype)


def neighbourhood_attention(q, k, v, kc, vc, rpb):
    B, T, H, dh = q.shape
    rows = T // GRID_W
    kh = min(NA_MAX_ROWS, rows)
    scale = dh ** -0.5
    grid = lambda u: u.reshape(B, rows, GRID_W, H, dh)
    qg, kg, vg = grid(q), grid(k), grid(v)
    cols = jnp.arange(GRID_W)
    cstart = jnp.clip(cols - NA_COLS // 2, 0, GRID_W - NA_COLS)
    col_mask = (cols[None, :] >= cstart[:, None]) & (cols[None, :] < cstart[:, None] + NA_COLS)
    dj_idx = jnp.clip(cols[None, :] - cols[:, None] + NA_COLS - 1, 0, 2 * NA_COLS - 2)
    rpb_cols = rpb[:, :, dj_idx]

    def one_row(args):
        i, q_i = args
        rs = jnp.clip(i - kh // 2, 0, rows - kh)
        k_band = lax.dynamic_slice_in_dim(kg, rs, kh, axis=1)
        v_band = lax.dynamic_slice_in_dim(vg, rs, kh, axis=1)
        s_band = jnp.einsum('bjhd,brchd->bhjrc', q_i, k_band).astype(jnp.float32) * scale
        di_idx = rs + jnp.arange(kh) - i + NA_MAX_ROWS - 1
        bias = jnp.take(rpb_cols, di_idx, axis=1).transpose(0, 2, 1, 3)
        s_band = jnp.where(col_mask[:, None, :], s_band + bias, -jnp.inf)
        s_ctx = jnp.einsum('bjhd,blhd->bhjl', q_i, kc).astype(jnp.float32) * scale
        s = jnp.concatenate([s_band.reshape(B, H, GRID_W, kh * GRID_W), s_ctx], axis=-1)
        p = jax.nn.softmax(s, axis=-1).astype(v.dtype)
        p_band = p[..., :kh * GRID_W].reshape(B, H, GRID_W, kh, GRID_W)
        p_ctx = p[..., kh * GRID_W:]
        return (jnp.einsum('bhjrc,brchd->bjhd', p_band, v_band)
                + jnp.einsum('bhjl,blhd->bjhd', p_ctx, vc))

    out = lax.map(one_row, (jnp.arange(rows), qg.transpose(1, 0, 2, 3, 4)))
    return out.transpose(1, 0, 2, 3, 4).reshape(B, T, H, dh)


def context_attention(q, k, v):
    scale = q.shape[-1] ** -0.5
    s = jnp.einsum('blhd,bmhd->bhlm', q, k).astype(jnp.float32) * scale
    p = jax.nn.softmax(s, axis=-1).astype(v.dtype)
    return jnp.einsum('bhlm,bmhd->blhd', p, v)


def to_scan_dirs(u):
    u = u.transpose(1, 2, 0, 3, 4)
    return jnp.stack([u[:, 0], u[::-1, 1]], axis=1)


def to_scan_shared(u):
    u = u.transpose(1, 0, 2, 3)
    return jnp.stack([u, u[::-1]], axis=1)


def from_scan(y):
    return (y[:, 0] + y[::-1, 1]).transpose(1, 0, 2, 3)


def delta_rule_scan(state0, xs):
    def step(S, inp):
        r_t, w_t, k_t, a_t, b_t, v_t = inp
        sa = jnp.einsum('...vk,...k->...v', S, a_t)
        S = S * w_t[..., None, :] + sa[..., :, None] * b_t[..., None, :] + v_t[..., :, None] * k_t[..., None, :]
        return S, jnp.einsum('...vk,...k->...v', S, r_t)
    return lax.scan(step, state0, xs)


def rwkv_prepare(r, k, v, wd, ad, mu, w0, w_up, a0, a_up, k_k, k_a, r_k, rotary):
    B, T, C = r.shape
    f32 = jnp.float32
    r = to_heads(centred_shift(r, mu[0]), RW_HEADS).astype(f32)
    k = to_heads(centred_shift(k, mu[1]), RW_HEADS).astype(f32)
    v = to_heads(centred_shift(v, mu[2]), RW_HEADS).astype(f32)
    if rotary:
        r, k = axial_rope(r), axial_rope(k)
    wd = wd.reshape(B, T, N_DIR, DECAY_LORA)
    ad = ad.reshape(B, T, N_DIR, AAA_LORA)
    w = -jax.nn.softplus(-(w0 + jnp.einsum('btdr,drc->btdc', jnp.tanh(wd), w_up))) - 0.5
    decay = to_heads(jnp.exp(-jnp.exp(w.astype(f32))), RW_HEADS)
    a = to_heads(jax.nn.sigmoid((a0 + jnp.einsum('btdr,drc->btdc', ad, a_up)).astype(f32)), RW_HEADS)
    kk = k * to_heads(k_k, RW_HEADS)
    kk = kk / jnp.maximum(jnp.sqrt(jnp.sum(jnp.square(kk), axis=-1, keepdims=True)), 1e-12)
    k_dir = k[:, :, None] * (1.0 + (a - 1.0) * to_heads(k_a, RW_HEADS))
    b_vec = kk[:, :, None] * a
    bonus = jnp.sum(r[:, :, None] * r_k * k_dir, axis=(2, 4))[..., None] * v
    xs = (to_scan_shared(r), to_scan_dirs(decay), to_scan_dirs(k_dir),
          to_scan_shared(-kk), to_scan_dirs(b_vec), to_scan_shared(v))
    return xs, bonus


def rwkv_readout(ys, bonus, z, gn_g, gn_b):
    y = from_scan(ys)
    B, T = y.shape[0], y.shape[1]
    mu = jnp.mean(y, axis=-1, keepdims=True)
    var = jnp.mean(jnp.square(y - mu), axis=-1, keepdims=True)
    y = ((y - mu) * lax.rsqrt(var + GN_EPS)).reshape(B, T, RW_WIDTH) * gn_g + gn_b
    y = y + bonus.reshape(B, T, RW_WIDTH)
    return (y * jax.nn.silu(z.astype(jnp.float32))).astype(z.dtype)


def merge_branches(y_a, y_b, m_g, w_branch_a, w_branch_b, w_out):
    g_a, g_b = jnp.split(jax.nn.sigmoid(m_g), 2, axis=-1)
    return (g_a * (y_a @ w_branch_a) + g_b * (y_b @ w_branch_b)) @ w_out


def hybrid_layer(x, ctx, c, c_ctx, w_ada, b_ada, w_in, na_rpb, rw_mu, rw_w0, rw_w_up, rw_a0, rw_a_up,
                 rw_k_k, rw_k_a, rw_r_k, rw_gn_g, rw_gn_b, w_branch_a, w_branch_b, w_out, ln_g, ln_b,
                 need_ctx_out):
    B, T, _ = x.shape
    L = ctx.shape[1]
    shift, scale, gate = ada_modulation(c, w_ada, b_ada)
    shift_c, scale_c, gate_c = ada_modulation(c_ctx, w_ada, b_ada)
    h = layer_norm(x) * (1.0 + scale[:, None]) + shift[:, None]
    h_c = layer_norm(ctx) * (1.0 + scale_c) + shift_c
    q_a, k_a, v_a, z_a, r_b, k_b, v_b, z_b, wd_b, ad_b, m_g = split_projection(h @ w_in)
    q_ac, k_ac, v_ac, z_ac, r_bc, k_bc, v_bc, z_bc, wd_bc, ad_bc, m_gc = split_projection(h_c @ w_in)
    rw_params = (rw_mu, rw_w0, rw_w_up, rw_a0, rw_a_up, rw_k_k, rw_k_a, rw_r_k)

    kc_h, vc_h = to_heads(k_ac, NA_HEADS), to_heads(v_ac, NA_HEADS)
    o_a = neighbourhood_attention(to_heads(q_a, NA_HEADS), to_heads(k_a, NA_HEADS), to_heads(v_a, NA_HEADS),
                                  kc_h, vc_h, na_rpb)
    y_a = o_a.reshape(B, T, NA_WIDTH) * jax.nn.silu(z_a)

    xs_c, bonus_c = rwkv_prepare(r_bc, k_bc, v_bc, wd_bc, ad_bc, *rw_params, rotary=False)
    state0 = jnp.zeros((N_DIR, B, RW_HEADS, RW_HEAD_DIM, RW_HEAD_DIM), jnp.float32)
    state_c, ys_c = delta_rule_scan(state0, xs_c)
    xs, bonus = rwkv_prepare(r_b, k_b, v_b, wd_b, ad_b, *rw_params, rotary=True)
    _, ys = delta_rule_scan(state_c, xs)
    y_b = rwkv_readout(ys, bonus, z_b, rw_gn_g, rw_gn_b)

    out = merge_branches(y_a, y_b, m_g, w_branch_a, w_branch_b, w_out)
    x_new = layer_norm(ALPHA * x + gate[:, None] * out, ln_g, ln_b)

    if need_ctx_out:
        o_ac = context_attention(to_heads(q_ac, NA_HEADS), kc_h, vc_h).reshape(B, L, NA_WIDTH)
        y_ac = o_ac * jax.nn.silu(z_ac)
        y_bc = rwkv_readout(ys_c, bonus_c, z_bc, rw_gn_g, rw_gn_b)
        out_c = merge_branches(y_ac, y_bc, m_gc, w_branch_a, w_branch_b, w_out)
        ctx = layer_norm(ALPHA * ctx + gate_c * out_c, ln_g, ln_b)
    return x_new, ctx


def setup_inputs(seed: int = 0) -> dict:
    key = jax.random.key(seed)
    ks = jax.random.split(key, 24)
    f32 = jnp.float32
    nrm = lambda k, shape, s: jax.random.normal(k, shape, f32) * s
    col_scale = jnp.concatenate([
        jnp.ones((2 * NA_WIDTH,), f32), jnp.full((NA_WIDTH,), BETA, f32),
        jnp.ones((NA_WIDTH + 2 * RW_WIDTH,), f32), jnp.full((RW_WIDTH,), BETA, f32),
        jnp.ones((RW_WIDTH + N_DIR * (DECAY_LORA + AAA_LORA) + N_BRANCH * D_MODEL,), f32)])
    return {
        'x': nrm(ks[0], (BATCH, SEQ, D_MODEL), 1.0),
        'c': nrm(ks[1], (BATCH, D_MODEL), 1.0),
        'ctx': nrm(ks[2], (BATCH, CTX_LEN, D_MODEL), 1.0),
        'c_ctx': nrm(ks[3], (D_MODEL,), 1.0),
        'w_ada': nrm(ks[4], (DEPTH, D_MODEL, 3 * D_MODEL), 0.5 * D_MODEL ** -0.5),
        'b_ada': nrm(ks[5], (DEPTH, 3 * D_MODEL), 0.02),
        'w_in': nrm(ks[6], (DEPTH, D_MODEL, D_IN), D_MODEL ** -0.5) * col_scale,
        'na_rpb': nrm(ks[7], (DEPTH, NA_HEADS, 2 * NA_MAX_ROWS - 1, 2 * NA_COLS - 1), 0.1),
        'rw_mu': jax.random.uniform(ks[8], (DEPTH, 3, RW_WIDTH), f32, 0.0, 1.0),
        'rw_w0': jax.random.uniform(ks[9], (DEPTH, N_DIR, RW_WIDTH), f32, -5.0, 0.0),
        'rw_w_up': nrm(ks[10], (DEPTH, N_DIR, DECAY_LORA, RW_WIDTH), 0.3 * DECAY_LORA ** -0.5),
        'rw_a0': nrm(ks[11], (DEPTH, N_DIR, RW_WIDTH), 0.1),
        'rw_a_up': nrm(ks[12], (DEPTH, N_DIR, AAA_LORA, RW_WIDTH), 0.3 * AAA_LORA ** -0.5),
        'rw_k_k': 0.85 + nrm(ks[13], (DEPTH, RW_WIDTH), 0.02),
        'rw_k_a': 1.0 + nrm(ks[14], (DEPTH, RW_WIDTH), 0.02),
        'rw_r_k': nrm(ks[15], (DEPTH, RW_HEADS, RW_HEAD_DIM), 0.1),
        'rw_gn_g': 1.0 + nrm(ks[16], (DEPTH, RW_WIDTH), 0.02),
        'rw_gn_b': nrm(ks[17], (DEPTH, RW_WIDTH), 0.02),
        'w_branch_a': nrm(ks[18], (DEPTH, NA_WIDTH, D_MODEL), BETA * NA_WIDTH ** -0.5),
        'w_branch_b': nrm(ks[19], (DEPTH, RW_WIDTH, D_MODEL), BETA * RW_WIDTH ** -0.5),
        'w_out': nrm(ks[20], (DEPTH, D_MODEL, D_MODEL), BETA * D_MODEL ** -0.5),
        'ln_g': 1.0 + nrm(ks[21], (DEPTH, D_MODEL), 0.02),
        'ln_b': nrm(ks[22], (DEPTH, D_MODEL), 0.02),
    }


def reference(x, c, ctx, c_ctx, w_ada, b_ada, w_in, na_rpb, rw_mu, rw_w0, rw_w_up, rw_a0, rw_a_up,
              rw_k_k, rw_k_a, rw_r_k, rw_gn_g, rw_gn_b, w_branch_a, w_branch_b, w_out, ln_g, ln_b):
    for layer in range(DEPTH):
        x, ctx = hybrid_layer(
            x, ctx, c, c_ctx, w_ada[layer], b_ada[layer], w_in[layer], na_rpb[layer], rw_mu[layer],
            rw_w0[layer], rw_w_up[layer], rw_a0[layer], rw_a_up[layer], rw_k_k[layer], rw_k_a[layer],
            rw_r_k[layer], rw_gn_g[layer], rw_gn_b[layer], w_branch_a[layer], w_branch_b[layer],
            w_out[layer], ln_g[layer], ln_b[layer], need_ctx_out=layer < DEPTH - 1)
    return x
```

```python
import functools

import numpy as np
import jax
import jax.numpy as jnp
from jax import lax
from jax.experimental import pallas as pl
from jax.experimental.pallas import tpu as pltpu

F32 = jnp.float32
BF16 = jnp.bfloat16
HIGHEST = lax.Precision.HIGHEST

D_MODEL = 1024
GRID_W = 64
HEADS = 16
HEAD_DIM = 64
NA_MAX_ROWS = 8
NA_COLS = 16
LORA = 64
DEPTH = 1
ROPE_THETA = 10000.0
LN_EPS = 1e-5
GN_EPS = 64e-5
ALPHA = (2 * DEPTH) ** 0.25

LANES = 128
GROUP = 256
HEADS_PER_GROUP = GROUP // HEAD_DIM
N_GROUPS = D_MODEL // GROUP
CHUNK = 64

N_MAIN = 8 * D_MODEL
COL_MG = N_MAIN
COL_LORA = N_MAIN + 2 * D_MODEL
N_PROJ = COL_LORA + 4 * LORA
N_PAD = 21 * 512
VMEM_LIMIT = 56 * 1024 * 1024


def _cparams(sem):
    return pltpu.CompilerParams(dimension_semantics=sem, vmem_limit_bytes=VMEM_LIMIT)


def _silu(x):
    return x * jax.nn.sigmoid(x)


def _mod_kernel(c_ref, w_ref, b_ref, o_ref):
    s = _silu(c_ref[...])
    o_ref[...] = jnp.dot(s, w_ref[...], preferred_element_type=F32, precision=HIGHEST) + b_ref[...]


def _modulation(cond, w_ada, b_ada):
    rows = cond.shape[0]
    return pl.pallas_call(
        _mod_kernel,
        out_shape=jax.ShapeDtypeStruct((rows, 3 * D_MODEL), F32),
        grid=(3,),
        in_specs=[pl.BlockSpec((rows, D_MODEL), lambda j: (0, 0)),
                  pl.BlockSpec((D_MODEL, D_MODEL), lambda j: (0, j)),
                  pl.BlockSpec((1, D_MODEL), lambda j: (0, j))],
        out_specs=pl.BlockSpec((rows, D_MODEL), lambda j: (0, j)),
        compiler_params=_cparams(("arbitrary",)),
        name="ada_modulation",
    )(cond, w_ada, b_ada.reshape(1, 3 * D_MODEL))


def _inproj_kernel(x_ref, sh_ref, sc_ref, w_ref, o_ref, h_scr):
    @pl.when(pl.program_id(2) == 0)
    def _():
        x = x_ref[...]
        mu = jnp.mean(x, axis=-1, keepdims=True)
        xc = x - mu
        var = jnp.mean(xc * xc, axis=-1, keepdims=True)
        y = xc * lax.rsqrt(var + LN_EPS)
        h_scr[...] = (y * (1.0 + sc_ref[...]) + sh_ref[...]).astype(BF16)

    o_ref[...] = jnp.dot(h_scr[...], w_ref[...], preferred_element_type=F32).astype(BF16)


def _in_projection(x, mod3, w_bf, row_of_batch, tm, tn):
    B, T, _ = x.shape
    return pl.pallas_call(
        _inproj_kernel,
        out_shape=jax.ShapeDtypeStruct((B, T, N_PAD), BF16),
        grid=(B, T // tm, N_PAD // tn),
        in_specs=[pl.BlockSpec((None, tm, D_MODEL), lambda b, i, j: (b, i, 0)),
                  pl.BlockSpec((None, 1, D_MODEL), lambda b, i, j: (row_of_batch(b), 0, 0)),
                  pl.BlockSpec((None, 1, D_MODEL), lambda b, i, j: (row_of_batch(b), 0, 1)),
                  pl.BlockSpec((D_MODEL, tn), lambda b, i, j: (0, j))],
        out_specs=pl.BlockSpec((None, tm, tn), lambda b, i, j: (b, i, j)),
        scratch_shapes=[pltpu.VMEM((tm, D_MODEL), BF16)],
        compiler_params=_cparams(("arbitrary", "arbitrary", "arbitrary")),
        name="ln_mod_in_proj",
    )(x, mod3, mod3, w_bf)


def _na_kernel(q_ref, k_ref, v_ref, z_ref, kc_ref, vc_ref, bias_ref, o_ref, *, rows):
    kh = min(NA_MAX_ROWS, rows)
    band = kh * GRID_W
    lane = lax.broadcasted_iota(jnp.int32, (1, LANES), 1)
    first_head = lane < HEAD_DIM
    kc = kc_ref[...]
    vc = vc_ref[...]
    nt = (((1,), (1,)), ((), ()))

    def row_body(i, carry):
        rs = jnp.clip(i - kh // 2, 0, rows - kh)
        q0 = pl.multiple_of(i * GRID_W, GRID_W)
        k0 = pl.multiple_of(rs * GRID_W, GRID_W)
        q = q_ref[pl.ds(q0, GRID_W), :] * jnp.asarray(HEAD_DIM ** -0.5, BF16)
        kb = k_ref[pl.ds(k0, band), :]
        vb = v_ref[pl.ds(k0, band), :]
        outs = []
        for h in range(2):
            keep = first_head if h == 0 else jnp.logical_not(first_head)
            qh = jnp.where(keep, q, jnp.zeros_like(q))
            s = lax.dot_general(qh, kb, nt, preferred_element_type=F32) + bias_ref[h, i - rs]
            sc = lax.dot_general(qh, kc, nt, preferred_element_type=F32)
            m = jnp.maximum(jnp.max(s, axis=-1, keepdims=True), jnp.max(sc, axis=-1, keepdims=True))
            p = jnp.exp(s - m)
            pc = jnp.exp(sc - m)
            denom = jnp.sum(p, axis=-1, keepdims=True) + jnp.sum(pc, axis=-1, keepdims=True)
            o = (jnp.dot(p.astype(BF16), vb, preferred_element_type=F32)
                 + jnp.dot(pc.astype(BF16), vc, preferred_element_type=F32))
            outs.append(o / denom)
        o = jnp.where(first_head, outs[0], outs[1])
        z = z_ref[pl.ds(q0, GRID_W), :].astype(F32)
        o_ref[pl.ds(q0, GRID_W), :] = (o * _silu(z)).astype(BF16)
        return carry

    lax.fori_loop(0, rows, row_body, 0)


def _na_bias_table(rpb, rows):
    kh = min(NA_MAX_ROWS, rows)
    cols = np.arange(GRID_W)
    cstart = np.clip(cols - NA_COLS // 2, 0, GRID_W - NA_COLS)
    col_mask = (cols[None, :] >= cstart[:, None]) & (cols[None, :] < cstart[:, None] + NA_COLS)
    dj = np.clip(cols[None, :] - cols[:, None] + NA_COLS - 1, 0, 2 * NA_COLS - 2)
    di = np.arange(kh)[None, :] - np.arange(kh)[:, None] + NA_MAX_ROWS - 1
    tab = rpb[:, di[:, :, None, None], dj[None, None, :, :]]
    tab = jnp.where(col_mask[None, None, None], tab, -jnp.inf)
    return tab.transpose(0, 1, 3, 2, 4).reshape(HEADS, kh, GRID_W, kh * GRID_W).astype(F32)


def _neighbourhood_attention(u, u_ctx, bias_tab):
    B, T, _ = u.shape
    L = u_ctx.shape[1]
    rows = T // GRID_W
    kh = min(NA_MAX_ROWS, rows)
    blk = D_MODEL // LANES

    def col(part):
        return pl.BlockSpec((None, T, LANES), lambda b, g: (b, 0, part * blk + g))

    def col_ctx(part):
        return pl.BlockSpec((None, L, LANES), lambda b, g: (b, 0, part * blk + g))

    return pl.pallas_call(
        functools.partial(_na_kernel, rows=rows),
        out_shape=jax.ShapeDtypeStruct((B, T, D_MODEL), BF16),
        grid=(B, HEADS // 2),
        in_specs=[col(0), col(1), col(2), col(3), col_ctx(1), col_ctx(2),
                  pl.BlockSpec((2, kh, GRID_W, kh * GRID_W), lambda b, g: (g, 0, 0, 0))],
        out_specs=pl.BlockSpec((None, T, LANES), lambda b, g: (b, 0, g)),
        compiler_params=_cparams(("arbitrary", "arbitrary")),
        name="neighbourhood_attention",
    )(u, u, u, u, u_ctx, u_ctx, bias_tab)


def _split_dot(x, w_bf):
    hi = x.astype(BF16)
    lo = (x - hi.astype(F32)).astype(BF16)
    return (jnp.dot(hi, w_bf, preferred_element_type=F32)
            + jnp.dot(lo, w_bf, preferred_element_type=F32))


def _head_sum(x, ones_bd):
    parts = [_split_dot(x[:, g * GROUP:(g + 1) * GROUP], ones_bd) for g in range(N_GROUPS)]
    return jnp.concatenate(parts, axis=1)


def _block_diag_masks():
    r = lax.broadcasted_iota(jnp.int32, (GROUP, GROUP), 0)
    c = lax.broadcasted_iota(jnp.int32, (GROUP, GROUP), 1)
    return (r // HEAD_DIM) == (c // HEAD_DIM)


def _rwkv_kernel(r_ref, k_ref, v_ref, rp_ref, kp_ref, vp_ref, rn_ref, kn_ref, vn_ref, lo_ref,
                 cos_ref, sin_ref, mu_ref, w0_ref, wup_ref, a0_ref, aup_ref, kk_ref, ka_ref, rk_ref,
                 s0_ref, y_ref, bonus_ref, sout_ref, h_scr, *, n_chunks, rotary, emit_y):
    C = CHUNK
    d = pl.program_id(1)
    c = pl.program_id(2)
    cidx = c + d * (n_chunks - 1 - 2 * c)

    @pl.when(c == 0)
    def _():
        h_scr[...] = s0_ref[...]

    same_head = _block_diag_masks()
    ones_bd = jnp.where(same_head, 1.0, 0.0).astype(BF16)
    row = lax.broadcasted_iota(jnp.int32, (C, 1), 0)

    def shifted(x_ref, p_ref, n_ref, mu):
        x = x_ref[...].astype(F32)
        prev_row = jnp.where(cidx > 0, p_ref[...].astype(F32)[15:16, :], 0.0)
        next_row = jnp.where(cidx < n_chunks - 1, n_ref[...].astype(F32)[0:1, :], 0.0)
        x_prev = jnp.where(row == 0, prev_row, pltpu.roll(x, 1, axis=0))
        x_next = jnp.where(row == C - 1, next_row, pltpu.roll(x, C - 1, axis=0))
        return x + mu * (0.5 * (x_prev + x_next) - x)

    r_s = shifted(r_ref, rp_ref, rn_ref, mu_ref[0:1, :])
    k_s = shifted(k_ref, kp_ref, kn_ref, mu_ref[1:2, :])
    v_s = shifted(v_ref, vp_ref, vn_ref, mu_ref[2:3, :])

    if rotary:
        lane = lax.broadcasted_iota(jnp.int32, (1, D_MODEL), 1)
        low = (lane % 32) < 16
        cos_t = jnp.tile(cos_ref[...], (1, D_MODEL // LANES))
        sin_t = jnp.tile(sin_ref[...], (1, D_MODEL // LANES))

        def rope(x):
            partner = jnp.where(low, pltpu.roll(x, D_MODEL - 16, axis=1), pltpu.roll(x, 16, axis=1))
            return x * cos_t + partner * sin_t

        r_s = rope(r_s)
        k_s = rope(k_s)

    lo = lo_ref[...]
    lw = w0_ref[...] + jnp.dot(jnp.tanh(lo.astype(F32)).astype(BF16), wup_ref[...], preferred_element_type=F32)
    w = -jax.nn.softplus(-lw) - 0.5
    ld = -jnp.exp(w)
    a = jax.nn.sigmoid(a0_ref[...] + jnp.dot(lo, aup_ref[...], preferred_element_type=F32))

    kk = k_s * kk_ref[...]
    kk = kk / jnp.maximum(jnp.sqrt(_head_sum(kk * kk, ones_bd)), 1e-12)
    k_dir = k_s * (1.0 + (a - 1.0) * ka_ref[...])
    b_vec = kk * a
    bonus_ref[...] = (_head_sum(r_s * rk_ref[...] * k_dir, ones_bd) * v_s).astype(bonus_ref.dtype)

    sgn = 1 - 2 * d
    tt = lax.broadcasted_iota(jnp.int32, (C, C), 0)
    ss = lax.broadcasted_iota(jnp.int32, (C, C), 1)
    tri = jnp.where((tt - ss) * sgn >= 0, 1.0, 0.0).astype(F32)
    cl = jnp.dot(tri, ld, preferred_element_type=F32, precision=HIGHEST)
    cl_tot = jnp.sum(ld, axis=0, keepdims=True)
    e_pos = jnp.exp(cl)
    e_neg = jnp.exp(-cl)
    e_rest = jnp.exp(cl_tot - cl)
    a_t = (-kk * jnp.exp(cl - ld)).astype(BF16)
    r_t = (r_s * e_pos).astype(BF16)
    b_t = (b_vec * e_neg).astype(BF16)
    k_t = (k_dir * e_neg).astype(BF16)
    b_h = (b_vec * e_rest).astype(BF16)
    k_h = (k_dir * e_rest).astype(BF16)
    p_c = jnp.exp(cl_tot)
    v_b = v_s.astype(BF16)

    t_i = lax.broadcasted_iota(jnp.int32, (C, GROUP), 0)
    s_i = lax.broadcasted_iota(jnp.int32, (C, GROUP), 1) % C
    strict = (t_i - s_i) * sgn > 0
    incl = (t_i - s_i) * sgn >= 0
    eye = jnp.where(t_i == s_i, 1.0, 0.0).astype(F32)
    nt = (((1,), (1,)), ((), ()))
    tn = (((0,), (0,)), ((), ()))

    def bd(x):
        return jnp.where(same_head, jnp.tile(x, (HEADS_PER_GROUP, 1)), jnp.zeros((), x.dtype))

    def hdot(a_side, x):
        return jnp.dot(a_side.astype(BF16), bd(x.astype(BF16)), preferred_element_type=F32)

    for g in range(N_GROUPS):
        sl = slice(g * GROUP, (g + 1) * GROUP)
        at, rt, bt, kt, bh, kh_, vv = a_t[:, sl], r_t[:, sl], b_t[:, sl], k_t[:, sl], b_h[:, sl], k_h[:, sl], v_b[:, sl]
        lhs = jnp.concatenate([at, rt], axis=0)
        m_b = lax.dot_general(lhs, bd(bt), nt, preferred_element_type=F32)
        m_k = lax.dot_general(lhs, bd(kt), nt, preferred_element_type=F32)
        a_ab = jnp.where(strict, m_b[:C], 0.0)
        a_rb = jnp.where(incl, m_b[C:], 0.0)
        a_ak = jnp.where(strict, m_k[:C], 0.0)
        a_rk = jnp.where(incl, m_k[C:], 0.0)

        inv = eye + a_ab
        npow = a_ab
        steps = C.bit_length() - 1
        for _ in range(steps - 1):
            npow = hdot(npow, npow)
            inv = inv + hdot(inv, npow)

        a_p = hdot(inv, at)
        u0 = hdot(inv, hdot(a_ak, vv))
        hk = jnp.concatenate([bh, kh_], axis=0)
        zeros = jnp.zeros((C, GROUP), BF16)
        m_full = lax.dot_general(hk, jnp.concatenate([a_p.astype(BF16), zeros], axis=0), tn,
                                 preferred_element_type=F32)
        g_full = lax.dot_general(hk, jnp.concatenate([u0.astype(BF16), vv], axis=0), tn,
                                 preferred_element_type=F32)

        def diag_blocks(full):
            keep = jnp.where(same_head, full, 0.0)
            out = keep[0:HEAD_DIM]
            for j in range(1, HEADS_PER_GROUP):
                out = out + keep[j * HEAD_DIM:(j + 1) * HEAD_DIM]
            return out

        h_old = h_scr[:, sl]
        h_bd = bd(h_old.astype(BF16))
        m_side = diag_blocks(m_full)
        g_side = diag_blocks(g_full)
        pc_rows = _split_dot(eye[:HEAD_DIM] * p_c[:, sl], ones_bd)
        if emit_y:
            r_p = rt.astype(F32) + hdot(a_rb, a_p)
            y0 = hdot(a_rb, u0) + hdot(a_rk, vv)
            both = jnp.dot(jnp.concatenate([r_p.astype(BF16), m_side.astype(BF16)], axis=0), h_bd,
                           preferred_element_type=F32)
            y_ref[:, sl] = (both[:C] + y0).astype(y_ref.dtype)
            mh = both[C:]
        else:
            mh = jnp.dot(m_side.astype(BF16), h_bd, preferred_element_type=F32)
        h_scr[:, sl] = pc_rows * h_old + mh + g_side

    if not emit_y:
        y_ref[...] = jnp.zeros(y_ref.shape, y_ref.dtype)

    @pl.when(c == n_chunks - 1)
    def _():
        sout_ref[...] = h_scr[...]


def _rope_tables(T):
    half = HEAD_DIM // 2
    n_freq = half // 2
    t = np.arange(T)[:, None]
    lane = np.arange(LANES)[None, :]
    inv_freq = ROPE_THETA ** (-(np.arange(n_freq, dtype=np.float32)) / n_freq)
    pos = np.where((lane % HEAD_DIM) < half, t // GRID_W, t % GRID_W).astype(np.float32)
    ang = (pos * inv_freq[lane % n_freq].astype(np.float32)).astype(np.float32)
    sign = np.where((lane % half) < n_freq, -1.0, 1.0)
    return jnp.asarray(np.cos(ang), F32), jnp.asarray(np.sin(ang) * sign, F32)


def _rwkv_scan(u, state0, params, rotary, emit_y):
    B, T, _ = u.shape
    C = CHUNK
    n_chunks = T // C
    halo = 16
    per = C // halo
    mu, w0, wup, a0, aup, k_k, k_a, r_k = params

    def cidx(d, c):
        return c + d * (n_chunks - 1 - 2 * c)

    def main(part):
        return pl.BlockSpec((None, C, D_MODEL), lambda b, d, c: (b, cidx(d, c), part))

    def prev(part):
        return pl.BlockSpec((None, halo, D_MODEL),
                            lambda b, d, c: (b, jnp.maximum(cidx(d, c) * per - 1, 0), part))

    def nxt(part):
        return pl.BlockSpec((None, halo, D_MODEL),
                            lambda b, d, c: (b, jnp.minimum((cidx(d, c) + 1) * per, T // halo - 1), part))

    def whole(shape):
        return pl.BlockSpec(shape, lambda b, d, c: (0,) * len(shape))

    def per_dir(shape):
        return pl.BlockSpec((None,) + shape, lambda b, d, c: (d,) + (0,) * len(shape))

    cos_t, sin_t = _rope_tables(T)
    tab = pl.BlockSpec((C, LANES), lambda b, d, c: (cidx(d, c), 0))
    state = pl.BlockSpec((None, None, HEAD_DIM, D_MODEL), lambda b, d, c: (b, d, 0, 0))
    seq_out = pl.BlockSpec((None, None, C, D_MODEL), lambda b, d, c: (d, b, cidx(d, c), 0))
    R_B, K_B, V_B = 4, 5, 6
    return pl.pallas_call(
        functools.partial(_rwkv_kernel, n_chunks=n_chunks, rotary=rotary, emit_y=emit_y),
        out_shape=(jax.ShapeDtypeStruct((2, B, T, D_MODEL), BF16),
                   jax.ShapeDtypeStruct((2, B, T, D_MODEL), BF16),
                   jax.ShapeDtypeStruct((B, 2, HEAD_DIM, D_MODEL), F32)),
        grid=(B, 2, n_chunks),
        in_specs=[main(R_B), main(K_B), main(V_B), prev(R_B), prev(K_B), prev(V_B),
                  nxt(R_B), nxt(K_B), nxt(V_B),
                  pl.BlockSpec((None, C, 4 * LORA), lambda b, d, c: (b, cidx(d, c), COL_LORA // (4 * LORA))),
                  tab, tab, whole((3, D_MODEL)), per_dir((1, D_MODEL)), per_dir((4 * LORA, D_MODEL)),
                  per_dir((1, D_MODEL)), per_dir((4 * LORA, D_MODEL)),
                  whole((1, D_MODEL)), whole((1, D_MODEL)), whole((1, D_MODEL)), state],
        out_specs=(seq_out, seq_out, state),
        scratch_shapes=[pltpu.VMEM((HEAD_DIM, D_MODEL), F32)],
        compiler_params=_cparams(("arbitrary", "arbitrary", "arbitrary")),
        name="rwkv7_chunk_scan_rot" if rotary else "rwkv7_chunk_scan_ctx",
    )(u, u, u, u, u, u, u, u, u, u, cos_t, sin_t, mu, w0, wup, a0, aup, k_k, k_a, r_k, state0)


def _final_kernel(x_ref, ya_ref, yf_ref, yb_ref, bf_ref, bb_ref, z_ref, ga_ref, gb_ref, gate_ref,
                  gng_ref, gnb_ref, wa_ref, wb_ref, wo_ref, lng_ref, lnb_ref, o_ref):
    same_head = _block_diag_masks()
    ones_bd = jnp.where(same_head, 1.0, 0.0).astype(BF16)
    y = yf_ref[...].astype(F32) + yb_ref[...].astype(F32)
    mu = _head_sum(y, ones_bd) * (1.0 / HEAD_DIM)
    yc = y - mu
    var = _head_sum(yc * yc, ones_bd) * (1.0 / HEAD_DIM)
    yn = yc * lax.rsqrt(var + GN_EPS) * gng_ref[...] + gnb_ref[...]
    yn = yn + bf_ref[...].astype(F32) + bb_ref[...].astype(F32)
    y_b = (yn * _silu(z_ref[...].astype(F32))).astype(BF16)
    p_a = jnp.dot(ya_ref[...], wa_ref[...], preferred_element_type=F32)
    p_b = jnp.dot(y_b, wb_ref[...], preferred_element_type=F32)
    merged = (jax.nn.sigmoid(ga_ref[...].astype(F32)) * p_a
              + jax.nn.sigmoid(gb_ref[...].astype(F32)) * p_b)
    out = jnp.dot(merged.astype(BF16), wo_ref[...], preferred_element_type=F32)
    t = ALPHA * x_ref[...] + gate_ref[...] * out
    m = jnp.mean(t, axis=-1, keepdims=True)
    tc = t - m
    v = jnp.mean(tc * tc, axis=-1, keepdims=True)
    o_ref[...] = tc * lax.rsqrt(v + LN_EPS) * lng_ref[...] + lnb_ref[...]


def _final_stage(x, y_a, y_dir, bonus_dir, u, mod3, gn_g, gn_b, wa, wb, wo, ln_g, ln_b, tm):
    B, T, _ = x.shape

    def tok(shape_lead=()):
        return pl.BlockSpec((None, tm, D_MODEL), lambda b, i: (b, i, 0))

    def dirs(d):
        return pl.BlockSpec((None, None, tm, D_MODEL), lambda b, i: (d, b, i, 0))

    def ucol(blk):
        return pl.BlockSpec((None, tm, D_MODEL), lambda b, i: (b, i, blk))

    def vec():
        return pl.BlockSpec((1, D_MODEL), lambda b, i: (0, 0))

    def mat():
        return pl.BlockSpec((D_MODEL, D_MODEL), lambda b, i: (0, 0))

    return pl.pallas_call(
        _final_kernel,
        out_shape=jax.ShapeDtypeStruct((B, T, D_MODEL), F32),
        grid=(B, T // tm),
        in_specs=[tok(), tok(), dirs(0), dirs(1), dirs(0), dirs(1), ucol(7), ucol(8), ucol(9),
                  pl.BlockSpec((None, 1, D_MODEL), lambda b, i: (b, 0, 2)),
                  vec(), vec(), mat(), mat(), mat(), vec(), vec()],
        out_specs=tok(),
        compiler_params=_cparams(("arbitrary", "arbitrary")),
        name="readout_merge_out_proj",
    )(x, y_a, y_dir, y_dir, bonus_dir, bonus_dir, u, u, u, mod3, gn_g, gn_b, wa, wb, wo, ln_g, ln_b)


def _layer(x, c, ctx, c_ctx, w_ada, b_ada, w_in, na_rpb, rw_mu, rw_w0, rw_w_up, rw_a0, rw_a_up,
           rw_k_k, rw_k_a, rw_r_k, rw_gn_g, rw_gn_b, w_branch_a, w_branch_b, w_out, ln_g, ln_b):
    B, T, _ = x.shape
    L = ctx.shape[1]
    rows = T // GRID_W

    n_cond = -(-(B + 1) // 8) * 8
    cond = jnp.zeros((n_cond, D_MODEL), F32).at[:B].set(c).at[B].set(c_ctx)
    mod3 = _modulation(cond, w_ada, b_ada).reshape(n_cond, 1, 3 * D_MODEL)

    w_bf = jnp.concatenate(
        [w_in[:, :N_MAIN], w_in[:, N_MAIN + 4 * LORA:], w_in[:, N_MAIN:N_MAIN + 4 * LORA],
         jnp.zeros((D_MODEL, N_PAD - N_PROJ), w_in.dtype)], axis=1).astype(BF16)
    u = _in_projection(x, mod3, w_bf, lambda b: b, tm=min(T, 1024), tn=512)
    u_ctx = _in_projection(ctx, mod3, w_bf, lambda b: B, tm=L, tn=512)

    y_a = _neighbourhood_attention(u, u_ctx, _na_bias_table(na_rpb, rows))

    def lora_pad(w_up, base):
        out = jnp.zeros((2, 4 * LORA, D_MODEL), F32)
        for d in range(2):
            out = out.at[d, base + d * LORA: base + (d + 1) * LORA].set(w_up[d])
        return out.astype(BF16)

    params = (rw_mu, rw_w0.reshape(2, 1, D_MODEL), lora_pad(rw_w_up, 0), rw_a0.reshape(2, 1, D_MODEL),
              lora_pad(rw_a_up, 2 * LORA), rw_k_k.reshape(1, D_MODEL), rw_k_a.reshape(1, D_MODEL),
              rw_r_k.reshape(1, D_MODEL))
    state0 = jnp.zeros((B, 2, HEAD_DIM, D_MODEL), F32)
    _, _, state_c = _rwkv_scan(u_ctx, state0, params, rotary=False, emit_y=False)
    y_dir, bonus_dir, _ = _rwkv_scan(u, state_c, params, rotary=True, emit_y=True)

    return _final_stage(x, y_a, y_dir, bonus_dir, u, mod3, rw_gn_g.reshape(1, D_MODEL),
                        rw_gn_b.reshape(1, D_MODEL), w_branch_a.astype(BF16), w_branch_b.astype(BF16),
                        w_out.astype(BF16), ln_g.reshape(1, D_MODEL), ln_b.reshape(1, D_MODEL),
                        tm=min(T, 512))


def kernel(x, c, ctx, c_ctx, w_ada, b_ada, w_in, na_rpb, rw_mu, rw_w0, rw_w_up, rw_a0, rw_a_up, rw_k_k, rw_k_a, rw_r_k, rw_gn_g, rw_gn_b, w_branch_a, w_branch_b, w_out, ln_g, ln_b):
    assert w_ada.shape[0] == DEPTH
    return _layer(x, c, ctx, c_ctx, w_ada[0], b_ada[0], w_in[0], na_rpb[0], rw_mu[0], rw_w0[0],
                  rw_w_up[0], rw_a0[0], rw_a_up[0], rw_k_k[0], rw_k_a[0], rw_r_k[0], rw_gn_g[0],
                  rw_gn_b[0], w_branch_a[0], w_branch_b[0], w_out[0], ln_g[0], ln_b[0])
```

```python
import functools

import numpy as np
import jax
import jax.numpy as jnp
from jax import lax
from jax.experimental import pallas as pl
from jax.experimental.pallas import tpu as pltpu

F32 = jnp.float32
BF16 = jnp.bfloat16
HIGHEST = lax.Precision.HIGHEST

D_MODEL = 1024
GRID_W = 64
HEADS = 16
HEAD_DIM = 64
NA_MAX_ROWS = 8
NA_COLS = 16
LORA = 64
DEPTH = 1
ROPE_THETA = 10000.0
LN_EPS = 1e-5
GN_EPS = 64e-5
ALPHA = (2 * DEPTH) ** 0.25

LANES = 128
GROUP = 256
HEADS_PER_GROUP = GROUP // HEAD_DIM
N_GROUPS = D_MODEL // GROUP
CHUNK = 64

N_MAIN = 8 * D_MODEL
COL_MG = N_MAIN
COL_LORA = N_MAIN + 2 * D_MODEL
N_PROJ = COL_LORA + 4 * LORA
N_PAD = 21 * 512
VMEM_LIMIT = 56 * 1024 * 1024


def _cparams(sem):
    return pltpu.CompilerParams(dimension_semantics=sem, vmem_limit_bytes=VMEM_LIMIT)


def _silu(x):
    return x * jax.nn.sigmoid(x)


def _mod_kernel(c_ref, w_ref, b_ref, o_ref):
    s = _silu(c_ref[...])
    o_ref[...] = jnp.dot(s, w_ref[...], preferred_element_type=F32, precision=HIGHEST) + b_ref[...]


def _modulation(cond, w_ada, b_ada):
    rows = cond.shape[0]
    return pl.pallas_call(
        _mod_kernel,
        out_shape=jax.ShapeDtypeStruct((rows, 3 * D_MODEL), F32),
        grid=(3,),
        in_specs=[pl.BlockSpec((rows, D_MODEL), lambda j: (0, 0)),
                  pl.BlockSpec((D_MODEL, D_MODEL), lambda j: (0, j)),
                  pl.BlockSpec((1, D_MODEL), lambda j: (0, j))],
        out_specs=pl.BlockSpec((rows, D_MODEL), lambda j: (0, j)),
        compiler_params=_cparams(("arbitrary",)),
        name="ada_modulation",
    )(cond, w_ada, b_ada.reshape(1, 3 * D_MODEL))


def _inproj_kernel(x_ref, sh_ref, sc_ref, w_ref, o_ref, h_scr):
    @pl.when(pl.program_id(2) == 0)
    def _():
        x = x_ref[...]
        mu = jnp.mean(x, axis=-1, keepdims=True)
        xc = x - mu
        var = jnp.mean(xc * xc, axis=-1, keepdims=True)
        y = xc * lax.rsqrt(var + LN_EPS)
        h_scr[...] = (y * (1.0 + sc_ref[...]) + sh_ref[...]).astype(BF16)

    o_ref[...] = jnp.dot(h_scr[...], w_ref[...], preferred_element_type=F32).astype(BF16)


def _in_projection(x, mod3, w_bf, row_of_batch, tm, tn):
    B, T, _ = x.shape
    return pl.pallas_call(
        _inproj_kernel,
        out_shape=jax.ShapeDtypeStruct((B, T, N_PAD), BF16),
        grid=(B, T // tm, N_PAD // tn),
        in_specs=[pl.BlockSpec((None, tm, D_MODEL), lambda b, i, j: (b, i, 0)),
                  pl.BlockSpec((None, 1, D_MODEL), lambda b, i, j: (row_of_batch(b), 0, 0)),
                  pl.BlockSpec((None, 1, D_MODEL), lambda b, i, j: (row_of_batch(b), 0, 1)),
                  pl.BlockSpec((D_MODEL, tn), lambda b, i, j: (0, j))],
        out_specs=pl.BlockSpec((None, tm, tn), lambda b, i, j: (b, i, j)),
        scratch_shapes=[pltpu.VMEM((tm, D_MODEL), BF16)],
        compiler_params=_cparams(("arbitrary", "arbitrary", "arbitrary")),
        name="ln_mod_in_proj",
    )(x, mod3, mod3, w_bf)


def _na_kernel(q_ref, k_ref, v_ref, z_ref, kc_ref, vc_ref, bias_ref, o_ref, *, rows):
    kh = min(NA_MAX_ROWS, rows)
    band = kh * GRID_W
    lane = lax.broadcasted_iota(jnp.int32, (1, LANES), 1)
    first_head = lane < HEAD_DIM
    kc = kc_ref[...]
    vc = vc_ref[...]
    nt = (((1,), (1,)), ((), ()))

    def row_body(i, carry):
        rs = jnp.clip(i - kh // 2, 0, rows - kh)
        q0 = pl.multiple_of(i * GRID_W, GRID_W)
        k0 = pl.multiple_of(rs * GRID_W, GRID_W)
        q = q_ref[pl.ds(q0, GRID_W), :] * jnp.asarray(HEAD_DIM ** -0.5, BF16)
        kb = k_ref[pl.ds(k0, band), :]
        vb = v_ref[pl.ds(k0, band), :]
        outs = []
        for h in range(2):
            keep = first_head if h == 0 else jnp.logical_not(first_head)
            qh = jnp.where(keep, q, jnp.zeros_like(q))
            s = lax.dot_general(qh, kb, nt, preferred_element_type=F32) + bias_ref[h, i - rs]
            sc = lax.dot_general(qh, kc, nt, preferred_element_type=F32)
            m = jnp.maximum(jnp.max(s, axis=-1, keepdims=True), jnp.max(sc, axis=-1, keepdims=True))
            p = jnp.exp(s - m)
            pc = jnp.exp(sc - m)
            denom = jnp.sum(p, axis=-1, keepdims=True) + jnp.sum(pc, axis=-1, keepdims=True)
            o = (jnp.dot(p.astype(BF16), vb, preferred_element_type=F32)
                 + jnp.dot(pc.astype(BF16), vc, preferred_element_type=F32))
            outs.append(o / denom)
        o = jnp.where(first_head, outs[0], outs[1])
        z = z_ref[pl.ds(q0, GRID_W), :].astype(F32)
        o_ref[pl.ds(q0, GRID_W), :] = (o * _silu(z)).astype(BF16)
        return carry

    lax.fori_loop(0, rows, row_body, 0)


def _na_bias_table(rpb, rows):
    kh = min(NA_MAX_ROWS, rows)
    n_dj = 2 * NA_COLS - 1
    cols = np.arange(GRID_W)
    cstart = np.clip(cols - NA_COLS // 2, 0, GRID_W - NA_COLS)
    col_mask = (cols[None, :] >= cstart[:, None]) & (cols[None, :] < cstart[:, None] + NA_COLS)
    dj = np.clip(cols[None, :] - cols[:, None] + NA_COLS - 1, 0, n_dj - 1)
    onehot = (np.arange(n_dj)[:, None] == dj.reshape(1, -1)).astype(np.float32)
    exp = jnp.dot(rpb.reshape(-1, n_dj), jnp.asarray(onehot), precision=HIGHEST)
    exp = exp.reshape(HEADS, 2 * NA_MAX_ROWS - 1, GRID_W, GRID_W)
    exp = jnp.where(col_mask[None, None], exp, -jnp.inf)
    lo = NA_MAX_ROWS - 1
    tab = jnp.stack([exp[:, lo - o: lo - o + kh] for o in range(kh)], axis=1)
    return tab.transpose(0, 1, 3, 2, 4).reshape(HEADS, kh, GRID_W, kh * GRID_W)


def _neighbourhood_attention(u, u_ctx, bias_tab):
    B, T, _ = u.shape
    L = u_ctx.shape[1]
    rows = T // GRID_W
    kh = min(NA_MAX_ROWS, rows)
    blk = D_MODEL // LANES

    def col(part):
        return pl.BlockSpec((None, T, LANES), lambda b, g: (b, 0, part * blk + g))

    def col_ctx(part):
        return pl.BlockSpec((None, L, LANES), lambda b, g: (b, 0, part * blk + g))

    return pl.pallas_call(
        functools.partial(_na_kernel, rows=rows),
        out_shape=jax.ShapeDtypeStruct((B, T, D_MODEL), BF16),
        grid=(B, HEADS // 2),
        in_specs=[col(0), col(1), col(2), col(3), col_ctx(1), col_ctx(2),
                  pl.BlockSpec((2, kh, GRID_W, kh * GRID_W), lambda b, g: (g, 0, 0, 0))],
        out_specs=pl.BlockSpec((None, T, LANES), lambda b, g: (b, 0, g)),
        compiler_params=_cparams(("arbitrary", "arbitrary")),
        name="neighbourhood_attention",
    )(u, u, u, u, u_ctx, u_ctx, bias_tab)


def _block_diag_masks():
    r = lax.broadcasted_iota(jnp.int32, (GROUP, GROUP), 0)
    c = lax.broadcasted_iota(jnp.int32, (GROUP, GROUP), 1)
    return (r // HEAD_DIM) == (c // HEAD_DIM)


def _head_sum(x, ones_bd):
    xb = x.astype(BF16)
    parts = [jnp.dot(xb[:, g * GROUP:(g + 1) * GROUP], ones_bd, preferred_element_type=F32)
             for g in range(N_GROUPS)]
    return jnp.concatenate(parts, axis=1)


def _split_head_sum(x, ones_bd):
    hi = x.astype(BF16)
    return _head_sum(hi, ones_bd) + _head_sum(x - hi.astype(F32), ones_bd)


def _rwkv_prepare(d, cidx, n_chunks, refs, prm, ones_bd, rotary):
    (r_ref, k_ref, v_ref, rp_ref, kp_ref, vp_ref, rn_ref, kn_ref, vn_ref, lo_ref, cos_ref, sin_ref,
     bonus_ref) = refs
    mu_ref, w0_ref, wup_ref, a0_ref, aup_ref, kk_ref, ka_ref, rk_ref = prm
    C = CHUNK
    row = lax.broadcasted_iota(jnp.int32, (C, 1), 0)

    def shifted(x_ref, p_ref, n_ref, mu):
        x = x_ref[...].astype(F32)
        prev_row = jnp.where(cidx > 0, p_ref[...].astype(F32)[15:16, :], 0.0)
        next_row = jnp.where(cidx < n_chunks - 1, n_ref[...].astype(F32)[0:1, :], 0.0)
        x_prev = jnp.where(row == 0, prev_row, pltpu.roll(x, 1, axis=0))
        x_next = jnp.where(row == C - 1, next_row, pltpu.roll(x, C - 1, axis=0))
        return x + mu * (0.5 * (x_prev + x_next) - x)

    r_s = shifted(r_ref, rp_ref, rn_ref, mu_ref[0:1, :])
    k_s = shifted(k_ref, kp_ref, kn_ref, mu_ref[1:2, :])
    v_s = shifted(v_ref, vp_ref, vn_ref, mu_ref[2:3, :])

    if rotary:
        lane = lax.broadcasted_iota(jnp.int32, (1, D_MODEL), 1)
        low = (lane % 32) < 16
        cos_t = jnp.tile(cos_ref[...], (1, D_MODEL // LANES))
        sin_t = jnp.tile(sin_ref[...], (1, D_MODEL // LANES))

        def rope(x):
            partner = jnp.where(low, pltpu.roll(x, D_MODEL - 16, axis=1), pltpu.roll(x, 16, axis=1))
            return x * cos_t + partner * sin_t

        r_s = rope(r_s)
        k_s = rope(k_s)

    lo = lo_ref[...]
    lw = w0_ref[d:d + 1, :] + jnp.dot(jnp.tanh(lo[:, :2 * LORA].astype(F32)).astype(BF16), wup_ref[d],
                                     preferred_element_type=F32)
    ld = -jnp.exp(-jax.nn.softplus(-lw) - 0.5)
    a = jax.nn.sigmoid(a0_ref[d:d + 1, :] + jnp.dot(lo[:, 2 * LORA:], aup_ref[d], preferred_element_type=F32))

    kk = k_s * kk_ref[...]
    kk = kk / jnp.maximum(jnp.sqrt(_head_sum(kk * kk, ones_bd)), 1e-12)
    k_dir = k_s * (1.0 + (a - 1.0) * ka_ref[...])
    b_vec = kk * a
    bonus_ref[...] = (_head_sum(r_s * rk_ref[...] * k_dir, ones_bd) * v_s).astype(bonus_ref.dtype)

    tt = lax.broadcasted_iota(jnp.int32, (C, C), 0)
    ss = lax.broadcasted_iota(jnp.int32, (C, C), 1)
    tri = jnp.where((ss <= tt) if d == 0 else (ss >= tt), 1.0, 0.0).astype(BF16)
    ld_hi = ld.astype(BF16)
    ld_lo = (ld - ld_hi.astype(F32)).astype(BF16)
    cl = (jnp.dot(tri, ld_hi, preferred_element_type=F32) + jnp.dot(tri, ld_lo, preferred_element_type=F32))
    cl_tot = cl[C - 1:C, :] if d == 0 else cl[0:1, :]
    e_neg = jnp.exp(-cl)
    e_rest = jnp.exp(cl_tot - cl)
    return dict(
        a=(-kk * jnp.exp(cl - ld)).astype(BF16), r=(r_s * jnp.exp(cl)).astype(BF16),
        b=(b_vec * e_neg).astype(BF16), k=(k_dir * e_neg).astype(BF16),
        bh=(b_vec * e_rest).astype(BF16), kh=(k_dir * e_rest).astype(BF16),
        v=v_s.astype(BF16), pc=jnp.exp(cl_tot))


def _rwkv_kernel(*refs, n_chunks, rotary, emit_y):
    per_dir = 12
    in_f, in_b = refs[:per_dir], refs[per_dir:2 * per_dir]
    prm = refs[2 * per_dir:2 * per_dir + 8]
    s0_ref = refs[2 * per_dir + 8]
    yf_ref, yb_ref, bonf_ref, bonb_ref, sout_ref, h_scr = refs[2 * per_dir + 9:]
    C = CHUNK
    c = pl.program_id(1)

    @pl.when(c == 0)
    def _():
        h_scr[:, :D_MODEL] = s0_ref[0]
        h_scr[:, D_MODEL:] = s0_ref[1]

    same_head = _block_diag_masks()
    ones_bd = jnp.where(same_head, 1.0, 0.0).astype(BF16)
    ops = [_rwkv_prepare(0, c, n_chunks, in_f + (bonf_ref,), prm, ones_bd, rotary),
           _rwkv_prepare(1, n_chunks - 1 - c, n_chunks, in_b + (bonb_ref,), prm, ones_bd, rotary)]

    n_g = 2 * N_GROUPS
    dirs = [g // N_GROUPS for g in range(n_g)]

    def grp(name):
        return [ops[dirs[g]][name][:, (g % N_GROUPS) * GROUP:(g % N_GROUPS + 1) * GROUP] for g in range(n_g)]

    at, rt, bt, kt, bh, kh, vv, pc = (grp(n) for n in ("a", "r", "b", "k", "bh", "kh", "v", "pc"))
    t_i = lax.broadcasted_iota(jnp.int32, (C, GROUP), 0)
    s_i = lax.broadcasted_iota(jnp.int32, (C, GROUP), 1) % C
    before = [s_i < t_i, s_i > t_i]
    upto = [s_i <= t_i, s_i >= t_i]
    eye = jnp.where(t_i == s_i, 1.0, 0.0).astype(F32)
    nt = (((1,), (1,)), ((), ()))
    tn = (((0,), (0,)), ((), ()))

    def bd(x):
        return jnp.where(same_head, jnp.tile(x, (HEADS_PER_GROUP, 1)), jnp.zeros((), x.dtype))

    def hdot(a_side, x):
        return jnp.dot(a_side, bd(x), preferred_element_type=F32)

    def masked(m, keep):
        return jnp.where(keep, m, 0.0).astype(BF16)

    if emit_y:
        lhs = [jnp.concatenate([at[g], rt[g]], axis=0) for g in range(n_g)]
    else:
        lhs = at
    m_b = [lax.dot_general(lhs[g], bd(bt[g]), nt, preferred_element_type=F32) for g in range(n_g)]
    m_k = [lax.dot_general(lhs[g], bd(kt[g]), nt, preferred_element_type=F32) for g in range(n_g)]
    a_ab = [masked(m_b[g][:C], before[dirs[g]]) for g in range(n_g)]
    a_ak = [masked(m_k[g][:C], before[dirs[g]]) for g in range(n_g)]
    if emit_y:
        a_rb = [masked(m_b[g][C:], upto[dirs[g]]) for g in range(n_g)]
        a_rk = [masked(m_k[g][C:], upto[dirs[g]]) for g in range(n_g)]
        w_v = [hdot(jnp.concatenate([a_ak[g], a_rk[g]], axis=0), vv[g]) for g in range(n_g)]
    else:
        w_v = [hdot(a_ak[g], vv[g]) for g in range(n_g)]

    eye_bf = eye.astype(BF16)
    inv = [eye_bf + a_ab[g] for g in range(n_g)]
    npow = a_ab
    for _ in range(C.bit_length() - 2):
        npow = [hdot(npow[g], npow[g]).astype(BF16) for g in range(n_g)]
        inv = [hdot(inv[g], eye_bf + npow[g]).astype(BF16) for g in range(n_g)]

    a_p = [hdot(inv[g], at[g]).astype(BF16) for g in range(n_g)]
    u0 = [hdot(inv[g], w_v[g][:C].astype(BF16)) for g in range(n_g)]

    h_old = [h_scr[:, g * GROUP:(g + 1) * GROUP] for g in range(n_g)]
    decay_rows = [(eye * pc[g]).astype(BF16) for g in range(n_g)]
    stack = [jnp.concatenate([a_p[g]] + ([rt[g]] if emit_y else []) + [decay_rows[g]], axis=0)
             for g in range(n_g)]
    s1 = [hdot(stack[g], h_old[g].astype(BF16)) for g in range(n_g)]
    u = [(s1[g][:C] + u0[g]).astype(BF16) for g in range(n_g)]
    if emit_y:
        y_u = [hdot(a_rb[g], u[g]) for g in range(n_g)]
        for g in range(n_g):
            y_ref = yf_ref if dirs[g] == 0 else yb_ref
            lanes = slice((g % N_GROUPS) * GROUP, (g % N_GROUPS + 1) * GROUP)
            y_ref[:, lanes] = (s1[g][C:2 * C] + y_u[g] + w_v[g][C:]).astype(y_ref.dtype)
    else:
        yf_ref[...] = jnp.zeros(yf_ref.shape, yf_ref.dtype)
        yb_ref[...] = jnp.zeros(yb_ref.shape, yb_ref.dtype)

    full = [lax.dot_general(jnp.concatenate([bh[g], kh[g]], axis=0),
                            jnp.concatenate([u[g], vv[g]], axis=0), tn, preferred_element_type=F32)
            for g in range(n_g)]
    for g in range(n_g):
        keep = jnp.where(same_head, full[g], 0.0)
        upd = keep[0:HEAD_DIM]
        for j in range(1, HEADS_PER_GROUP):
            upd = upd + keep[j * HEAD_DIM:(j + 1) * HEAD_DIM]
        h_scr[:, g * GROUP:(g + 1) * GROUP] = s1[g][-C:] + upd

    @pl.when(c == n_chunks - 1)
    def _():
        sout_ref[0] = h_scr[:, :D_MODEL]
        sout_ref[1] = h_scr[:, D_MODEL:]


def _rope_tables(T):
    half = HEAD_DIM // 2
    n_freq = half // 2
    t = np.arange(T)[:, None]
    lane = np.arange(LANES)[None, :]
    inv_freq = ROPE_THETA ** (-(np.arange(n_freq, dtype=np.float32)) / n_freq)
    pos = np.where((lane % HEAD_DIM) < half, t // GRID_W, t % GRID_W).astype(np.float32)
    ang = (pos * inv_freq[lane % n_freq].astype(np.float32)).astype(np.float32)
    sign = np.where((lane % half) < n_freq, -1.0, 1.0)
    return jnp.asarray(np.cos(ang), F32), jnp.asarray(np.sin(ang) * sign, F32)


def _rwkv_scan(u, state0, params, rotary, emit_y):
    B, T, _ = u.shape
    C = CHUNK
    n_chunks = T // C
    halo = 16
    per = C // halo

    def chunk_of(d, c):
        return c if d == 0 else n_chunks - 1 - c

    def specs(d):
        def main(part):
            return pl.BlockSpec((None, C, D_MODEL), lambda b, c: (b, chunk_of(d, c), part))

        def prev(part):
            return pl.BlockSpec((None, halo, D_MODEL),
                                lambda b, c: (b, jnp.maximum(chunk_of(d, c) * per - 1, 0), part))

        def nxt(part):
            return pl.BlockSpec((None, halo, D_MODEL),
                                lambda b, c: (b, jnp.minimum((chunk_of(d, c) + 1) * per, T // halo - 1), part))

        r_blk, k_blk, v_blk = 4, 5, 6
        lora = pl.BlockSpec((None, C, 4 * LORA), lambda b, c: (b, chunk_of(d, c), COL_LORA // (4 * LORA)))
        tab = pl.BlockSpec((C, LANES), lambda b, c: (chunk_of(d, c), 0))
        return [main(r_blk), main(k_blk), main(v_blk), prev(r_blk), prev(k_blk), prev(v_blk),
                nxt(r_blk), nxt(k_blk), nxt(v_blk), lora, tab, tab]

    def whole(shape):
        return pl.BlockSpec(shape, lambda b, c: (0,) * len(shape))

    cos_t, sin_t = _rope_tables(T)
    state = pl.BlockSpec((None, 2, HEAD_DIM, D_MODEL), lambda b, c: (b, 0, 0, 0))
    param_specs = [whole((3, D_MODEL)), whole((2, D_MODEL)), whole((2, 2 * LORA, D_MODEL)),
                   whole((2, D_MODEL)), whole((2, 2 * LORA, D_MODEL)),
                   whole((1, D_MODEL)), whole((1, D_MODEL)), whole((1, D_MODEL))]
    seq = jax.ShapeDtypeStruct((B, T, D_MODEL), BF16)
    out_f = pl.BlockSpec((None, C, D_MODEL), lambda b, c: (b, chunk_of(0, c), 0))
    out_b = pl.BlockSpec((None, C, D_MODEL), lambda b, c: (b, chunk_of(1, c), 0))
    seq_in = (u,) * 10 + (cos_t, sin_t)
    return pl.pallas_call(
        functools.partial(_rwkv_kernel, n_chunks=n_chunks, rotary=rotary, emit_y=emit_y),
        out_shape=(seq, seq, seq, seq, jax.ShapeDtypeStruct((B, 2, HEAD_DIM, D_MODEL), F32)),
        grid=(B, n_chunks),
        in_specs=specs(0) + specs(1) + param_specs + [state],
        out_specs=(out_f, out_b, out_f, out_b, state),
        scratch_shapes=[pltpu.VMEM((HEAD_DIM, 2 * D_MODEL), F32)],
        compiler_params=_cparams(("arbitrary", "arbitrary")),
        name="rwkv7_chunk_scan_rot" if rotary else "rwkv7_chunk_scan_ctx",
    )(*seq_in, *seq_in, *params, state0)


def _final_kernel(x_ref, ya_ref, yf_ref, yb_ref, bf_ref, bb_ref, z_ref, ga_ref, gb_ref, gate_ref,
                  gng_ref, gnb_ref, wa_ref, wb_ref, wo_ref, lng_ref, lnb_ref, o_ref):
    same_head = _block_diag_masks()
    ones_bd = jnp.where(same_head, 1.0, 0.0).astype(BF16)
    y = yf_ref[...].astype(F32) + yb_ref[...].astype(F32)
    mu = _split_head_sum(y, ones_bd) * (1.0 / HEAD_DIM)
    yc = y - mu
    var = _split_head_sum(yc * yc, ones_bd) * (1.0 / HEAD_DIM)
    yn = yc * lax.rsqrt(var + GN_EPS) * gng_ref[...] + gnb_ref[...]
    yn = yn + bf_ref[...].astype(F32) + bb_ref[...].astype(F32)
    y_b = (yn * _silu(z_ref[...].astype(F32))).astype(BF16)
    p_a = jnp.dot(ya_ref[...], wa_ref[...], preferred_element_type=F32)
    p_b = jnp.dot(y_b, wb_ref[...], preferred_element_type=F32)
    merged = (jax.nn.sigmoid(ga_ref[...].astype(F32)) * p_a
              + jax.nn.sigmoid(gb_ref[...].astype(F32)) * p_b)
    out = jnp.dot(merged.astype(BF16), wo_ref[...], preferred_element_type=F32)
    t = ALPHA * x_ref[...] + gate_ref[...] * out
    m = jnp.mean(t, axis=-1, keepdims=True)
    tc = t - m
    v = jnp.mean(tc * tc, axis=-1, keepdims=True)
    o_ref[...] = tc * lax.rsqrt(v + LN_EPS) * lng_ref[...] + lnb_ref[...]


def _final_stage(x, y_a, y_f, y_b, bonus_f, bonus_b, u, mod3, gn_g, gn_b, wa, wb, wo, ln_g, ln_b, tm):
    B, T, _ = x.shape

    def tok():
        return pl.BlockSpec((None, tm, D_MODEL), lambda b, i: (b, i, 0))

    def ucol(blk):
        return pl.BlockSpec((None, tm, D_MODEL), lambda b, i: (b, i, blk))

    def vec():
        return pl.BlockSpec((1, D_MODEL), lambda b, i: (0, 0))

    def mat():
        return pl.BlockSpec((D_MODEL, D_MODEL), lambda b, i: (0, 0))

    return pl.pallas_call(
        _final_kernel,
        out_shape=jax.ShapeDtypeStruct((B, T, D_MODEL), F32),
        grid=(B, T // tm),
        in_specs=[tok(), tok(), tok(), tok(), tok(), tok(), ucol(7), ucol(8), ucol(9),
                  pl.BlockSpec((None, 1, D_MODEL), lambda b, i: (b, 0, 2)),
                  vec(), vec(), mat(), mat(), mat(), vec(), vec()],
        out_specs=tok(),
        compiler_params=_cparams(("arbitrary", "arbitrary")),
        name="readout_merge_out_proj",
    )(x, y_a, y_f, y_b, bonus_f, bonus_b, u, u, u, mod3, gn_g, gn_b, wa, wb, wo, ln_g, ln_b)


def _layer(x, c, ctx, c_ctx, w_ada, b_ada, w_in, na_rpb, rw_mu, rw_w0, rw_w_up, rw_a0, rw_a_up,
           rw_k_k, rw_k_a, rw_r_k, rw_gn_g, rw_gn_b, w_branch_a, w_branch_b, w_out, ln_g, ln_b):
    B, T, _ = x.shape
    L = ctx.shape[1]
    rows = T // GRID_W

    n_cond = -(-(B + 1) // 8) * 8
    cond = jnp.zeros((n_cond, D_MODEL), F32).at[:B].set(c).at[B].set(c_ctx)
    mod3 = _modulation(cond, w_ada, b_ada).reshape(n_cond, 1, 3 * D_MODEL)

    w_bf = jnp.concatenate(
        [w_in[:, :N_MAIN], w_in[:, N_MAIN + 4 * LORA:], w_in[:, N_MAIN:N_MAIN + 4 * LORA],
         jnp.zeros((D_MODEL, N_PAD - N_PROJ), w_in.dtype)], axis=1).astype(BF16)
    u = _in_projection(x, mod3, w_bf, lambda b: b, tm=min(T, 1024), tn=512)
    u_ctx = _in_projection(ctx, mod3, w_bf, lambda b: B, tm=L, tn=512)

    y_a = _neighbourhood_attention(u, u_ctx, _na_bias_table(na_rpb, rows))

    def lora_pad(w_up):
        out = jnp.zeros((2, 2 * LORA, D_MODEL), F32)
        for d in range(2):
            out = out.at[d, d * LORA:(d + 1) * LORA].set(w_up[d])
        return out.astype(BF16)

    params = (rw_mu, rw_w0, lora_pad(rw_w_up), rw_a0, lora_pad(rw_a_up), rw_k_k.reshape(1, D_MODEL),
              rw_k_a.reshape(1, D_MODEL), rw_r_k.reshape(1, D_MODEL))
    state0 = jnp.zeros((B, 2, HEAD_DIM, D_MODEL), F32)
    state_c = _rwkv_scan(u_ctx, state0, params, rotary=False, emit_y=False)[-1]
    y_f, y_b, bonus_f, bonus_b, _ = _rwkv_scan(u, state_c, params, rotary=True, emit_y=True)

    return _final_stage(x, y_a, y_f, y_b, bonus_f, bonus_b, u, mod3, rw_gn_g.reshape(1, D_MODEL),
                        rw_gn_b.reshape(1, D_MODEL), w_branch_a.astype(BF16), w_branch_b.astype(BF16),
                        w_out.astype(BF16), ln_g.reshape(1, D_MODEL), ln_b.reshape(1, D_MODEL),
                        tm=min(T, 512))


def kernel(x, c, ctx, c_ctx, w_ada, b_ada, w_in, na_rpb, rw_mu, rw_w0, rw_w_up, rw_a0, rw_a_up, rw_k_k, rw_k_a, rw_r_k, rw_gn_g, rw_gn_b, w_branch_a, w_branch_b, w_out, ln_g, ln_b):
    assert w_ada.shape[0] == DEPTH
    return _layer(x, c, ctx, c_ctx, w_ada[0], b_ada[0], w_in[0], na_rpb[0], rw_mu[0], rw_w0[0],
                  rw_w_up[0], rw_a0[0], rw_a_up[0], rw_k_k[0], rw_k_a[0], rw_r_k[0], rw_gn_g[0],
                  rw_gn_b[0], w_branch_a[0], w_branch_b[0], w_out[0], ln_g[0], ln_b[0])
```

```python
import functools

import numpy as np
import jax
import jax.numpy as jnp
from jax import lax
from jax.experimental import pallas as pl
from jax.experimental.pallas import tpu as pltpu

F32 = jnp.float32
BF16 = jnp.bfloat16
HIGHEST = lax.Precision.HIGHEST

D_MODEL = 1024
GRID_W = 64
HEADS = 16
HEAD_DIM = 64
NA_MAX_ROWS = 8
NA_COLS = 16
LORA = 64
DEPTH = 1
ROPE_THETA = 10000.0
LN_EPS = 1e-5
GN_EPS = 64e-5
ALPHA = (2 * DEPTH) ** 0.25

LANES = 128
GROUP = 256
HEADS_PER_GROUP = GROUP // HEAD_DIM
N_GROUPS = D_MODEL // GROUP
CHUNK = 64

N_MAIN = 8 * D_MODEL
COL_MG = N_MAIN
COL_LORA = N_MAIN + 2 * D_MODEL
N_PROJ = COL_LORA + 4 * LORA
N_PAD = 21 * 512
VMEM_LIMIT = 56 * 1024 * 1024


def _cparams(sem):
    return pltpu.CompilerParams(dimension_semantics=sem, vmem_limit_bytes=VMEM_LIMIT)


def _silu(x):
    return x * jax.nn.sigmoid(x)


def _mod_kernel(c_ref, w_ref, b_ref, o_ref):
    s = _silu(c_ref[...])
    o_ref[...] = jnp.dot(s, w_ref[...], preferred_element_type=F32, precision=HIGHEST) + b_ref[...]


def _modulation(cond, w_ada, b_ada):
    rows = cond.shape[0]
    return pl.pallas_call(
        _mod_kernel,
        out_shape=jax.ShapeDtypeStruct((rows, 3 * D_MODEL), F32),
        grid=(3,),
        in_specs=[pl.BlockSpec((rows, D_MODEL), lambda j: (0, 0)),
                  pl.BlockSpec((D_MODEL, D_MODEL), lambda j: (0, j)),
                  pl.BlockSpec((1, D_MODEL), lambda j: (0, j))],
        out_specs=pl.BlockSpec((rows, D_MODEL), lambda j: (0, j)),
        compiler_params=_cparams(("arbitrary",)),
        name="ada_modulation",
    )(cond, w_ada, b_ada.reshape(1, 3 * D_MODEL))


def _inproj_kernel(x_ref, sh_ref, sc_ref, w_ref, o_ref, h_scr):
    @pl.when(pl.program_id(2) == 0)
    def _():
        x = x_ref[...]
        mu = jnp.mean(x, axis=-1, keepdims=True)
        xc = x - mu
        var = jnp.mean(xc * xc, axis=-1, keepdims=True)
        y = xc * lax.rsqrt(var + LN_EPS)
        h_scr[...] = (y * (1.0 + sc_ref[...]) + sh_ref[...]).astype(BF16)

    o_ref[...] = jnp.dot(h_scr[...], w_ref[...], preferred_element_type=F32).astype(BF16)


def _in_projection(x, mod3, w_bf, row_of_batch, tm, tn):
    B, T, _ = x.shape
    return pl.pallas_call(
        _inproj_kernel,
        out_shape=jax.ShapeDtypeStruct((B, T, N_PAD), BF16),
        grid=(B, T // tm, N_PAD // tn),
        in_specs=[pl.BlockSpec((None, tm, D_MODEL), lambda b, i, j: (b, i, 0)),
                  pl.BlockSpec((None, 1, D_MODEL), lambda b, i, j: (row_of_batch(b), 0, 0)),
                  pl.BlockSpec((None, 1, D_MODEL), lambda b, i, j: (row_of_batch(b), 0, 1)),
                  pl.BlockSpec((D_MODEL, tn), lambda b, i, j: (0, j))],
        out_specs=pl.BlockSpec((None, tm, tn), lambda b, i, j: (b, i, j)),
        scratch_shapes=[pltpu.VMEM((tm, D_MODEL), BF16)],
        compiler_params=_cparams(("arbitrary", "arbitrary", "arbitrary")),
        name="ln_mod_in_proj",
    )(x, mod3, mod3, w_bf)


NA_HEADS_PER_STEP = GROUP // HEAD_DIM


def _na_kernel(q_ref, k_ref, v_ref, z_ref, kc_ref, vc_ref, bias_ref, o_ref, *, rows):
    kh = min(NA_MAX_ROWS, rows)
    band = kh * GRID_W
    nh = NA_HEADS_PER_STEP
    lane = lax.broadcasted_iota(jnp.int32, (1, GROUP), 1)
    head_lanes = [(lane // HEAD_DIM) == h for h in range(nh)]
    kc = kc_ref[...]
    vc = vc_ref[...]
    nt = (((1,), (1,)), ((), ()))

    def row_body(i, carry):
        rs = jnp.clip(i - kh // 2, 0, rows - kh)
        q0 = pl.multiple_of(i * GRID_W, GRID_W)
        k0 = pl.multiple_of(rs * GRID_W, GRID_W)
        q = q_ref[pl.ds(q0, GRID_W), :] * jnp.asarray(HEAD_DIM ** -0.5, BF16)
        kb = k_ref[pl.ds(k0, band), :]
        vb = v_ref[pl.ds(k0, band), :]
        qs = jnp.concatenate([jnp.where(head_lanes[h], q, jnp.zeros_like(q)) for h in range(nh)], axis=0)
        s = lax.dot_general(qs, kb, nt, preferred_element_type=F32) + bias_ref[i - rs]
        sc = lax.dot_general(qs, kc, nt, preferred_element_type=F32)
        m = jnp.maximum(jnp.max(s, axis=-1, keepdims=True), jnp.max(sc, axis=-1, keepdims=True))
        p = jnp.exp(s - m)
        pc = jnp.exp(sc - m)
        denom = jnp.sum(p, axis=-1, keepdims=True) + jnp.sum(pc, axis=-1, keepdims=True)
        o_all = (jnp.dot(p.astype(BF16), vb, preferred_element_type=F32)
                 + jnp.dot(pc.astype(BF16), vc, preferred_element_type=F32)) / denom
        o = jnp.where(head_lanes[0], o_all[0:GRID_W], 0.0)
        for h in range(1, nh):
            o = o + jnp.where(head_lanes[h], o_all[h * GRID_W:(h + 1) * GRID_W], 0.0)
        z = z_ref[pl.ds(q0, GRID_W), :].astype(F32)
        o_ref[pl.ds(q0, GRID_W), :] = (o * _silu(z)).astype(BF16)
        return carry

    lax.fori_loop(0, rows, row_body, 0, unroll=4)


def _na_bias_table(rpb, rows):
    kh = min(NA_MAX_ROWS, rows)
    nh = NA_HEADS_PER_STEP
    n_dj = 2 * NA_COLS - 1
    cols = np.arange(GRID_W)
    cstart = np.clip(cols - NA_COLS // 2, 0, GRID_W - NA_COLS)
    col_mask = (cols[None, :] >= cstart[:, None]) & (cols[None, :] < cstart[:, None] + NA_COLS)
    dj = np.clip(cols[None, :] - cols[:, None] + NA_COLS - 1, 0, n_dj - 1)
    onehot = (np.arange(n_dj)[:, None] == dj.reshape(1, -1)).astype(np.float32)
    exp = jnp.dot(rpb.reshape(-1, n_dj), jnp.asarray(onehot), precision=HIGHEST)
    exp = exp.reshape(HEADS, 2 * NA_MAX_ROWS - 1, GRID_W, GRID_W)
    exp = jnp.where(col_mask[None, None], exp, -jnp.inf)
    lo = NA_MAX_ROWS - 1
    tab = jnp.stack([exp[:, lo - o: lo - o + kh] for o in range(kh)], axis=1)
    tab = tab.reshape(HEADS // nh, nh, kh, kh, GRID_W, GRID_W)
    return tab.transpose(0, 2, 1, 4, 3, 5).reshape(HEADS // nh, kh, nh * GRID_W, kh * GRID_W)


def _neighbourhood_attention(u, u_ctx, bias_tab):
    B, T, _ = u.shape
    L = u_ctx.shape[1]
    rows = T // GRID_W
    kh = min(NA_MAX_ROWS, rows)
    blk = D_MODEL // GROUP

    def col(part):
        return pl.BlockSpec((None, T, GROUP), lambda b, g: (b, 0, part * blk + g))

    def col_ctx(part):
        return pl.BlockSpec((None, L, GROUP), lambda b, g: (b, 0, part * blk + g))

    return pl.pallas_call(
        functools.partial(_na_kernel, rows=rows),
        out_shape=jax.ShapeDtypeStruct((B, T, D_MODEL), BF16),
        grid=(B, blk),
        in_specs=[col(0), col(1), col(2), col(3), col_ctx(1), col_ctx(2),
                  pl.BlockSpec((None, kh, NA_HEADS_PER_STEP * GRID_W, kh * GRID_W), lambda b, g: (g, 0, 0, 0))],
        out_specs=pl.BlockSpec((None, T, GROUP), lambda b, g: (b, 0, g)),
        compiler_params=_cparams(("arbitrary", "arbitrary")),
        name="neighbourhood_attention",
    )(u, u, u, u, u_ctx, u_ctx, bias_tab)


def _block_diag_masks():
    r = lax.broadcasted_iota(jnp.int32, (GROUP, GROUP), 0)
    c = lax.broadcasted_iota(jnp.int32, (GROUP, GROUP), 1)
    return (r // HEAD_DIM) == (c // HEAD_DIM)


def _head_sum(x, ones_bd):
    xb = x.astype(BF16)
    parts = [jnp.dot(xb[:, g * GROUP:(g + 1) * GROUP], ones_bd, preferred_element_type=F32)
             for g in range(N_GROUPS)]
    return jnp.concatenate(parts, axis=1)


def _split_head_sum(x, ones_bd):
    hi = x.astype(BF16)
    return _head_sum(hi, ones_bd) + _head_sum(x - hi.astype(F32), ones_bd)


def _rwkv_prepare(d, cidx, n_chunks, refs, prm, ones_bd, rotary):
    (r_ref, k_ref, v_ref, rp_ref, kp_ref, vp_ref, rn_ref, kn_ref, vn_ref, lo_ref, cos_ref, sin_ref,
     bonus_ref) = refs
    mu_ref, w0_ref, wup_ref, a0_ref, aup_ref, kk_ref, ka_ref, rk_ref = prm
    C = CHUNK
    row = lax.broadcasted_iota(jnp.int32, (C, 1), 0)

    def shifted(x_ref, p_ref, n_ref, mu):
        x = x_ref[...].astype(F32)
        prev_row = jnp.where(cidx > 0, p_ref[...].astype(F32)[15:16, :], 0.0)
        next_row = jnp.where(cidx < n_chunks - 1, n_ref[...].astype(F32)[0:1, :], 0.0)
        x_prev = jnp.where(row == 0, prev_row, pltpu.roll(x, 1, axis=0))
        x_next = jnp.where(row == C - 1, next_row, pltpu.roll(x, C - 1, axis=0))
        return x + mu * (0.5 * (x_prev + x_next) - x)

    r_s = shifted(r_ref, rp_ref, rn_ref, mu_ref[0:1, :])
    k_s = shifted(k_ref, kp_ref, kn_ref, mu_ref[1:2, :])
    v_s = shifted(v_ref, vp_ref, vn_ref, mu_ref[2:3, :])

    if rotary:
        lane = lax.broadcasted_iota(jnp.int32, (1, D_MODEL), 1)
        low = (lane % 32) < 16
        cos_t = jnp.tile(cos_ref[...], (1, D_MODEL // LANES))
        sin_t = jnp.tile(sin_ref[...], (1, D_MODEL // LANES))

        def rope(x):
            partner = jnp.where(low, pltpu.roll(x, D_MODEL - 16, axis=1), pltpu.roll(x, 16, axis=1))
            return x * cos_t + partner * sin_t

        r_s = rope(r_s)
        k_s = rope(k_s)

    lo = lo_ref[...]
    lw = w0_ref[d:d + 1, :] + jnp.dot(jnp.tanh(lo[:, :2 * LORA].astype(F32)).astype(BF16), wup_ref[d],
                                     preferred_element_type=F32)
    ld = -jnp.exp(-jax.nn.softplus(-lw) - 0.5)
    a = jax.nn.sigmoid(a0_ref[d:d + 1, :] + jnp.dot(lo[:, 2 * LORA:], aup_ref[d], preferred_element_type=F32))

    kk = k_s * kk_ref[...]
    kk = kk / jnp.maximum(jnp.sqrt(_head_sum(kk * kk, ones_bd)), 1e-12)
    k_dir = k_s * (1.0 + (a - 1.0) * ka_ref[...])
    b_vec = kk * a
    bonus_ref[...] = (_head_sum(r_s * rk_ref[...] * k_dir, ones_bd) * v_s).astype(bonus_ref.dtype)

    tt = lax.broadcasted_iota(jnp.int32, (C, C), 0)
    ss = lax.broadcasted_iota(jnp.int32, (C, C), 1)
    tri = jnp.where((ss <= tt) if d == 0 else (ss >= tt), 1.0, 0.0).astype(BF16)
    ld_hi = ld.astype(BF16)
    ld_lo = (ld - ld_hi.astype(F32)).astype(BF16)
    cl = (jnp.dot(tri, ld_hi, preferred_element_type=F32) + jnp.dot(tri, ld_lo, preferred_element_type=F32))
    cl_tot = cl[C - 1:C, :] if d == 0 else cl[0:1, :]
    e_neg = jnp.exp(-cl)
    e_rest = jnp.exp(cl_tot - cl)
    return dict(
        a=(-kk * jnp.exp(cl - ld)).astype(BF16), r=(r_s * jnp.exp(cl)).astype(BF16),
        b=(b_vec * e_neg).astype(BF16), k=(k_dir * e_neg).astype(BF16),
        bh=(b_vec * e_rest).astype(BF16), kh=(k_dir * e_rest).astype(BF16),
        v=v_s.astype(BF16), pc=jnp.exp(cl_tot))


def _rwkv_kernel(*refs, n_chunks, rotary, emit_y):
    per_dir = 12
    in_f, in_b = refs[:per_dir], refs[per_dir:2 * per_dir]
    prm = refs[2 * per_dir:2 * per_dir + 8]
    s0_ref = refs[2 * per_dir + 8]
    yf_ref, yb_ref, bonf_ref, bonb_ref, sout_ref, h_scr = refs[2 * per_dir + 9:]
    C = CHUNK
    c = pl.program_id(1)

    @pl.when(c == 0)
    def _():
        h_scr[:, :D_MODEL] = s0_ref[0]
        h_scr[:, D_MODEL:] = s0_ref[1]

    same_head = _block_diag_masks()
    ones_bd = jnp.where(same_head, 1.0, 0.0).astype(BF16)
    ops = [_rwkv_prepare(0, c, n_chunks, in_f + (bonf_ref,), prm, ones_bd, rotary),
           _rwkv_prepare(1, n_chunks - 1 - c, n_chunks, in_b + (bonb_ref,), prm, ones_bd, rotary)]

    n_g = 2 * N_GROUPS
    dirs = [g // N_GROUPS for g in range(n_g)]

    def grp(name):
        return [ops[dirs[g]][name][:, (g % N_GROUPS) * GROUP:(g % N_GROUPS + 1) * GROUP] for g in range(n_g)]

    at, rt, bt, kt, bh, kh, vv, pc = (grp(n) for n in ("a", "r", "b", "k", "bh", "kh", "v", "pc"))
    t_i = lax.broadcasted_iota(jnp.int32, (C, GROUP), 0)
    s_i = lax.broadcasted_iota(jnp.int32, (C, GROUP), 1) % C
    before = [s_i < t_i, s_i > t_i]
    upto = [s_i <= t_i, s_i >= t_i]
    eye = jnp.where(t_i == s_i, 1.0, 0.0).astype(F32)
    nt = (((1,), (1,)), ((), ()))
    tn = (((0,), (0,)), ((), ()))

    def bd(x):
        return jnp.where(same_head, jnp.tile(x, (HEADS_PER_GROUP, 1)), jnp.zeros((), x.dtype))

    def hdot(a_side, x):
        return jnp.dot(a_side, bd(x), preferred_element_type=F32)

    def masked(m, keep):
        return jnp.where(keep, m, 0.0).astype(BF16)

    if emit_y:
        lhs = [jnp.concatenate([at[g], rt[g]], axis=0) for g in range(n_g)]
    else:
        lhs = at
    m_b = [lax.dot_general(lhs[g], bd(bt[g]), nt, preferred_element_type=F32) for g in range(n_g)]
    m_k = [lax.dot_general(lhs[g], bd(kt[g]), nt, preferred_element_type=F32) for g in range(n_g)]
    a_ab = [masked(m_b[g][:C], before[dirs[g]]) for g in range(n_g)]
    a_ak = [masked(m_k[g][:C], before[dirs[g]]) for g in range(n_g)]
    if emit_y:
        a_rb = [masked(m_b[g][C:], upto[dirs[g]]) for g in range(n_g)]
        a_rk = [masked(m_k[g][C:], upto[dirs[g]]) for g in range(n_g)]
        w_v = [hdot(jnp.concatenate([a_ak[g], a_rk[g]], axis=0), vv[g]) for g in range(n_g)]
    else:
        w_v = [hdot(a_ak[g], vv[g]) for g in range(n_g)]

    eye_bf = eye.astype(BF16)
    inv = [eye_bf + a_ab[g] for g in range(n_g)]
    npow = a_ab
    for _ in range(C.bit_length() - 2):
        npow = [hdot(npow[g], npow[g]).astype(BF16) for g in range(n_g)]
        inv = [hdot(inv[g], eye_bf + npow[g]).astype(BF16) for g in range(n_g)]

    a_p = [hdot(inv[g], at[g]).astype(BF16) for g in range(n_g)]
    u0 = [hdot(inv[g], w_v[g][:C].astype(BF16)) for g in range(n_g)]

    h_old = [h_scr[:, g * GROUP:(g + 1) * GROUP] for g in range(n_g)]
    decay_rows = [(eye * pc[g]).astype(BF16) for g in range(n_g)]
    stack = [jnp.concatenate([a_p[g]] + ([rt[g]] if emit_y else []) + [decay_rows[g]], axis=0)
             for g in range(n_g)]
    s1 = [hdot(stack[g], h_old[g].astype(BF16)) for g in range(n_g)]
    u = [(s1[g][:C] + u0[g]).astype(BF16) for g in range(n_g)]
    if emit_y:
        y_u = [hdot(a_rb[g], u[g]) for g in range(n_g)]
        for g in range(n_g):
            y_ref = yf_ref if dirs[g] == 0 else yb_ref
            lanes = slice((g % N_GROUPS) * GROUP, (g % N_GROUPS + 1) * GROUP)
            y_ref[:, lanes] = (s1[g][C:2 * C] + y_u[g] + w_v[g][C:]).astype(y_ref.dtype)
    else:
        yf_ref[...] = jnp.zeros(yf_ref.shape, yf_ref.dtype)
        yb_ref[...] = jnp.zeros(yb_ref.shape, yb_ref.dtype)

    full = [lax.dot_general(jnp.concatenate([bh[g], kh[g]], axis=0),
                            jnp.concatenate([u[g], vv[g]], axis=0), tn, preferred_element_type=F32)
            for g in range(n_g)]
    for g in range(n_g):
        keep = jnp.where(same_head, full[g], 0.0)
        upd = keep[0:HEAD_DIM]
        for j in range(1, HEADS_PER_GROUP):
            upd = upd + keep[j * HEAD_DIM:(j + 1) * HEAD_DIM]
        h_scr[:, g * GROUP:(g + 1) * GROUP] = s1[g][-C:] + upd

    @pl.when(c == n_chunks - 1)
    def _():
        sout_ref[0] = h_scr[:, :D_MODEL]
        sout_ref[1] = h_scr[:, D_MODEL:]


def _rope_tables(T):
    half = HEAD_DIM // 2
    n_freq = half // 2
    t = np.arange(T)[:, None]
    lane = np.arange(LANES)[None, :]
    inv_freq = ROPE_THETA ** (-(np.arange(n_freq, dtype=np.float32)) / n_freq)
    pos = np.where((lane % HEAD_DIM) < half, t // GRID_W, t % GRID_W).astype(np.float32)
    ang = (pos * inv_freq[lane % n_freq].astype(np.float32)).astype(np.float32)
    sign = np.where((lane % half) < n_freq, -1.0, 1.0)
    return jnp.asarray(np.cos(ang), F32), jnp.asarray(np.sin(ang) * sign, F32)


def _rwkv_scan(u, state0, params, rotary, emit_y):
    B, T, _ = u.shape
    C = CHUNK
    n_chunks = T // C
    halo = 16
    per = C // halo

    def chunk_of(d, c):
        return c if d == 0 else n_chunks - 1 - c

    def specs(d):
        def main(part):
            return pl.BlockSpec((None, C, D_MODEL), lambda b, c: (b, chunk_of(d, c), part))

        def prev(part):
            return pl.BlockSpec((None, halo, D_MODEL),
                                lambda b, c: (b, jnp.maximum(chunk_of(d, c) * per - 1, 0), part))

        def nxt(part):
            return pl.BlockSpec((None, halo, D_MODEL),
                                lambda b, c: (b, jnp.minimum((chunk_of(d, c) + 1) * per, T // halo - 1), part))

        r_blk, k_blk, v_blk = 4, 5, 6
        lora = pl.BlockSpec((None, C, 4 * LORA), lambda b, c: (b, chunk_of(d, c), COL_LORA // (4 * LORA)))
        tab = pl.BlockSpec((C, LANES), lambda b, c: (chunk_of(d, c), 0))
        return [main(r_blk), main(k_blk), main(v_blk), prev(r_blk), prev(k_blk), prev(v_blk),
                nxt(r_blk), nxt(k_blk), nxt(v_blk), lora, tab, tab]

    def whole(shape):
        return pl.BlockSpec(shape, lambda b, c: (0,) * len(shape))

    cos_t, sin_t = _rope_tables(T)
    state = pl.BlockSpec((None, 2, HEAD_DIM, D_MODEL), lambda b, c: (b, 0, 0, 0))
    param_specs = [whole((3, D_MODEL)), whole((2, D_MODEL)), whole((2, 2 * LORA, D_MODEL)),
                   whole((2, D_MODEL)), whole((2, 2 * LORA, D_MODEL)),
                   whole((1, D_MODEL)), whole((1, D_MODEL)), whole((1, D_MODEL))]
    seq = jax.ShapeDtypeStruct((B, T, D_MODEL), BF16)
    out_f = pl.BlockSpec((None, C, D_MODEL), lambda b, c: (b, chunk_of(0, c), 0))
    out_b = pl.BlockSpec((None, C, D_MODEL), lambda b, c: (b, chunk_of(1, c), 0))
    seq_in = (u,) * 10 + (cos_t, sin_t)
    return pl.pallas_call(
        functools.partial(_rwkv_kernel, n_chunks=n_chunks, rotary=rotary, emit_y=emit_y),
        out_shape=(seq, seq, seq, seq, jax.ShapeDtypeStruct((B, 2, HEAD_DIM, D_MODEL), F32)),
        grid=(B, n_chunks),
        in_specs=specs(0) + specs(1) + param_specs + [state],
        out_specs=(out_f, out_b, out_f, out_b, state),
        scratch_shapes=[pltpu.VMEM((HEAD_DIM, 2 * D_MODEL), F32)],
        compiler_params=_cparams(("arbitrary", "arbitrary")),
        name="rwkv7_chunk_scan_rot" if rotary else "rwkv7_chunk_scan_ctx",
    )(*seq_in, *seq_in, *params, state0)


def _final_kernel(x_ref, ya_ref, yf_ref, yb_ref, bf_ref, bb_ref, z_ref, ga_ref, gb_ref, gate_ref,
                  gng_ref, gnb_ref, wa_ref, wb_ref, wo_ref, lng_ref, lnb_ref, o_ref):
    same_head = _block_diag_masks()
    ones_bd = jnp.where(same_head, 1.0, 0.0).astype(BF16)
    y = yf_ref[...].astype(F32) + yb_ref[...].astype(F32)
    mu = _split_head_sum(y, ones_bd) * (1.0 / HEAD_DIM)
    yc = y - mu
    var = _split_head_sum(yc * yc, ones_bd) * (1.0 / HEAD_DIM)
    yn = yc * lax.rsqrt(var + GN_EPS) * gng_ref[...] + gnb_ref[...]
    yn = yn + bf_ref[...].astype(F32) + bb_ref[...].astype(F32)
    y_b = (yn * _silu(z_ref[...].astype(F32))).astype(BF16)
    p_a = jnp.dot(ya_ref[...], wa_ref[...], preferred_element_type=F32)
    p_b = jnp.dot(y_b, wb_ref[...], preferred_element_type=F32)
    merged = (jax.nn.sigmoid(ga_ref[...].astype(F32)) * p_a
              + jax.nn.sigmoid(gb_ref[...].astype(F32)) * p_b)
    out = jnp.dot(merged.astype(BF16), wo_ref[...], preferred_element_type=F32)
    t = ALPHA * x_ref[...] + gate_ref[...] * out
    m = jnp.mean(t, axis=-1, keepdims=True)
    tc = t - m
    v = jnp.mean(tc * tc, axis=-1, keepdims=True)
    o_ref[...] = tc * lax.rsqrt(v + LN_EPS) * lng_ref[...] + lnb_ref[...]


def _final_stage(x, y_a, y_f, y_b, bonus_f, bonus_b, u, mod3, gn_g, gn_b, wa, wb, wo, ln_g, ln_b, tm):
    B, T, _ = x.shape

    def tok():
        return pl.BlockSpec((None, tm, D_MODEL), lambda b, i: (b, i, 0))

    def ucol(blk):
        return pl.BlockSpec((None, tm, D_MODEL), lambda b, i: (b, i, blk))

    def vec():
        return pl.BlockSpec((1, D_MODEL), lambda b, i: (0, 0))

    def mat():
        return pl.BlockSpec((D_MODEL, D_MODEL), lambda b, i: (0, 0))

    return pl.pallas_call(
        _final_kernel,
        out_shape=jax.ShapeDtypeStruct((B, T, D_MODEL), F32),
        grid=(B, T // tm),
        in_specs=[tok(), tok(), tok(), tok(), tok(), tok(), ucol(7), ucol(8), ucol(9),
                  pl.BlockSpec((None, 1, D_MODEL), lambda b, i: (b, 0, 2)),
                  vec(), vec(), mat(), mat(), mat(), vec(), vec()],
        out_specs=tok(),
        compiler_params=_cparams(("arbitrary", "arbitrary")),
        name="readout_merge_out_proj",
    )(x, y_a, y_f, y_b, bonus_f, bonus_b, u, u, u, mod3, gn_g, gn_b, wa, wb, wo, ln_g, ln_b)


def _layer(x, c, ctx, c_ctx, w_ada, b_ada, w_in, na_rpb, rw_mu, rw_w0, rw_w_up, rw_a0, rw_a_up,
           rw_k_k, rw_k_a, rw_r_k, rw_gn_g, rw_gn_b, w_branch_a, w_branch_b, w_out, ln_g, ln_b):
    B, T, _ = x.shape
    L = ctx.shape[1]
    rows = T // GRID_W

    n_cond = -(-(B + 1) // 8) * 8
    cond = jnp.zeros((n_cond, D_MODEL), F32).at[:B].set(c).at[B].set(c_ctx)
    mod3 = _modulation(cond, w_ada, b_ada).reshape(n_cond, 1, 3 * D_MODEL)

    w_bf = jnp.concatenate(
        [w_in[:, :N_MAIN], w_in[:, N_MAIN + 4 * LORA:], w_in[:, N_MAIN:N_MAIN + 4 * LORA],
         jnp.zeros((D_MODEL, N_PAD - N_PROJ), w_in.dtype)], axis=1).astype(BF16)
    u = _in_projection(x, mod3, w_bf, lambda b: b, tm=min(T, 1024), tn=512)
    u_ctx = _in_projection(ctx, mod3, w_bf, lambda b: B, tm=L, tn=512)

    y_a = _neighbourhood_attention(u, u_ctx, _na_bias_table(na_rpb, rows))

    def lora_pad(w_up):
        out = jnp.zeros((2, 2 * LORA, D_MODEL), F32)
        for d in range(2):
            out = out.at[d, d * LORA:(d + 1) * LORA].set(w_up[d])
        return out.astype(BF16)

    params = (rw_mu, rw_w0, lora_pad(rw_w_up), rw_a0, lora_pad(rw_a_up), rw_k_k.reshape(1, D_MODEL),
              rw_k_a.reshape(1, D_MODEL), rw_r_k.reshape(1, D_MODEL))
    state0 = jnp.zeros((B, 2, HEAD_DIM, D_MODEL), F32)
    state_c = _rwkv_scan(u_ctx, state0, params, rotary=False, emit_y=False)[-1]
    y_f, y_b, bonus_f, bonus_b, _ = _rwkv_scan(u, state_c, params, rotary=True, emit_y=True)

    return _final_stage(x, y_a, y_f, y_b, bonus_f, bonus_b, u, mod3, rw_gn_g.reshape(1, D_MODEL),
                        rw_gn_b.reshape(1, D_MODEL), w_branch_a.astype(BF16), w_branch_b.astype(BF16),
                        w_out.astype(BF16), ln_g.reshape(1, D_MODEL), ln_b.reshape(1, D_MODEL),
                        tm=min(T, 512))


def kernel(x, c, ctx, c_ctx, w_ada, b_ada, w_in, na_rpb, rw_mu, rw_w0, rw_w_up, rw_a0, rw_a_up, rw_k_k, rw_k_a, rw_r_k, rw_gn_g, rw_gn_b, w_branch_a, w_branch_b, w_out, ln_g, ln_b):
    assert w_ada.shape[0] == DEPTH
    return _layer(x, c, ctx, c_ctx, w_ada[0], b_ada[0], w_in[0], na_rpb[0], rw_mu[0], rw_w0[0],
                  rw_w_up[0], rw_a0[0], rw_a_up[0], rw_k_k[0], rw_k_a[0], rw_r_k[0], rw_gn_g[0],
                  rw_gn_b[0], w_branch_a[0], w_branch_b[0], w_out[0], ln_g[0], ln_b[0])
```

```python
import functools

import numpy as np
import jax
import jax.numpy as jnp
from jax import lax
from jax.experimental import pallas as pl
from jax.experimental.pallas import tpu as pltpu

F32 = jnp.float32
BF16 = jnp.bfloat16
HIGHEST = lax.Precision.HIGHEST

D_MODEL = 1024
GRID_W = 64
HEADS = 16
HEAD_DIM = 64
NA_MAX_ROWS = 8
NA_COLS = 16
LORA = 64
DEPTH = 1
ROPE_THETA = 10000.0
LN_EPS = 1e-5
GN_EPS = 64e-5
ALPHA = (2 * DEPTH) ** 0.25

LANES = 128
GROUP = 256
HEADS_PER_GROUP = GROUP // HEAD_DIM
N_GROUPS = D_MODEL // GROUP
CHUNK = 64

N_MAIN = 8 * D_MODEL
COL_MG = N_MAIN
COL_LORA = N_MAIN + 2 * D_MODEL
N_PROJ = COL_LORA + 4 * LORA
N_PAD = 21 * 512
VMEM_LIMIT = 56 * 1024 * 1024


def _cparams(sem):
    return pltpu.CompilerParams(dimension_semantics=sem, vmem_limit_bytes=VMEM_LIMIT)


def _silu(x):
    return x * jax.nn.sigmoid(x)


def _mod_kernel(c_ref, w_ref, b_ref, o_ref):
    s = _silu(c_ref[...])
    o_ref[...] = jnp.dot(s, w_ref[...], preferred_element_type=F32, precision=HIGHEST) + b_ref[...]


def _modulation(cond, w_ada, b_ada):
    rows = cond.shape[0]
    return pl.pallas_call(
        _mod_kernel,
        out_shape=jax.ShapeDtypeStruct((rows, 3 * D_MODEL), F32),
        grid=(3,),
        in_specs=[pl.BlockSpec((rows, D_MODEL), lambda j: (0, 0)),
                  pl.BlockSpec((D_MODEL, D_MODEL), lambda j: (0, j)),
                  pl.BlockSpec((1, D_MODEL), lambda j: (0, j))],
        out_specs=pl.BlockSpec((rows, D_MODEL), lambda j: (0, j)),
        compiler_params=_cparams(("arbitrary",)),
        name="ada_modulation",
    )(cond, w_ada, b_ada.reshape(1, 3 * D_MODEL))


def _inproj_kernel(x_ref, sh_ref, sc_ref, w_ref, o_ref, h_scr):
    @pl.when(pl.program_id(2) == 0)
    def _():
        x = x_ref[...]
        mu = jnp.mean(x, axis=-1, keepdims=True)
        xc = x - mu
        var = jnp.mean(xc * xc, axis=-1, keepdims=True)
        y = xc * lax.rsqrt(var + LN_EPS)
        h_scr[...] = (y * (1.0 + sc_ref[...]) + sh_ref[...]).astype(BF16)

    o_ref[...] = jnp.dot(h_scr[...], w_ref[...], preferred_element_type=F32).astype(BF16)


def _in_projection(x, mod3, w_bf, row_of_batch, tm, tn):
    B, T, _ = x.shape
    return pl.pallas_call(
        _inproj_kernel,
        out_shape=jax.ShapeDtypeStruct((B, T, N_PAD), BF16),
        grid=(B, T // tm, N_PAD // tn),
        in_specs=[pl.BlockSpec((None, tm, D_MODEL), lambda b, i, j: (b, i, 0)),
                  pl.BlockSpec((None, 1, D_MODEL), lambda b, i, j: (row_of_batch(b), 0, 0)),
                  pl.BlockSpec((None, 1, D_MODEL), lambda b, i, j: (row_of_batch(b), 0, 1)),
                  pl.BlockSpec((D_MODEL, tn), lambda b, i, j: (0, j))],
        out_specs=pl.BlockSpec((None, tm, tn), lambda b, i, j: (b, i, j)),
        scratch_shapes=[pltpu.VMEM((tm, D_MODEL), BF16)],
        compiler_params=_cparams(("arbitrary", "arbitrary", "arbitrary")),
        name="ln_mod_in_proj",
    )(x, mod3, mod3, w_bf)


NA_HEADS_PER_STEP = GROUP // HEAD_DIM


def _na_kernel(q_ref, k_ref, v_ref, z_ref, kc_ref, vc_ref, bias_ref, o_ref, *, rows):
    kh = min(NA_MAX_ROWS, rows)
    band = kh * GRID_W
    nh = NA_HEADS_PER_STEP
    lane = lax.broadcasted_iota(jnp.int32, (1, GROUP), 1)
    head_lanes = [(lane // HEAD_DIM) == h for h in range(nh)]
    kc = kc_ref[...]
    vc = vc_ref[...]
    nt = (((1,), (1,)), ((), ()))

    def row_body(i, carry):
        rs = jnp.clip(i - kh // 2, 0, rows - kh)
        q0 = pl.multiple_of(i * GRID_W, GRID_W)
        k0 = pl.multiple_of(rs * GRID_W, GRID_W)
        q = q_ref[pl.ds(q0, GRID_W), :] * jnp.asarray(HEAD_DIM ** -0.5, BF16)
        kb = k_ref[pl.ds(k0, band), :]
        vb = v_ref[pl.ds(k0, band), :]
        qs = jnp.concatenate([jnp.where(head_lanes[h], q, jnp.zeros_like(q)) for h in range(nh)], axis=0)
        s = lax.dot_general(qs, kb, nt, preferred_element_type=F32) + bias_ref[i - rs]
        sc = lax.dot_general(qs, kc, nt, preferred_element_type=F32)
        m = jnp.maximum(jnp.max(s, axis=-1, keepdims=True), jnp.max(sc, axis=-1, keepdims=True))
        p = jnp.exp(s - m)
        pc = jnp.exp(sc - m)
        denom = jnp.sum(p, axis=-1, keepdims=True) + jnp.sum(pc, axis=-1, keepdims=True)
        o_all = (jnp.dot(p.astype(BF16), vb, preferred_element_type=F32)
                 + jnp.dot(pc.astype(BF16), vc, preferred_element_type=F32)) / denom
        o = jnp.where(head_lanes[0], o_all[0:GRID_W], 0.0)
        for h in range(1, nh):
            o = o + jnp.where(head_lanes[h], o_all[h * GRID_W:(h + 1) * GRID_W], 0.0)
        z = z_ref[pl.ds(q0, GRID_W), :].astype(F32)
        o_ref[pl.ds(q0, GRID_W), :] = (o * _silu(z)).astype(BF16)
        return carry

    lax.fori_loop(0, rows, row_body, 0, unroll=4)


def _na_bias_table(rpb, rows):
    kh = min(NA_MAX_ROWS, rows)
    nh = NA_HEADS_PER_STEP
    n_dj = 2 * NA_COLS - 1
    cols = np.arange(GRID_W)
    cstart = np.clip(cols - NA_COLS // 2, 0, GRID_W - NA_COLS)
    col_mask = (cols[None, :] >= cstart[:, None]) & (cols[None, :] < cstart[:, None] + NA_COLS)
    dj = np.clip(cols[None, :] - cols[:, None] + NA_COLS - 1, 0, n_dj - 1)
    onehot = (np.arange(n_dj)[:, None] == dj.reshape(1, -1)).astype(np.float32)
    exp = jnp.dot(rpb.reshape(-1, n_dj), jnp.asarray(onehot), precision=HIGHEST)
    exp = exp.reshape(HEADS, 2 * NA_MAX_ROWS - 1, GRID_W, GRID_W)
    exp = jnp.where(col_mask[None, None], exp, -jnp.inf)
    lo = NA_MAX_ROWS - 1
    tab = jnp.stack([exp[:, lo - o: lo - o + kh] for o in range(kh)], axis=1)
    tab = tab.reshape(HEADS // nh, nh, kh, kh, GRID_W, GRID_W)
    return tab.transpose(0, 2, 1, 4, 3, 5).reshape(HEADS // nh, kh, nh * GRID_W, kh * GRID_W)


def _neighbourhood_attention(u, u_ctx, bias_tab):
    B, T, _ = u.shape
    L = u_ctx.shape[1]
    rows = T // GRID_W
    kh = min(NA_MAX_ROWS, rows)
    blk = D_MODEL // GROUP

    def col(part):
        return pl.BlockSpec((None, T, GROUP), lambda b, g: (b, 0, part * blk + g))

    def col_ctx(part):
        return pl.BlockSpec((None, L, GROUP), lambda b, g: (b, 0, part * blk + g))

    return pl.pallas_call(
        functools.partial(_na_kernel, rows=rows),
        out_shape=jax.ShapeDtypeStruct((B, T, D_MODEL), BF16),
        grid=(B, blk),
        in_specs=[col(0), col(1), col(2), col(3), col_ctx(1), col_ctx(2),
                  pl.BlockSpec((None, kh, NA_HEADS_PER_STEP * GRID_W, kh * GRID_W), lambda b, g: (g, 0, 0, 0))],
        out_specs=pl.BlockSpec((None, T, GROUP), lambda b, g: (b, 0, g)),
        compiler_params=_cparams(("arbitrary", "arbitrary")),
        name="neighbourhood_attention",
    )(u, u, u, u, u_ctx, u_ctx, bias_tab)


def _block_diag_masks():
    r = lax.broadcasted_iota(jnp.int32, (GROUP, GROUP), 0)
    c = lax.broadcasted_iota(jnp.int32, (GROUP, GROUP), 1)
    return (r // HEAD_DIM) == (c // HEAD_DIM)


def _head_sum(x, ones_bd):
    rows = x.shape[0]
    xb = x.astype(BF16)
    stacked = jnp.concatenate([xb[:, g * GROUP:(g + 1) * GROUP] for g in range(N_GROUPS)], axis=0)
    sums = jnp.dot(stacked, ones_bd, preferred_element_type=F32)
    return jnp.concatenate([sums[g * rows:(g + 1) * rows] for g in range(N_GROUPS)], axis=1)


def _split_head_sum(x, ones_bd):
    hi = x.astype(BF16)
    return _head_sum(hi, ones_bd) + _head_sum(x - hi.astype(F32), ones_bd)


def _rwkv_prepare(d, cidx, n_chunks, refs, prm, ones_bd, rotary):
    (r_ref, k_ref, v_ref, rp_ref, kp_ref, vp_ref, rn_ref, kn_ref, vn_ref, lo_ref, cos_ref, sin_ref,
     bonus_ref) = refs
    mu_ref, w0_ref, wup_ref, a0_ref, aup_ref, kk_ref, ka_ref, rk_ref = prm
    C = CHUNK
    halo = rp_ref.shape[0]

    tt = lax.broadcasted_iota(jnp.int32, (C, C + 2 * halo), 0)
    ss = lax.broadcasted_iota(jnp.int32, (C, C + 2 * halo), 1) - halo
    first = jnp.where(cidx > 0, -1, 0)
    last = jnp.where(cidx < n_chunks - 1, C, C - 1)
    nb = (jnp.abs(ss - tt) == 1) & (ss >= first) & (ss <= last)
    nb = jnp.where(nb, 1.0, 0.0).astype(BF16)

    def shifted(x_ref, p_ref, n_ref, mu):
        x = x_ref[...]
        both = jnp.dot(nb, jnp.concatenate([p_ref[...], x, n_ref[...]], axis=0), preferred_element_type=F32)
        return (1.0 - mu) * x.astype(F32) + (0.5 * mu) * both

    r_s = shifted(r_ref, rp_ref, rn_ref, mu_ref[0:1, :])
    k_s = shifted(k_ref, kp_ref, kn_ref, mu_ref[1:2, :])
    v_s = shifted(v_ref, vp_ref, vn_ref, mu_ref[2:3, :])

    if rotary:
        lane = lax.broadcasted_iota(jnp.int32, (1, D_MODEL), 1)
        low = (lane % 32) < 16
        cos_t = jnp.tile(cos_ref[...], (1, D_MODEL // LANES))
        sin_t = jnp.tile(sin_ref[...], (1, D_MODEL // LANES))

        def rope(x):
            partner = jnp.where(low, pltpu.roll(x, D_MODEL - 16, axis=1), pltpu.roll(x, 16, axis=1))
            return x * cos_t + partner * sin_t

        r_s = rope(r_s)
        k_s = rope(k_s)

    lo = lo_ref[...]
    lw = w0_ref[d:d + 1, :] + jnp.dot(jnp.tanh(lo[:, :2 * LORA].astype(F32)).astype(BF16), wup_ref[d],
                                     preferred_element_type=F32)
    ld = (-np.exp(-0.5)) * jax.nn.sigmoid(lw)
    a = jax.nn.sigmoid(a0_ref[d:d + 1, :] + jnp.dot(lo[:, 2 * LORA:], aup_ref[d], preferred_element_type=F32))

    kk = k_s * kk_ref[...]
    kk = kk * jnp.minimum(lax.rsqrt(_head_sum(kk * kk, ones_bd)), 1e12)
    k_dir = k_s * (1.0 + (a - 1.0) * ka_ref[...])
    b_vec = kk * a
    bonus_ref[...] = (_head_sum(r_s * rk_ref[...] * k_dir, ones_bd) * v_s).astype(bonus_ref.dtype)

    tt = lax.broadcasted_iota(jnp.int32, (C, C), 0)
    ss = lax.broadcasted_iota(jnp.int32, (C, C), 1)
    tri = jnp.where((ss <= tt) if d == 0 else (ss >= tt), 1.0, 0.0).astype(BF16)
    ld_hi = ld.astype(BF16)
    ld_lo = (ld - ld_hi.astype(F32)).astype(BF16)
    cl = (jnp.dot(tri, ld_hi, preferred_element_type=F32) + jnp.dot(tri, ld_lo, preferred_element_type=F32))
    cl_tot = cl[C - 1:C, :] if d == 0 else cl[0:1, :]
    e_neg = jnp.exp(-cl)
    p_c = jnp.exp(cl_tot)
    b_t = b_vec * e_neg
    k_t = k_dir * e_neg
    return dict(
        a=(-kk * jnp.exp(cl - ld)).astype(BF16), r=(r_s * jnp.exp(cl)).astype(BF16),
        b=b_t.astype(BF16), k=k_t.astype(BF16), bh=(b_t * p_c).astype(BF16), kh=(k_t * p_c).astype(BF16),
        v=v_s.astype(BF16), pc=p_c)


def _rwkv_kernel(*refs, n_chunks, rotary, emit_y):
    per_dir = 12
    in_f, in_b = refs[:per_dir], refs[per_dir:2 * per_dir]
    prm = refs[2 * per_dir:2 * per_dir + 8]
    s0_ref = refs[2 * per_dir + 8]
    yf_ref, yb_ref, bonf_ref, bonb_ref, sout_ref, h_scr = refs[2 * per_dir + 9:]
    C = CHUNK
    c = pl.program_id(1)

    @pl.when(c == 0)
    def _():
        h_scr[:, :D_MODEL] = s0_ref[0]
        h_scr[:, D_MODEL:] = s0_ref[1]

    same_head = _block_diag_masks()
    ones_bd = jnp.where(same_head, 1.0, 0.0).astype(BF16)
    ops = [_rwkv_prepare(0, c, n_chunks, in_f + (bonf_ref,), prm, ones_bd, rotary),
           _rwkv_prepare(1, n_chunks - 1 - c, n_chunks, in_b + (bonb_ref,), prm, ones_bd, rotary)]

    n_g = 2 * N_GROUPS
    dirs = [g // N_GROUPS for g in range(n_g)]

    def grp(name):
        return [ops[dirs[g]][name][:, (g % N_GROUPS) * GROUP:(g % N_GROUPS + 1) * GROUP] for g in range(n_g)]

    at, rt, bt, kt, bh, kh, vv, pc = (grp(n) for n in ("a", "r", "b", "k", "bh", "kh", "v", "pc"))
    t_i = lax.broadcasted_iota(jnp.int32, (C, GROUP), 0)
    s_i = lax.broadcasted_iota(jnp.int32, (C, GROUP), 1) % C
    before = [s_i < t_i, s_i > t_i]
    upto = [s_i <= t_i, s_i >= t_i]
    eye = jnp.where(t_i == s_i, 1.0, 0.0).astype(F32)
    nt = (((1,), (1,)), ((), ()))
    tn = (((0,), (0,)), ((), ()))

    def bd(x):
        return jnp.where(same_head, jnp.tile(x, (HEADS_PER_GROUP, 1)), jnp.zeros((), x.dtype))

    def hdot(a_side, x):
        return jnp.dot(a_side, bd(x), preferred_element_type=F32)

    def masked(m, keep):
        return jnp.where(keep, m, 0.0).astype(BF16)

    if emit_y:
        lhs = [jnp.concatenate([at[g], rt[g]], axis=0) for g in range(n_g)]
    else:
        lhs = at
    m_b = [lax.dot_general(lhs[g], bd(bt[g]), nt, preferred_element_type=F32) for g in range(n_g)]
    m_k = [lax.dot_general(lhs[g], bd(kt[g]), nt, preferred_element_type=F32) for g in range(n_g)]
    a_ab = [masked(m_b[g][:C], before[dirs[g]]) for g in range(n_g)]
    a_ak = [masked(m_k[g][:C], before[dirs[g]]) for g in range(n_g)]
    if emit_y:
        a_rb = [masked(m_b[g][C:], upto[dirs[g]]) for g in range(n_g)]
        a_rk = [masked(m_k[g][C:], upto[dirs[g]]) for g in range(n_g)]
        w_v = [hdot(jnp.concatenate([a_ak[g], a_rk[g]], axis=0), vv[g]) for g in range(n_g)]
    else:
        w_v = [hdot(a_ak[g], vv[g]) for g in range(n_g)]

    npow = [hdot(a_ab[g], a_ab[g]) for g in range(n_g)]
    inv = [eye + a_ab[g].astype(F32) for g in range(n_g)]
    n_factors = C.bit_length() - 1
    for j in range(1, n_factors):
        last = j == n_factors - 1
        pw = [npow[g].astype(BF16) for g in range(n_g)]
        lhs_j = [inv[g].astype(BF16) if last else jnp.concatenate([inv[g].astype(BF16), pw[g]], axis=0)
                 for g in range(n_g)]
        prod = [hdot(lhs_j[g], pw[g]) for g in range(n_g)]
        inv = [inv[g] + prod[g][:C] for g in range(n_g)]
        if not last:
            npow = [prod[g][C:] for g in range(n_g)]
    inv = [inv[g].astype(BF16) for g in range(n_g)]

    a_p = [hdot(inv[g], at[g]).astype(BF16) for g in range(n_g)]
    u0 = [hdot(inv[g], w_v[g][:C].astype(BF16)) for g in range(n_g)]

    h_old = [h_scr[:, g * GROUP:(g + 1) * GROUP] for g in range(n_g)]
    decay_rows = [(eye * pc[g]).astype(BF16) for g in range(n_g)]
    stack = [jnp.concatenate([a_p[g]] + ([rt[g]] if emit_y else []) + [decay_rows[g]], axis=0)
             for g in range(n_g)]
    s1 = [hdot(stack[g], h_old[g].astype(BF16)) for g in range(n_g)]
    u = [(s1[g][:C] + u0[g]).astype(BF16) for g in range(n_g)]
    if emit_y:
        y_u = [hdot(a_rb[g], u[g]) for g in range(n_g)]
        for g in range(n_g):
            y_ref = yf_ref if dirs[g] == 0 else yb_ref
            lanes = slice((g % N_GROUPS) * GROUP, (g % N_GROUPS + 1) * GROUP)
            y_ref[:, lanes] = (s1[g][C:2 * C] + y_u[g] + w_v[g][C:]).astype(y_ref.dtype)
    else:
        yf_ref[...] = jnp.zeros(yf_ref.shape, yf_ref.dtype)
        yb_ref[...] = jnp.zeros(yb_ref.shape, yb_ref.dtype)

    full = [lax.dot_general(jnp.concatenate([bh[g], kh[g]], axis=0),
                            jnp.concatenate([u[g], vv[g]], axis=0), tn, preferred_element_type=F32)
            for g in range(n_g)]
    lane_head = lax.broadcasted_iota(jnp.int32, (HEAD_DIM, GROUP), 1) // HEAD_DIM
    for g in range(n_g):
        upd = full[g][(HEADS_PER_GROUP - 1) * HEAD_DIM:]
        for j in range(HEADS_PER_GROUP - 2, -1, -1):
            upd = jnp.where(lane_head == j, full[g][j * HEAD_DIM:(j + 1) * HEAD_DIM], upd)
        h_scr[:, g * GROUP:(g + 1) * GROUP] = s1[g][-C:] + upd

    @pl.when(c == n_chunks - 1)
    def _():
        sout_ref[0] = h_scr[:, :D_MODEL]
        sout_ref[1] = h_scr[:, D_MODEL:]


def _rope_tables(T):
    half = HEAD_DIM // 2
    n_freq = half // 2
    t = np.arange(T)[:, None]
    lane = np.arange(LANES)[None, :]
    inv_freq = ROPE_THETA ** (-(np.arange(n_freq, dtype=np.float32)) / n_freq)
    pos = np.where((lane % HEAD_DIM) < half, t // GRID_W, t % GRID_W).astype(np.float32)
    ang = (pos * inv_freq[lane % n_freq].astype(np.float32)).astype(np.float32)
    sign = np.where((lane % half) < n_freq, -1.0, 1.0)
    return jnp.asarray(np.cos(ang), F32), jnp.asarray(np.sin(ang) * sign, F32)


def _rwkv_scan(u, state0, params, rotary, emit_y):
    B, T, _ = u.shape
    C = CHUNK
    n_chunks = T // C
    halo = 16
    per = C // halo

    def chunk_of(d, c):
        return c if d == 0 else n_chunks - 1 - c

    def specs(d):
        def main(part):
            return pl.BlockSpec((None, C, D_MODEL), lambda b, c: (b, chunk_of(d, c), part))

        def prev(part):
            return pl.BlockSpec((None, halo, D_MODEL),
                                lambda b, c: (b, jnp.maximum(chunk_of(d, c) * per - 1, 0), part))

        def nxt(part):
            return pl.BlockSpec((None, halo, D_MODEL),
                                lambda b, c: (b, jnp.minimum((chunk_of(d, c) + 1) * per, T // halo - 1), part))

        r_blk, k_blk, v_blk = 4, 5, 6
        lora = pl.BlockSpec((None, C, 4 * LORA), lambda b, c: (b, chunk_of(d, c), COL_LORA // (4 * LORA)))
        tab = pl.BlockSpec((C, LANES), lambda b, c: (chunk_of(d, c), 0))
        return [main(r_blk), main(k_blk), main(v_blk), prev(r_blk), prev(k_blk), prev(v_blk),
                nxt(r_blk), nxt(k_blk), nxt(v_blk), lora, tab, tab]

    def whole(shape):
        return pl.BlockSpec(shape, lambda b, c: (0,) * len(shape))

    cos_t, sin_t = _rope_tables(T)
    state = pl.BlockSpec((None, 2, HEAD_DIM, D_MODEL), lambda b, c: (b, 0, 0, 0))
    param_specs = [whole((3, D_MODEL)), whole((2, D_MODEL)), whole((2, 2 * LORA, D_MODEL)),
                   whole((2, D_MODEL)), whole((2, 2 * LORA, D_MODEL)),
                   whole((1, D_MODEL)), whole((1, D_MODEL)), whole((1, D_MODEL))]
    seq = jax.ShapeDtypeStruct((B, T, D_MODEL), BF16)
    out_f = pl.BlockSpec((None, C, D_MODEL), lambda b, c: (b, chunk_of(0, c), 0))
    out_b = pl.BlockSpec((None, C, D_MODEL), lambda b, c: (b, chunk_of(1, c), 0))
    seq_in = (u,) * 10 + (cos_t, sin_t)
    return pl.pallas_call(
        functools.partial(_rwkv_kernel, n_chunks=n_chunks, rotary=rotary, emit_y=emit_y),
        out_shape=(seq, seq, seq, seq, jax.ShapeDtypeStruct((B, 2, HEAD_DIM, D_MODEL), F32)),
        grid=(B, n_chunks),
        in_specs=specs(0) + specs(1) + param_specs + [state],
        out_specs=(out_f, out_b, out_f, out_b, state),
        scratch_shapes=[pltpu.VMEM((HEAD_DIM, 2 * D_MODEL), F32)],
        compiler_params=_cparams(("arbitrary", "arbitrary")),
        name="rwkv7_chunk_scan_rot" if rotary else "rwkv7_chunk_scan_ctx",
    )(*seq_in, *seq_in, *params, state0)


def _final_kernel(x_ref, ya_ref, yf_ref, yb_ref, bf_ref, bb_ref, z_ref, ga_ref, gb_ref, gate_ref,
                  gng_ref, gnb_ref, wa_ref, wb_ref, wo_ref, lng_ref, lnb_ref, o_ref):
    same_head = _block_diag_masks()
    ones_bd = jnp.where(same_head, 1.0, 0.0).astype(BF16)
    y = yf_ref[...].astype(F32) + yb_ref[...].astype(F32)
    mu = _split_head_sum(y, ones_bd) * (1.0 / HEAD_DIM)
    yc = y - mu
    var = _split_head_sum(yc * yc, ones_bd) * (1.0 / HEAD_DIM)
    yn = yc * lax.rsqrt(var + GN_EPS) * gng_ref[...] + gnb_ref[...]
    yn = yn + bf_ref[...].astype(F32) + bb_ref[...].astype(F32)
    y_b = (yn * _silu(z_ref[...].astype(F32))).astype(BF16)
    p_a = jnp.dot(ya_ref[...], wa_ref[...], preferred_element_type=F32)
    p_b = jnp.dot(y_b, wb_ref[...], preferred_element_type=F32)
    merged = (jax.nn.sigmoid(ga_ref[...].astype(F32)) * p_a
              + jax.nn.sigmoid(gb_ref[...].astype(F32)) * p_b)
    out = jnp.dot(merged.astype(BF16), wo_ref[...], preferred_element_type=F32)
    t = ALPHA * x_ref[...] + gate_ref[...] * out
    m = jnp.mean(t, axis=-1, keepdims=True)
    tc = t - m
    v = jnp.mean(tc * tc, axis=-1, keepdims=True)
    o_ref[...] = tc * lax.rsqrt(v + LN_EPS) * lng_ref[...] + lnb_ref[...]


def _final_stage(x, y_a, y_f, y_b, bonus_f, bonus_b, u, mod3, gn_g, gn_b, wa, wb, wo, ln_g, ln_b, tm):
    B, T, _ = x.shape

    def tok():
        return pl.BlockSpec((None, tm, D_MODEL), lambda b, i: (b, i, 0))

    def ucol(blk):
        return pl.BlockSpec((None, tm, D_MODEL), lambda b, i: (b, i, blk))

    def vec():
        return pl.BlockSpec((1, D_MODEL), lambda b, i: (0, 0))

    def mat():
        return pl.BlockSpec((D_MODEL, D_MODEL), lambda b, i: (0, 0))

    return pl.pallas_call(
        _final_kernel,
        out_shape=jax.ShapeDtypeStruct((B, T, D_MODEL), F32),
        grid=(B, T // tm),
        in_specs=[tok(), tok(), tok(), tok(), tok(), tok(), ucol(7), ucol(8), ucol(9),
                  pl.BlockSpec((None, 1, D_MODEL), lambda b, i: (b, 0, 2)),
                  vec(), vec(), mat(), mat(), mat(), vec(), vec()],
        out_specs=tok(),
        compiler_params=_cparams(("arbitrary", "arbitrary")),
        name="readout_merge_out_proj",
    )(x, y_a, y_f, y_b, bonus_f, bonus_b, u, u, u, mod3, gn_g, gn_b, wa, wb, wo, ln_g, ln_b)


def _layer(x, c, ctx, c_ctx, w_ada, b_ada, w_in, na_rpb, rw_mu, rw_w0, rw_w_up, rw_a0, rw_a_up,
           rw_k_k, rw_k_a, rw_r_k, rw_gn_g, rw_gn_b, w_branch_a, w_branch_b, w_out, ln_g, ln_b):
    B, T, _ = x.shape
    L = ctx.shape[1]
    rows = T // GRID_W

    n_cond = -(-(B + 1) // 8) * 8
    cond = jnp.zeros((n_cond, D_MODEL), F32).at[:B].set(c).at[B].set(c_ctx)
    mod3 = _modulation(cond, w_ada, b_ada).reshape(n_cond, 1, 3 * D_MODEL)

    w_bf = jnp.concatenate(
        [w_in[:, :N_MAIN], w_in[:, N_MAIN + 4 * LORA:], w_in[:, N_MAIN:N_MAIN + 4 * LORA],
         jnp.zeros((D_MODEL, N_PAD - N_PROJ), w_in.dtype)], axis=1).astype(BF16)
    u = _in_projection(x, mod3, w_bf, lambda b: b, tm=min(T, 1024), tn=512)
    u_ctx = _in_projection(ctx, mod3, w_bf, lambda b: B, tm=L, tn=512)

    y_a = _neighbourhood_attention(u, u_ctx, _na_bias_table(na_rpb, rows))

    def lora_pad(w_up):
        out = jnp.zeros((2, 2 * LORA, D_MODEL), F32)
        for d in range(2):
            out = out.at[d, d * LORA:(d + 1) * LORA].set(w_up[d])
        return out.astype(BF16)

    params = (rw_mu, rw_w0, lora_pad(rw_w_up), rw_a0, lora_pad(rw_a_up), rw_k_k.reshape(1, D_MODEL),
              rw_k_a.reshape(1, D_MODEL), rw_r_k.reshape(1, D_MODEL))
    state0 = jnp.zeros((B, 2, HEAD_DIM, D_MODEL), F32)
    state_c = _rwkv_scan(u_ctx, state0, params, rotary=False, emit_y=False)[-1]
    y_f, y_b, bonus_f, bonus_b, _ = _rwkv_scan(u, state_c, params, rotary=True, emit_y=True)

    return _final_stage(x, y_a, y_f, y_b, bonus_f, bonus_b, u, mod3, rw_gn_g.reshape(1, D_MODEL),
                        rw_gn_b.reshape(1, D_MODEL), w_branch_a.astype(BF16), w_branch_b.astype(BF16),
                        w_out.astype(BF16), ln_g.reshape(1, D_MODEL), ln_b.reshape(1, D_MODEL),
                        tm=min(T, 512))


def kernel(x, c, ctx, c_ctx, w_ada, b_ada, w_in, na_rpb, rw_mu, rw_w0, rw_w_up, rw_a0, rw_a_up, rw_k_k, rw_k_a, rw_r_k, rw_gn_g, rw_gn_b, w_branch_a, w_branch_b, w_out, ln_g, ln_b):
    assert w_ada.shape[0] == DEPTH
    return _layer(x, c, ctx, c_ctx, w_ada[0], b_ada[0], w_in[0], na_rpb[0], rw_mu[0], rw_w0[0],
                  rw_w_up[0], rw_a0[0], rw_a_up[0], rw_k_k[0], rw_k_a[0], rw_r_k[0], rw_gn_g[0],
                  rw_gn_b[0], w_branch_a[0], w_branch_b[0], w_out[0], ln_g[0], ln_b[0])
```

```python
import functools

import numpy as np
import jax
import jax.numpy as jnp
from jax import lax
from jax.experimental import pallas as pl
from jax.experimental.pallas import tpu as pltpu

F32 = jnp.float32
BF16 = jnp.bfloat16
HIGHEST = lax.Precision.HIGHEST

D_MODEL = 1024
GRID_W = 64
HEADS = 16
HEAD_DIM = 64
NA_MAX_ROWS = 8
NA_COLS = 16
LORA = 64
DEPTH = 1
ROPE_THETA = 10000.0
LN_EPS = 1e-5
GN_EPS = 64e-5
ALPHA = (2 * DEPTH) ** 0.25

LANES = 128
GROUP = 256
HEADS_PER_GROUP = GROUP // HEAD_DIM
N_GROUPS = D_MODEL // GROUP
CHUNK = 64

N_MAIN = 8 * D_MODEL
COL_MG = N_MAIN
COL_LORA = N_MAIN + 2 * D_MODEL
N_PROJ = COL_LORA + 4 * LORA
N_PAD = 21 * 512
VMEM_LIMIT = 56 * 1024 * 1024


def _cparams(sem):
    return pltpu.CompilerParams(dimension_semantics=sem, vmem_limit_bytes=VMEM_LIMIT)


def _silu(x):
    return x * jax.nn.sigmoid(x)


def _mod_kernel(c_ref, w_ref, b_ref, o_ref):
    s = _silu(c_ref[...])
    o_ref[...] = jnp.dot(s, w_ref[...], preferred_element_type=F32, precision=HIGHEST) + b_ref[...]


def _modulation(cond, w_ada, b_ada):
    rows = cond.shape[0]
    return pl.pallas_call(
        _mod_kernel,
        out_shape=jax.ShapeDtypeStruct((rows, 3 * D_MODEL), F32),
        grid=(3,),
        in_specs=[pl.BlockSpec((rows, D_MODEL), lambda j: (0, 0)),
                  pl.BlockSpec((D_MODEL, D_MODEL), lambda j: (0, j)),
                  pl.BlockSpec((1, D_MODEL), lambda j: (0, j))],
        out_specs=pl.BlockSpec((rows, D_MODEL), lambda j: (0, j)),
        compiler_params=_cparams(("arbitrary",)),
        name="ada_modulation",
    )(cond, w_ada, b_ada.reshape(1, 3 * D_MODEL))


def _inproj_kernel(x_ref, sh_ref, sc_ref, w_ref, o_ref, h_scr):
    @pl.when(pl.program_id(2) == 0)
    def _():
        x = x_ref[...]
        mu = jnp.mean(x, axis=-1, keepdims=True)
        xc = x - mu
        var = jnp.mean(xc * xc, axis=-1, keepdims=True)
        y = xc * lax.rsqrt(var + LN_EPS)
        h_scr[...] = (y * (1.0 + sc_ref[...]) + sh_ref[...]).astype(BF16)

    o_ref[...] = jnp.dot(h_scr[...], w_ref[...], preferred_element_type=F32).astype(BF16)


def _in_projection(x, mod3, w_bf, row_of_batch, tm, tn):
    B, T, _ = x.shape
    return pl.pallas_call(
        _inproj_kernel,
        out_shape=jax.ShapeDtypeStruct((B, T, N_PAD), BF16),
        grid=(B, T // tm, N_PAD // tn),
        in_specs=[pl.BlockSpec((None, tm, D_MODEL), lambda b, i, j: (b, i, 0)),
                  pl.BlockSpec((None, 1, D_MODEL), lambda b, i, j: (row_of_batch(b), 0, 0)),
                  pl.BlockSpec((None, 1, D_MODEL), lambda b, i, j: (row_of_batch(b), 0, 1)),
                  pl.BlockSpec((D_MODEL, tn), lambda b, i, j: (0, j))],
        out_specs=pl.BlockSpec((None, tm, tn), lambda b, i, j: (b, i, j)),
        scratch_shapes=[pltpu.VMEM((tm, D_MODEL), BF16)],
        compiler_params=_cparams(("arbitrary", "arbitrary", "arbitrary")),
        name="ln_mod_in_proj",
    )(x, mod3, mod3, w_bf)


NA_HEADS_PER_STEP = GROUP // HEAD_DIM


def _na_kernel(q_ref, k_ref, v_ref, z_ref, kc_ref, vc_ref, bias_ref, o_ref, *, rows):
    kh = min(NA_MAX_ROWS, rows)
    band = kh * GRID_W
    nh = NA_HEADS_PER_STEP
    lane = lax.broadcasted_iota(jnp.int32, (1, GROUP), 1)
    head_lanes = [(lane // HEAD_DIM) == h for h in range(nh)]
    kc = kc_ref[...]
    vc = vc_ref[...]
    nt = (((1,), (1,)), ((), ()))

    def row_body(i, carry):
        rs = jnp.clip(i - kh // 2, 0, rows - kh)
        q0 = pl.multiple_of(i * GRID_W, GRID_W)
        k0 = pl.multiple_of(rs * GRID_W, GRID_W)
        q = q_ref[pl.ds(q0, GRID_W), :] * jnp.asarray(HEAD_DIM ** -0.5, BF16)
        kb = k_ref[pl.ds(k0, band), :]
        vb = v_ref[pl.ds(k0, band), :]
        qs = jnp.concatenate([jnp.where(head_lanes[h], q, jnp.zeros_like(q)) for h in range(nh)], axis=0)
        s = lax.dot_general(qs, kb, nt, preferred_element_type=F32) + bias_ref[i - rs]
        sc = lax.dot_general(qs, kc, nt, preferred_element_type=F32)
        m = jnp.maximum(jnp.max(s, axis=-1, keepdims=True), jnp.max(sc, axis=-1, keepdims=True))
        p = jnp.exp(s - m)
        pc = jnp.exp(sc - m)
        denom = jnp.sum(p, axis=-1, keepdims=True) + jnp.sum(pc, axis=-1, keepdims=True)
        o_all = (jnp.dot(p.astype(BF16), vb, preferred_element_type=F32)
                 + jnp.dot(pc.astype(BF16), vc, preferred_element_type=F32)) / denom
        o = jnp.where(head_lanes[0], o_all[0:GRID_W], 0.0)
        for h in range(1, nh):
            o = o + jnp.where(head_lanes[h], o_all[h * GRID_W:(h + 1) * GRID_W], 0.0)
        z = z_ref[pl.ds(q0, GRID_W), :].astype(F32)
        o_ref[pl.ds(q0, GRID_W), :] = (o * _silu(z)).astype(BF16)
        return carry

    lax.fori_loop(0, rows, row_body, 0, unroll=4)


def _na_bias_table(rpb, rows):
    kh = min(NA_MAX_ROWS, rows)
    nh = NA_HEADS_PER_STEP
    n_dj = 2 * NA_COLS - 1
    cols = np.arange(GRID_W)
    cstart = np.clip(cols - NA_COLS // 2, 0, GRID_W - NA_COLS)
    col_mask = (cols[None, :] >= cstart[:, None]) & (cols[None, :] < cstart[:, None] + NA_COLS)
    dj = np.clip(cols[None, :] - cols[:, None] + NA_COLS - 1, 0, n_dj - 1)
    onehot = (np.arange(n_dj)[:, None] == dj.reshape(1, -1)).astype(np.float32)
    exp = jnp.dot(rpb.reshape(-1, n_dj), jnp.asarray(onehot), precision=HIGHEST)
    exp = exp.reshape(HEADS, 2 * NA_MAX_ROWS - 1, GRID_W, GRID_W)
    exp = jnp.where(col_mask[None, None], exp, -jnp.inf)
    lo = NA_MAX_ROWS - 1
    tab = jnp.stack([exp[:, lo - o: lo - o + kh] for o in range(kh)], axis=1)
    tab = tab.reshape(HEADS // nh, nh, kh, kh, GRID_W, GRID_W)
    return tab.transpose(0, 2, 1, 4, 3, 5).reshape(HEADS // nh, kh, nh * GRID_W, kh * GRID_W)


def _neighbourhood_attention(u, u_ctx, bias_tab):
    B, T, _ = u.shape
    L = u_ctx.shape[1]
    rows = T // GRID_W
    kh = min(NA_MAX_ROWS, rows)
    blk = D_MODEL // GROUP

    def col(part):
        return pl.BlockSpec((None, T, GROUP), lambda b, g: (b, 0, part * blk + g))

    def col_ctx(part):
        return pl.BlockSpec((None, L, GROUP), lambda b, g: (b, 0, part * blk + g))

    return pl.pallas_call(
        functools.partial(_na_kernel, rows=rows),
        out_shape=jax.ShapeDtypeStruct((B, T, D_MODEL), BF16),
        grid=(B, blk),
        in_specs=[col(0), col(1), col(2), col(3), col_ctx(1), col_ctx(2),
                  pl.BlockSpec((None, kh, NA_HEADS_PER_STEP * GRID_W, kh * GRID_W), lambda b, g: (g, 0, 0, 0))],
        out_specs=pl.BlockSpec((None, T, GROUP), lambda b, g: (b, 0, g)),
        compiler_params=_cparams(("arbitrary", "arbitrary")),
        name="neighbourhood_attention",
    )(u, u, u, u, u_ctx, u_ctx, bias_tab)


def _block_diag_masks():
    r = lax.broadcasted_iota(jnp.int32, (GROUP, GROUP), 0)
    c = lax.broadcasted_iota(jnp.int32, (GROUP, GROUP), 1)
    return (r // HEAD_DIM) == (c // HEAD_DIM)


def _head_sum(x, ones_bd):
    rows = x.shape[0]
    xb = x.astype(BF16)
    stacked = jnp.concatenate([xb[:, g * GROUP:(g + 1) * GROUP] for g in range(N_GROUPS)], axis=0)
    sums = jnp.dot(stacked, ones_bd, preferred_element_type=F32)
    return jnp.concatenate([sums[g * rows:(g + 1) * rows] for g in range(N_GROUPS)], axis=1)


def _rwkv_prepare(d, cidx, n_chunks, refs, prm, ones_bd, rotary, ops_ref, pc_ref):
    (r_ref, k_ref, v_ref, rp_ref, kp_ref, vp_ref, rn_ref, kn_ref, vn_ref, lo_ref, cos_ref, sin_ref,
     bonus_ref) = refs
    mu_ref, w0_ref, wup_ref, a0_ref, aup_ref, kk_ref, ka_ref, rk_ref = prm
    C = CHUNK
    halo = rp_ref.shape[0]

    def store(name, value):
        ops_ref[d * len(OPERANDS) + OPERANDS.index(name)] = value.astype(BF16)

    tt = lax.broadcasted_iota(jnp.int32, (C, C + 2 * halo), 0)
    ss = lax.broadcasted_iota(jnp.int32, (C, C + 2 * halo), 1) - halo
    first = jnp.where(cidx > 0, -1, 0)
    last = jnp.where(cidx < n_chunks - 1, C, C - 1)
    nb = (jnp.abs(ss - tt) == 1) & (ss >= first) & (ss <= last)
    nb = jnp.where(nb, 1.0, 0.0).astype(BF16)

    def shifted(x_ref, p_ref, n_ref, mu):
        x = x_ref[...]
        both = jnp.dot(nb, jnp.concatenate([p_ref[...], x, n_ref[...]], axis=0), preferred_element_type=F32)
        return (1.0 - mu) * x.astype(F32) + (0.5 * mu) * both

    r_s = shifted(r_ref, rp_ref, rn_ref, mu_ref[0:1, :])
    yield
    k_s = shifted(k_ref, kp_ref, kn_ref, mu_ref[1:2, :])
    yield
    v_s = shifted(v_ref, vp_ref, vn_ref, mu_ref[2:3, :])
    store("v", v_s)
    yield

    if rotary:
        lane = lax.broadcasted_iota(jnp.int32, (1, D_MODEL), 1)
        low = (lane % 32) < 16
        cos_t = jnp.tile(cos_ref[...], (1, D_MODEL // LANES))
        sin_t = jnp.tile(sin_ref[...], (1, D_MODEL // LANES))

        def rope(x):
            partner = jnp.where(low, pltpu.roll(x, D_MODEL - 16, axis=1), pltpu.roll(x, 16, axis=1))
            return x * cos_t + partner * sin_t

        r_s = rope(r_s)
        yield
        k_s = rope(k_s)
        yield

    lo = lo_ref[...]
    lw = w0_ref[d:d + 1, :] + jnp.dot(jnp.tanh(lo[:, :2 * LORA].astype(F32)).astype(BF16), wup_ref[d],
                                     preferred_element_type=F32)
    ld = (-np.exp(-0.5)) * jax.nn.sigmoid(lw)
    yield
    a = jax.nn.sigmoid(a0_ref[d:d + 1, :] + jnp.dot(lo[:, 2 * LORA:], aup_ref[d], preferred_element_type=F32))
    yield

    kk = k_s * kk_ref[...]
    kk = kk * jnp.minimum(lax.rsqrt(_head_sum(kk * kk, ones_bd)), 1e12)
    yield
    k_dir = k_s * (1.0 + (a - 1.0) * ka_ref[...])
    b_vec = kk * a
    bonus_ref[...] = (_head_sum(r_s * rk_ref[...] * k_dir, ones_bd) * v_s).astype(bonus_ref.dtype)
    yield

    tt = lax.broadcasted_iota(jnp.int32, (C, C), 0)
    ss = lax.broadcasted_iota(jnp.int32, (C, C), 1)
    tri = jnp.where((ss <= tt) if d == 0 else (ss >= tt), 1.0, 0.0).astype(BF16)
    ld_hi = ld.astype(BF16)
    ld_lo = (ld - ld_hi.astype(F32)).astype(BF16)
    cl = (jnp.dot(tri, ld_hi, preferred_element_type=F32) + jnp.dot(tri, ld_lo, preferred_element_type=F32))
    cl_tot = cl[C - 1:C, :] if d == 0 else cl[0:1, :]
    yield
    e_neg = jnp.exp(-cl)
    p_c = jnp.exp(cl_tot)
    pc_ref[d] = jnp.broadcast_to(p_c, pc_ref.shape[1:])
    b_t = b_vec * e_neg
    store("b", b_t)
    store("bh", b_t * p_c)
    yield
    k_t = k_dir * e_neg
    store("k", k_t)
    store("kh", k_t * p_c)
    yield
    store("a", -kk * jnp.exp(cl - ld))
    store("r", r_s * jnp.exp(cl))


OPERANDS = ("a", "r", "b", "k", "bh", "kh", "v")


def _interleave(*gens):
    live = list(gens)
    while live:
        for gen in list(live):
            try:
                next(gen)
            except StopIteration:
                live.remove(gen)


def _rwkv_chunk_matmuls(ops_ref, pc_ref, h_scr, yf_ref, yb_ref, same_head, emit_y):
    C = CHUNK
    n_ops = len(OPERANDS)
    n_g = 2 * N_GROUPS
    dirs = [g // N_GROUPS for g in range(n_g)]

    def lanes_of(g):
        return slice((g % N_GROUPS) * GROUP, (g % N_GROUPS + 1) * GROUP)

    def grp(name):
        i = OPERANDS.index(name)
        return [ops_ref[dirs[g] * n_ops + i, :, lanes_of(g)] for g in range(n_g)]

    t_i = lax.broadcasted_iota(jnp.int32, (C, GROUP), 0)
    s_i = lax.broadcasted_iota(jnp.int32, (C, GROUP), 1) % C
    before = [s_i < t_i, s_i > t_i]
    upto = [s_i <= t_i, s_i >= t_i]
    eye = jnp.where(t_i == s_i, 1.0, 0.0).astype(F32)
    nt = (((1,), (1,)), ((), ()))
    tn = (((0,), (0,)), ((), ()))

    def bd(x):
        return jnp.where(same_head, jnp.tile(x, (HEADS_PER_GROUP, 1)), jnp.zeros((), x.dtype))

    def hdot(a_side, x):
        return jnp.dot(a_side, bd(x), preferred_element_type=F32)

    def masked(m, keep):
        return jnp.where(keep, m, 0.0).astype(BF16)

    at, rt = grp("a"), grp("r")
    lhs = [jnp.concatenate([at[g], rt[g]], axis=0) for g in range(n_g)] if emit_y else at
    bt = grp("b")
    m_b = [lax.dot_general(lhs[g], bd(bt[g]), nt, preferred_element_type=F32) for g in range(n_g)]
    yield
    kt = grp("k")
    m_k = [lax.dot_general(lhs[g], bd(kt[g]), nt, preferred_element_type=F32) for g in range(n_g)]
    yield
    a_ab = [masked(m_b[g][:C], before[dirs[g]]) for g in range(n_g)]
    a_ak = [masked(m_k[g][:C], before[dirs[g]]) for g in range(n_g)]
    vv = grp("v")
    if emit_y:
        a_rb = [masked(m_b[g][C:], upto[dirs[g]]) for g in range(n_g)]
        a_rk = [masked(m_k[g][C:], upto[dirs[g]]) for g in range(n_g)]
        w_v = [hdot(jnp.concatenate([a_ak[g], a_rk[g]], axis=0), vv[g]) for g in range(n_g)]
    else:
        w_v = [hdot(a_ak[g], vv[g]) for g in range(n_g)]
    yield

    npow = [hdot(a_ab[g], a_ab[g]) for g in range(n_g)]
    inv = [eye + a_ab[g].astype(F32) for g in range(n_g)]
    yield
    n_factors = C.bit_length() - 1
    for j in range(1, n_factors):
        last = j == n_factors - 1
        pw = [npow[g].astype(BF16) for g in range(n_g)]
        lhs_j = [inv[g].astype(BF16) if last else jnp.concatenate([inv[g].astype(BF16), pw[g]], axis=0)
                 for g in range(n_g)]
        prod = [hdot(lhs_j[g], pw[g]) for g in range(n_g)]
        inv = [inv[g] + prod[g][:C] for g in range(n_g)]
        if not last:
            npow = [prod[g][C:] for g in range(n_g)]
        yield
    inv = [inv[g].astype(BF16) for g in range(n_g)]

    a_p = [hdot(inv[g], at[g]).astype(BF16) for g in range(n_g)]
    yield
    u0 = [hdot(inv[g], w_v[g][:C].astype(BF16)) for g in range(n_g)]
    yield

    h_old = [h_scr[:, g * GROUP:(g + 1) * GROUP] for g in range(n_g)]
    decay_rows = [(eye * pc_ref[dirs[g], 0:1, lanes_of(g)]).astype(BF16) for g in range(n_g)]
    stack = [jnp.concatenate([a_p[g]] + ([rt[g]] if emit_y else []) + [decay_rows[g]], axis=0)
             for g in range(n_g)]
    s1 = [hdot(stack[g], h_old[g].astype(BF16)) for g in range(n_g)]
    u = [(s1[g][:C] + u0[g]).astype(BF16) for g in range(n_g)]
    yield
    if emit_y:
        y_u = [hdot(a_rb[g], u[g]) for g in range(n_g)]
        for g in range(n_g):
            y_ref = yf_ref if dirs[g] == 0 else yb_ref
            y_ref[:, lanes_of(g)] = (s1[g][C:2 * C] + y_u[g] + w_v[g][C:]).astype(y_ref.dtype)
    else:
        yf_ref[...] = jnp.zeros(yf_ref.shape, yf_ref.dtype)
        yb_ref[...] = jnp.zeros(yb_ref.shape, yb_ref.dtype)
    yield

    bh, kh = grp("bh"), grp("kh")
    full = [lax.dot_general(jnp.concatenate([bh[g], kh[g]], axis=0),
                            jnp.concatenate([u[g], vv[g]], axis=0), tn, preferred_element_type=F32)
            for g in range(n_g)]
    lane_head = lax.broadcasted_iota(jnp.int32, (HEAD_DIM, GROUP), 1) // HEAD_DIM
    for g in range(n_g):
        upd = full[g][(HEADS_PER_GROUP - 1) * HEAD_DIM:]
        for j in range(HEADS_PER_GROUP - 2, -1, -1):
            upd = jnp.where(lane_head == j, full[g][j * HEAD_DIM:(j + 1) * HEAD_DIM], upd)
        h_scr[:, g * GROUP:(g + 1) * GROUP] = s1[g][-C:] + upd


def _rwkv_kernel(*refs, n_chunks, rotary, emit_y):
    per_dir = 12
    in_f, in_b = refs[:per_dir], refs[per_dir:2 * per_dir]
    prm = refs[2 * per_dir:2 * per_dir + 8]
    s0_ref = refs[2 * per_dir + 8]
    yf_ref, yb_ref, bonf_ref, bonb_ref, sout_ref, h_scr, ops_scr, pc_scr = refs[2 * per_dir + 9:]
    step = pl.program_id(1)

    @pl.when(step == 0)
    def _():
        h_scr[:, :D_MODEL] = s0_ref[0]
        h_scr[:, D_MODEL:] = s0_ref[1]
        ops_scr[0] = jnp.zeros(ops_scr.shape[1:], ops_scr.dtype)
        pc_scr[0] = jnp.ones(pc_scr.shape[1:], pc_scr.dtype)

    nxt = jnp.minimum(step, n_chunks - 1)

    def body(read, write):
        same_head = _block_diag_masks()
        ones_bd = jnp.where(same_head, 1.0, 0.0).astype(BF16)
        _interleave(
            _rwkv_chunk_matmuls(ops_scr.at[read], pc_scr.at[read], h_scr, yf_ref, yb_ref, same_head, emit_y),
            _rwkv_prepare(0, nxt, n_chunks, in_f + (bonf_ref,), prm, ones_bd, rotary,
                          ops_scr.at[write], pc_scr.at[write]),
            _rwkv_prepare(1, n_chunks - 1 - nxt, n_chunks, in_b + (bonb_ref,), prm, ones_bd, rotary,
                          ops_scr.at[write], pc_scr.at[write]))

    for parity in range(2):
        pl.when(step % 2 == parity)(functools.partial(body, parity, 1 - parity))

    @pl.when(step == n_chunks)
    def _():
        sout_ref[0] = h_scr[:, :D_MODEL]
        sout_ref[1] = h_scr[:, D_MODEL:]


def _rope_tables(T):
    half = HEAD_DIM // 2
    n_freq = half // 2
    t = np.arange(T)[:, None]
    lane = np.arange(LANES)[None, :]
    inv_freq = ROPE_THETA ** (-(np.arange(n_freq, dtype=np.float32)) / n_freq)
    pos = np.where((lane % HEAD_DIM) < half, t // GRID_W, t % GRID_W).astype(np.float32)
    ang = (pos * inv_freq[lane % n_freq].astype(np.float32)).astype(np.float32)
    sign = np.where((lane % half) < n_freq, -1.0, 1.0)
    return jnp.asarray(np.cos(ang), F32), jnp.asarray(np.sin(ang) * sign, F32)


def _rwkv_scan(u, state0, params, rotary, emit_y):
    B, T, _ = u.shape
    C = CHUNK
    n_chunks = T // C
    halo = 16
    per = C // halo

    def mirror(d, c):
        return c if d == 0 else n_chunks - 1 - c

    def chunk_of(d, s):
        return mirror(d, jnp.minimum(s, n_chunks - 1))

    def done_of(d, s):
        return mirror(d, jnp.maximum(s - 1, 0))

    def specs(d):
        def main(part):
            return pl.BlockSpec((None, C, D_MODEL), lambda b, c: (b, chunk_of(d, c), part))

        def prev(part):
            return pl.BlockSpec((None, halo, D_MODEL),
                                lambda b, c: (b, jnp.maximum(chunk_of(d, c) * per - 1, 0), part))

        def nxt(part):
            return pl.BlockSpec((None, halo, D_MODEL),
                                lambda b, c: (b, jnp.minimum((chunk_of(d, c) + 1) * per, T // halo - 1), part))

        r_blk, k_blk, v_blk = 4, 5, 6
        lora = pl.BlockSpec((None, C, 4 * LORA), lambda b, c: (b, chunk_of(d, c), COL_LORA // (4 * LORA)))
        tab = pl.BlockSpec((C, LANES), lambda b, c: (chunk_of(d, c), 0))
        return [main(r_blk), main(k_blk), main(v_blk), prev(r_blk), prev(k_blk), prev(v_blk),
                nxt(r_blk), nxt(k_blk), nxt(v_blk), lora, tab, tab]

    def whole(shape):
        return pl.BlockSpec(shape, lambda b, c: (0,) * len(shape))

    cos_t, sin_t = _rope_tables(T)
    state = pl.BlockSpec((None, 2, HEAD_DIM, D_MODEL), lambda b, c: (b, 0, 0, 0))
    param_specs = [whole((3, D_MODEL)), whole((2, D_MODEL)), whole((2, 2 * LORA, D_MODEL)),
                   whole((2, D_MODEL)), whole((2, 2 * LORA, D_MODEL)),
                   whole((1, D_MODEL)), whole((1, D_MODEL)), whole((1, D_MODEL))]
    seq = jax.ShapeDtypeStruct((B, T, D_MODEL), BF16)
    y_f = pl.BlockSpec((None, C, D_MODEL), lambda b, c: (b, done_of(0, c), 0))
    y_b = pl.BlockSpec((None, C, D_MODEL), lambda b, c: (b, done_of(1, c), 0))
    bon_f = pl.BlockSpec((None, C, D_MODEL), lambda b, c: (b, chunk_of(0, c), 0))
    bon_b = pl.BlockSpec((None, C, D_MODEL), lambda b, c: (b, chunk_of(1, c), 0))
    seq_in = (u,) * 10 + (cos_t, sin_t)
    return pl.pallas_call(
        functools.partial(_rwkv_kernel, n_chunks=n_chunks, rotary=rotary, emit_y=emit_y),
        out_shape=(seq, seq, seq, seq, jax.ShapeDtypeStruct((B, 2, HEAD_DIM, D_MODEL), F32)),
        grid=(B, n_chunks + 1),
        in_specs=specs(0) + specs(1) + param_specs + [state],
        out_specs=(y_f, y_b, bon_f, bon_b, state),
        scratch_shapes=[pltpu.VMEM((HEAD_DIM, 2 * D_MODEL), F32),
                        pltpu.VMEM((2, 2 * len(OPERANDS), C, D_MODEL), BF16),
                        pltpu.VMEM((2, 2, 8, D_MODEL), F32)],
        compiler_params=_cparams(("arbitrary", "arbitrary")),
        name="rwkv7_chunk_scan_rot" if rotary else "rwkv7_chunk_scan_ctx",
    )(*seq_in, *seq_in, *params, state0)


def _final_kernel(x_ref, ya_ref, yf_ref, yb_ref, bf_ref, bb_ref, z_ref, ga_ref, gb_ref, gate_ref,
                  gng_ref, gnb_ref, wa_ref, wb_ref, wo_ref, lng_ref, lnb_ref, o_ref):
    same_head = _block_diag_masks()
    ones_bd = jnp.where(same_head, 1.0, 0.0).astype(BF16)
    y = yf_ref[...].astype(F32) + yb_ref[...].astype(F32)
    mu = _head_sum(y, ones_bd) * (1.0 / HEAD_DIM)
    yc = y - mu
    var = _head_sum(yc * yc, ones_bd) * (1.0 / HEAD_DIM)
    yn = yc * lax.rsqrt(var + GN_EPS) * gng_ref[...] + gnb_ref[...]
    yn = yn + bf_ref[...].astype(F32) + bb_ref[...].astype(F32)
    y_b = (yn * _silu(z_ref[...].astype(F32))).astype(BF16)
    p_a = jnp.dot(ya_ref[...], wa_ref[...], preferred_element_type=F32)
    p_b = jnp.dot(y_b, wb_ref[...], preferred_element_type=F32)
    merged = (jax.nn.sigmoid(ga_ref[...].astype(F32)) * p_a
              + jax.nn.sigmoid(gb_ref[...].astype(F32)) * p_b)
    out = jnp.dot(merged.astype(BF16), wo_ref[...], preferred_element_type=F32)
    t = ALPHA * x_ref[...] + gate_ref[...] * out
    m = jnp.mean(t, axis=-1, keepdims=True)
    tc = t - m
    v = jnp.mean(tc * tc, axis=-1, keepdims=True)
    o_ref[...] = tc * lax.rsqrt(v + LN_EPS) * lng_ref[...] + lnb_ref[...]


def _final_stage(x, y_a, y_f, y_b, bonus_f, bonus_b, u, mod3, gn_g, gn_b, wa, wb, wo, ln_g, ln_b, tm):
    B, T, _ = x.shape

    def tok():
        return pl.BlockSpec((None, tm, D_MODEL), lambda b, i: (b, i, 0))

    def ucol(blk):
        return pl.BlockSpec((None, tm, D_MODEL), lambda b, i: (b, i, blk))

    def vec():
        return pl.BlockSpec((1, D_MODEL), lambda b, i: (0, 0))

    def mat():
        return pl.BlockSpec((D_MODEL, D_MODEL), lambda b, i: (0, 0))

    return pl.pallas_call(
        _final_kernel,
        out_shape=jax.ShapeDtypeStruct((B, T, D_MODEL), F32),
        grid=(B, T // tm),
        in_specs=[tok(), tok(), tok(), tok(), tok(), tok(), ucol(7), ucol(8), ucol(9),
                  pl.BlockSpec((None, 1, D_MODEL), lambda b, i: (b, 0, 2)),
                  vec(), vec(), mat(), mat(), mat(), vec(), vec()],
        out_specs=tok(),
        compiler_params=_cparams(("arbitrary", "arbitrary")),
        name="readout_merge_out_proj",
    )(x, y_a, y_f, y_b, bonus_f, bonus_b, u, u, u, mod3, gn_g, gn_b, wa, wb, wo, ln_g, ln_b)


def _layer(x, c, ctx, c_ctx, w_ada, b_ada, w_in, na_rpb, rw_mu, rw_w0, rw_w_up, rw_a0, rw_a_up,
           rw_k_k, rw_k_a, rw_r_k, rw_gn_g, rw_gn_b, w_branch_a, w_branch_b, w_out, ln_g, ln_b):
    B, T, _ = x.shape
    L = ctx.shape[1]
    rows = T // GRID_W

    n_cond = -(-(B + 1) // 8) * 8
    cond = jnp.zeros((n_cond, D_MODEL), F32).at[:B].set(c).at[B].set(c_ctx)
    mod3 = _modulation(cond, w_ada, b_ada).reshape(n_cond, 1, 3 * D_MODEL)

    w_bf = jnp.concatenate(
        [w_in[:, :N_MAIN], w_in[:, N_MAIN + 4 * LORA:], w_in[:, N_MAIN:N_MAIN + 4 * LORA],
         jnp.zeros((D_MODEL, N_PAD - N_PROJ), w_in.dtype)], axis=1).astype(BF16)
    u = _in_projection(x, mod3, w_bf, lambda b: b, tm=min(T, 1024), tn=1536)
    u_ctx = _in_projection(ctx, mod3, w_bf, lambda b: B, tm=L, tn=1536)

    y_a = _neighbourhood_attention(u, u_ctx, _na_bias_table(na_rpb, rows))

    def lora_pad(w_up):
        out = jnp.zeros((2, 2 * LORA, D_MODEL), F32)
        for d in range(2):
            out = out.at[d, d * LORA:(d + 1) * LORA].set(w_up[d])
        return out.astype(BF16)

    params = (rw_mu, rw_w0, lora_pad(rw_w_up), rw_a0, lora_pad(rw_a_up), rw_k_k.reshape(1, D_MODEL),
              rw_k_a.reshape(1, D_MODEL), rw_r_k.reshape(1, D_MODEL))
    state0 = jnp.zeros((B, 2, HEAD_DIM, D_MODEL), F32)
    state_c = _rwkv_scan(u_ctx, state0, params, rotary=False, emit_y=False)[-1]
    y_f, y_b, bonus_f, bonus_b, _ = _rwkv_scan(u, state_c, params, rotary=True, emit_y=True)

    return _final_stage(x, y_a, y_f, y_b, bonus_f, bonus_b, u, mod3, rw_gn_g.reshape(1, D_MODEL),
                        rw_gn_b.reshape(1, D_MODEL), w_branch_a.astype(BF16), w_branch_b.astype(BF16),
                        w_out.astype(BF16), ln_g.reshape(1, D_MODEL), ln_b.reshape(1, D_MODEL),
                        tm=min(T, 512))


def kernel(x, c, ctx, c_ctx, w_ada, b_ada, w_in, na_rpb, rw_mu, rw_w0, rw_w_up, rw_a0, rw_a_up, rw_k_k, rw_k_a, rw_r_k, rw_gn_g, rw_gn_b, w_branch_a, w_branch_b, w_out, ln_g, ln_b):
    assert w_ada.shape[0] == DEPTH
    return _layer(x, c, ctx, c_ctx, w_ada[0], b_ada[0], w_in[0], na_rpb[0], rw_mu[0], rw_w0[0],
                  rw_w_up[0], rw_a0[0], rw_a_up[0], rw_k_k[0], rw_k_a[0], rw_r_k[0], rw_gn_g[0],
                  rw_gn_b[0], w_branch_a[0], w_branch_b[0], w_out[0], ln_g[0], ln_b[0])
```

```python
import functools

import numpy as np
import jax
import jax.numpy as jnp
from jax import lax
from jax.experimental import pallas as pl
from jax.experimental.pallas import tpu as pltpu

F32 = jnp.float32
BF16 = jnp.bfloat16
HIGHEST = lax.Precision.HIGHEST

D_MODEL = 1024
GRID_W = 64
HEADS = 16
HEAD_DIM = 64
NA_MAX_ROWS = 8
NA_COLS = 16
LORA = 64
DEPTH = 1
ROPE_THETA = 10000.0
LN_EPS = 1e-5
GN_EPS = 64e-5
ALPHA = (2 * DEPTH) ** 0.25

LANES = 128
GROUP = 256
HEADS_PER_GROUP = GROUP // HEAD_DIM
N_GROUPS = D_MODEL // GROUP
CHUNK = 64

COL_RKV = 0
COL_NA = 3 * D_MODEL
COL_ZB = 7 * D_MODEL
N_MAIN = 8 * D_MODEL
COL_MG = N_MAIN
COL_LORA = N_MAIN + 2 * D_MODEL
N_PROJ = COL_LORA + 4 * LORA
N_PAD = 21 * 512
VMEM_LIMIT = 56 * 1024 * 1024


def _cparams(sem):
    return pltpu.CompilerParams(dimension_semantics=sem, vmem_limit_bytes=VMEM_LIMIT)


def _silu(x):
    return x * jax.nn.sigmoid(x)


def _mod_kernel(c_ref, w_ref, b_ref, o_ref):
    s = _silu(c_ref[...])
    o_ref[...] = jnp.dot(s, w_ref[...], preferred_element_type=F32, precision=HIGHEST) + b_ref[...]


def _modulation(cond, w_ada, b_ada):
    rows = cond.shape[0]
    return pl.pallas_call(
        _mod_kernel,
        out_shape=jax.ShapeDtypeStruct((rows, 3 * D_MODEL), F32),
        grid=(3,),
        in_specs=[pl.BlockSpec((rows, D_MODEL), lambda j: (0, 0)),
                  pl.BlockSpec((D_MODEL, D_MODEL), lambda j: (0, j)),
                  pl.BlockSpec((1, D_MODEL), lambda j: (0, j))],
        out_specs=pl.BlockSpec((rows, D_MODEL), lambda j: (0, j)),
        compiler_params=_cparams(("arbitrary",)),
        name="ada_modulation",
    )(cond, w_ada, b_ada.reshape(1, 3 * D_MODEL))


def _inproj_kernel(x_ref, sh_ref, sc_ref, w_ref, o_ref, h_scr):
    @pl.when(pl.program_id(2) == 0)
    def _():
        x = x_ref[...]
        mu = jnp.mean(x, axis=-1, keepdims=True)
        xc = x - mu
        var = jnp.mean(xc * xc, axis=-1, keepdims=True)
        y = xc * lax.rsqrt(var + LN_EPS)
        h_scr[...] = (y * (1.0 + sc_ref[...]) + sh_ref[...]).astype(BF16)

    o_ref[...] = jnp.dot(h_scr[...], w_ref[...], preferred_element_type=F32).astype(BF16)


def _in_projection(x, mod3, w_bf, row_of_batch, tm, tn):
    B, T, _ = x.shape
    return pl.pallas_call(
        _inproj_kernel,
        out_shape=jax.ShapeDtypeStruct((B, T, N_PAD), BF16),
        grid=(B, T // tm, N_PAD // tn),
        in_specs=[pl.BlockSpec((None, tm, D_MODEL), lambda b, i, j: (b, i, 0)),
                  pl.BlockSpec((None, 1, D_MODEL), lambda b, i, j: (row_of_batch(b), 0, 0)),
                  pl.BlockSpec((None, 1, D_MODEL), lambda b, i, j: (row_of_batch(b), 0, 1)),
                  pl.BlockSpec((D_MODEL, tn), lambda b, i, j: (0, j))],
        out_specs=pl.BlockSpec((None, tm, tn), lambda b, i, j: (b, i, j)),
        scratch_shapes=[pltpu.VMEM((tm, D_MODEL), BF16)],
        compiler_params=_cparams(("arbitrary", "arbitrary", "arbitrary")),
        name="ln_mod_in_proj",
    )(x, mod3, mod3, w_bf)


NA_HEADS_PER_STEP = GROUP // HEAD_DIM


def _na_kernel(q_ref, k_ref, v_ref, z_ref, kc_ref, vc_ref, bias_ref, o_ref, *, rows):
    kh = min(NA_MAX_ROWS, rows)
    band = kh * GRID_W
    nh = NA_HEADS_PER_STEP
    lane = lax.broadcasted_iota(jnp.int32, (1, GROUP), 1)
    head_lanes = [(lane // HEAD_DIM) == h for h in range(nh)]
    kc = kc_ref[...]
    vc = vc_ref[...]
    nt = (((1,), (1,)), ((), ()))

    def row_body(i, carry):
        rs = jnp.clip(i - kh // 2, 0, rows - kh)
        q0 = pl.multiple_of(i * GRID_W, GRID_W)
        k0 = pl.multiple_of(rs * GRID_W, GRID_W)
        q = q_ref[pl.ds(q0, GRID_W), :] * jnp.asarray(HEAD_DIM ** -0.5, BF16)
        kb = k_ref[pl.ds(k0, band), :]
        vb = v_ref[pl.ds(k0, band), :]
        qs = jnp.concatenate([jnp.where(head_lanes[h], q, jnp.zeros_like(q)) for h in range(nh)], axis=0)
        s = lax.dot_general(qs, kb, nt, preferred_element_type=F32) + bias_ref[i - rs]
        sc = lax.dot_general(qs, kc, nt, preferred_element_type=F32)
        m = jnp.maximum(jnp.max(s, axis=-1, keepdims=True), jnp.max(sc, axis=-1, keepdims=True))
        p = jnp.exp(s - m)
        pc = jnp.exp(sc - m)
        denom = jnp.sum(p, axis=-1, keepdims=True) + jnp.sum(pc, axis=-1, keepdims=True)
        o_all = (jnp.dot(p.astype(BF16), vb, preferred_element_type=F32)
                 + jnp.dot(pc.astype(BF16), vc, preferred_element_type=F32)) / denom
        o = jnp.where(head_lanes[0], o_all[0:GRID_W], 0.0)
        for h in range(1, nh):
            o = o + jnp.where(head_lanes[h], o_all[h * GRID_W:(h + 1) * GRID_W], 0.0)
        z = z_ref[pl.ds(q0, GRID_W), :].astype(F32)
        o_ref[pl.ds(q0, GRID_W), :] = (o * _silu(z)).astype(BF16)
        return carry

    lax.fori_loop(0, rows, row_body, 0, unroll=4)


def _na_bias_table(rpb, rows):
    kh = min(NA_MAX_ROWS, rows)
    nh = NA_HEADS_PER_STEP
    n_dj = 2 * NA_COLS - 1
    cols = np.arange(GRID_W)
    cstart = np.clip(cols - NA_COLS // 2, 0, GRID_W - NA_COLS)
    col_mask = (cols[None, :] >= cstart[:, None]) & (cols[None, :] < cstart[:, None] + NA_COLS)
    dj = np.clip(cols[None, :] - cols[:, None] + NA_COLS - 1, 0, n_dj - 1)
    onehot = (np.arange(n_dj)[:, None] == dj.reshape(1, -1)).astype(np.float32)
    exp = jnp.dot(rpb.reshape(-1, n_dj), jnp.asarray(onehot), precision=HIGHEST)
    exp = exp.reshape(HEADS, 2 * NA_MAX_ROWS - 1, GRID_W, GRID_W)
    exp = jnp.where(col_mask[None, None], exp, -jnp.inf)
    lo = NA_MAX_ROWS - 1
    tab = jnp.stack([exp[:, lo - o: lo - o + kh] for o in range(kh)], axis=1)
    tab = tab.reshape(HEADS // nh, nh, kh, kh, GRID_W, GRID_W)
    return tab.transpose(0, 2, 1, 4, 3, 5).reshape(HEADS // nh, kh, nh * GRID_W, kh * GRID_W)


def _neighbourhood_attention(u, u_ctx, bias_tab):
    B, T, _ = u.shape
    L = u_ctx.shape[1]
    rows = T // GRID_W
    kh = min(NA_MAX_ROWS, rows)
    blk = D_MODEL // GROUP
    first = COL_NA // GROUP

    def col(part):
        return pl.BlockSpec((None, T, GROUP), lambda b, g: (b, 0, first + part * blk + g))

    def col_ctx(part):
        return pl.BlockSpec((None, L, GROUP), lambda b, g: (b, 0, first + part * blk + g))

    return pl.pallas_call(
        functools.partial(_na_kernel, rows=rows),
        out_shape=jax.ShapeDtypeStruct((B, T, D_MODEL), BF16),
        grid=(B, blk),
        in_specs=[col(0), col(1), col(2), col(3), col_ctx(1), col_ctx(2),
                  pl.BlockSpec((None, kh, NA_HEADS_PER_STEP * GRID_W, kh * GRID_W), lambda b, g: (g, 0, 0, 0))],
        out_specs=pl.BlockSpec((None, T, GROUP), lambda b, g: (b, 0, g)),
        compiler_params=_cparams(("arbitrary", "arbitrary")),
        name="neighbourhood_attention",
    )(u, u, u, u, u_ctx, u_ctx, bias_tab)


def _block_diag_masks():
    r = lax.broadcasted_iota(jnp.int32, (GROUP, GROUP), 0)
    c = lax.broadcasted_iota(jnp.int32, (GROUP, GROUP), 1)
    return (r // HEAD_DIM) == (c // HEAD_DIM)


def _head_sum(x, ones_bd):
    rows = x.shape[0]
    xb = x.astype(BF16)
    stacked = jnp.concatenate([xb[:, g * GROUP:(g + 1) * GROUP] for g in range(N_GROUPS)], axis=0)
    sums = jnp.dot(stacked, ones_bd, preferred_element_type=F32)
    return jnp.concatenate([sums[g * rows:(g + 1) * rows] for g in range(N_GROUPS)], axis=1)


def _rwkv_prepare(d, cidx, n_chunks, refs, prm, ones_bd, rotary, ops_ref, pc_ref):
    rkv_ref, prev_ref, next_ref, lo_ref, rope_ref, bonus_ref = refs
    mu_ref, w0_ref, wup_ref, a0_ref, aup_ref, kk_ref, ka_ref, rk_ref = prm
    C = CHUNK
    halo = prev_ref.shape[0]

    def store(name, value):
        ops_ref[d * len(OPERANDS) + OPERANDS.index(name)] = value.astype(BF16)

    tt = lax.broadcasted_iota(jnp.int32, (C, C + 2 * halo), 0)
    ss = lax.broadcasted_iota(jnp.int32, (C, C + 2 * halo), 1) - halo
    first = jnp.where(cidx > 0, -1, 0)
    last = jnp.where(cidx < n_chunks - 1, C, C - 1)
    nb = (jnp.abs(ss - tt) == 1) & (ss >= first) & (ss <= last)
    nb = jnp.where(nb, 1.0, 0.0).astype(BF16)

    def shifted(i):
        lanes = slice(i * D_MODEL, (i + 1) * D_MODEL)
        x = rkv_ref[:, lanes]
        both = jnp.dot(nb, jnp.concatenate([prev_ref[:, lanes], x, next_ref[:, lanes]], axis=0),
                       preferred_element_type=F32)
        mu = mu_ref[i:i + 1, :]
        return (1.0 - mu) * x.astype(F32) + (0.5 * mu) * both

    r_s = shifted(0)
    yield
    k_s = shifted(1)
    yield
    v_s = shifted(2)
    store("v", v_s)
    yield

    if rotary:
        lane = lax.broadcasted_iota(jnp.int32, (1, D_MODEL), 1)
        low = (lane % 32) < 16
        cos_t = jnp.tile(rope_ref[:, :LANES], (1, D_MODEL // LANES))
        sin_t = jnp.tile(rope_ref[:, LANES:], (1, D_MODEL // LANES))

        def rope(x):
            partner = jnp.where(low, pltpu.roll(x, D_MODEL - 16, axis=1), pltpu.roll(x, 16, axis=1))
            return x * cos_t + partner * sin_t

        r_s = rope(r_s)
        yield
        k_s = rope(k_s)
        yield

    lo = lo_ref[...]
    lw = w0_ref[d:d + 1, :] + jnp.dot(jnp.tanh(lo[:, :2 * LORA].astype(F32)).astype(BF16), wup_ref[d],
                                     preferred_element_type=F32)
    ld = (-np.exp(-0.5)) * jax.nn.sigmoid(lw)
    yield
    a = jax.nn.sigmoid(a0_ref[d:d + 1, :] + jnp.dot(lo[:, 2 * LORA:], aup_ref[d], preferred_element_type=F32))
    yield

    kk = k_s * kk_ref[...]
    kk = kk * jnp.minimum(lax.rsqrt(_head_sum(kk * kk, ones_bd)), 1e12)
    yield
    k_dir = k_s * (1.0 + (a - 1.0) * ka_ref[...])
    b_vec = kk * a
    bonus_ref[...] = (_head_sum(r_s * rk_ref[...] * k_dir, ones_bd) * v_s).astype(bonus_ref.dtype)
    yield

    tt = lax.broadcasted_iota(jnp.int32, (C, C), 0)
    ss = lax.broadcasted_iota(jnp.int32, (C, C), 1)
    tri = jnp.where((ss <= tt) if d == 0 else (ss >= tt), 1.0, 0.0).astype(BF16)
    ld_hi = ld.astype(BF16)
    ld_lo = (ld - ld_hi.astype(F32)).astype(BF16)
    cl = (jnp.dot(tri, ld_hi, preferred_element_type=F32) + jnp.dot(tri, ld_lo, preferred_element_type=F32))
    cl_tot = cl[C - 1:C, :] if d == 0 else cl[0:1, :]
    yield
    e_neg = jnp.exp(-cl)
    p_c = jnp.exp(cl_tot)
    pc_ref[d] = jnp.broadcast_to(p_c, pc_ref.shape[1:])
    b_t = b_vec * e_neg
    store("b", b_t)
    store("bh", b_t * p_c)
    yield
    k_t = k_dir * e_neg
    store("k", k_t)
    store("kh", k_t * p_c)
    yield
    store("a", -kk * jnp.exp(cl - ld))
    store("r", r_s * jnp.exp(cl))


OPERANDS = ("a", "r", "b", "k", "bh", "kh", "v")


def _interleave(*gens):
    live = list(gens)
    while live:
        for gen in list(live):
            try:
                next(gen)
            except StopIteration:
                live.remove(gen)


def _rwkv_chunk_matmuls(ops_ref, pc_ref, h_scr, yf_ref, yb_ref, same_head, emit_y):
    C = CHUNK
    n_ops = len(OPERANDS)
    n_g = 2 * N_GROUPS
    dirs = [g // N_GROUPS for g in range(n_g)]

    def lanes_of(g):
        return slice((g % N_GROUPS) * GROUP, (g % N_GROUPS + 1) * GROUP)

    def grp(name):
        i = OPERANDS.index(name)
        return [ops_ref[dirs[g] * n_ops + i, :, lanes_of(g)] for g in range(n_g)]

    t_i = lax.broadcasted_iota(jnp.int32, (C, GROUP), 0)
    s_i = lax.broadcasted_iota(jnp.int32, (C, GROUP), 1) % C
    before = [s_i < t_i, s_i > t_i]
    upto = [s_i <= t_i, s_i >= t_i]
    eye = jnp.where(t_i == s_i, 1.0, 0.0).astype(F32)
    nt = (((1,), (1,)), ((), ()))
    tn = (((0,), (0,)), ((), ()))

    def bd(x):
        return jnp.where(same_head, jnp.tile(x, (HEADS_PER_GROUP, 1)), jnp.zeros((), x.dtype))

    def hdot(a_side, x):
        return jnp.dot(a_side, bd(x), preferred_element_type=F32)

    def masked(m, keep):
        return jnp.where(keep, m, 0.0).astype(BF16)

    at, rt = grp("a"), grp("r")
    lhs = [jnp.concatenate([at[g], rt[g]], axis=0) for g in range(n_g)] if emit_y else at
    bt = grp("b")
    m_b = [lax.dot_general(lhs[g], bd(bt[g]), nt, preferred_element_type=F32) for g in range(n_g)]
    yield
    kt = grp("k")
    m_k = [lax.dot_general(lhs[g], bd(kt[g]), nt, preferred_element_type=F32) for g in range(n_g)]
    yield
    a_ab = [masked(m_b[g][:C], before[dirs[g]]) for g in range(n_g)]
    a_ak = [masked(m_k[g][:C], before[dirs[g]]) for g in range(n_g)]
    vv = grp("v")
    if emit_y:
        a_rb = [masked(m_b[g][C:], upto[dirs[g]]) for g in range(n_g)]
        a_rk = [masked(m_k[g][C:], upto[dirs[g]]) for g in range(n_g)]
        w_v = [hdot(jnp.concatenate([a_ak[g], a_rk[g]], axis=0), vv[g]) for g in range(n_g)]
    else:
        w_v = [hdot(a_ak[g], vv[g]) for g in range(n_g)]
    yield

    npow = [hdot(a_ab[g], a_ab[g]) for g in range(n_g)]
    inv = [eye + a_ab[g].astype(F32) for g in range(n_g)]
    yield
    n_factors = C.bit_length() - 1
    for j in range(1, n_factors):
        last = j == n_factors - 1
        pw = [npow[g].astype(BF16) for g in range(n_g)]
        lhs_j = [inv[g].astype(BF16) if last else jnp.concatenate([inv[g].astype(BF16), pw[g]], axis=0)
                 for g in range(n_g)]
        prod = [hdot(lhs_j[g], pw[g]) for g in range(n_g)]
        inv = [inv[g] + prod[g][:C] for g in range(n_g)]
        if not last:
            npow = [prod[g][C:] for g in range(n_g)]
        yield
    inv = [inv[g].astype(BF16) for g in range(n_g)]

    a_p = [hdot(inv[g], at[g]).astype(BF16) for g in range(n_g)]
    yield
    u0 = [hdot(inv[g], w_v[g][:C].astype(BF16)) for g in range(n_g)]
    yield

    h_old = [h_scr[:, g * GROUP:(g + 1) * GROUP] for g in range(n_g)]
    decay_rows = [(eye * pc_ref[dirs[g], 0:1, lanes_of(g)]).astype(BF16) for g in range(n_g)]
    stack = [jnp.concatenate([a_p[g]] + ([rt[g]] if emit_y else []) + [decay_rows[g]], axis=0)
             for g in range(n_g)]
    s1 = [hdot(stack[g], h_old[g].astype(BF16)) for g in range(n_g)]
    u = [(s1[g][:C] + u0[g]).astype(BF16) for g in range(n_g)]
    yield
    if emit_y:
        y_u = [hdot(a_rb[g], u[g]) for g in range(n_g)]
        for g in range(n_g):
            y_ref = yf_ref if dirs[g] == 0 else yb_ref
            y_ref[:, lanes_of(g)] = (s1[g][C:2 * C] + y_u[g] + w_v[g][C:]).astype(y_ref.dtype)
    else:
        yf_ref[...] = jnp.zeros(yf_ref.shape, yf_ref.dtype)
        yb_ref[...] = jnp.zeros(yb_ref.shape, yb_ref.dtype)
    yield

    bh, kh = grp("bh"), grp("kh")
    full = [lax.dot_general(jnp.concatenate([bh[g], kh[g]], axis=0),
                            jnp.concatenate([u[g], vv[g]], axis=0), tn, preferred_element_type=F32)
            for g in range(n_g)]
    lane_head = lax.broadcasted_iota(jnp.int32, (HEAD_DIM, GROUP), 1) // HEAD_DIM
    for g in range(n_g):
        upd = full[g][(HEADS_PER_GROUP - 1) * HEAD_DIM:]
        for j in range(HEADS_PER_GROUP - 2, -1, -1):
            upd = jnp.where(lane_head == j, full[g][j * HEAD_DIM:(j + 1) * HEAD_DIM], upd)
        h_scr[:, g * GROUP:(g + 1) * GROUP] = s1[g][-C:] + upd


def _rwkv_kernel(*refs, n_chunks, rotary, emit_y):
    per_dir = 5
    in_f, in_b = refs[:per_dir], refs[per_dir:2 * per_dir]
    prm = refs[2 * per_dir:2 * per_dir + 8]
    s0_ref = refs[2 * per_dir + 8]
    yf_ref, yb_ref, bonf_ref, bonb_ref, sout_ref, h_scr, ops_scr, pc_scr = refs[2 * per_dir + 9:]
    step = pl.program_id(1)

    @pl.when(step == 0)
    def _():
        h_scr[:, :D_MODEL] = s0_ref[0]
        h_scr[:, D_MODEL:] = s0_ref[1]
        ops_scr[0] = jnp.zeros(ops_scr.shape[1:], ops_scr.dtype)
        pc_scr[0] = jnp.ones(pc_scr.shape[1:], pc_scr.dtype)

    nxt = jnp.minimum(step, n_chunks - 1)

    def body(read, write):
        same_head = _block_diag_masks()
        ones_bd = jnp.where(same_head, 1.0, 0.0).astype(BF16)
        _interleave(
            _rwkv_chunk_matmuls(ops_scr.at[read], pc_scr.at[read], h_scr, yf_ref, yb_ref, same_head, emit_y),
            _rwkv_prepare(0, nxt, n_chunks, in_f + (bonf_ref,), prm, ones_bd, rotary,
                          ops_scr.at[write], pc_scr.at[write]),
            _rwkv_prepare(1, n_chunks - 1 - nxt, n_chunks, in_b + (bonb_ref,), prm, ones_bd, rotary,
                          ops_scr.at[write], pc_scr.at[write]))

    for parity in range(2):
        pl.when(step % 2 == parity)(functools.partial(body, parity, 1 - parity))

    @pl.when(step == n_chunks)
    def _():
        sout_ref[0] = h_scr[:, :D_MODEL]
        sout_ref[1] = h_scr[:, D_MODEL:]


def _rope_tables(T):
    half = HEAD_DIM // 2
    n_freq = half // 2
    t = np.arange(T)[:, None]
    lane = np.arange(LANES)[None, :]
    inv_freq = ROPE_THETA ** (-(np.arange(n_freq, dtype=np.float32)) / n_freq)
    pos = np.where((lane % HEAD_DIM) < half, t // GRID_W, t % GRID_W).astype(np.float32)
    ang = (pos * inv_freq[lane % n_freq].astype(np.float32)).astype(np.float32)
    sign = np.where((lane % half) < n_freq, -1.0, 1.0)
    return jnp.asarray(np.cos(ang), F32), jnp.asarray(np.sin(ang) * sign, F32)


def _rwkv_scan(u, state0, params, rotary, emit_y):
    B, T, _ = u.shape
    C = CHUNK
    n_chunks = T // C
    halo = 16
    per = C // halo

    def mirror(d, c):
        return c if d == 0 else n_chunks - 1 - c

    def chunk_of(d, s):
        return mirror(d, jnp.minimum(s, n_chunks - 1))

    def done_of(d, s):
        return mirror(d, jnp.maximum(s - 1, 0))

    def specs(d):
        rkv = COL_RKV // (3 * D_MODEL)
        main = pl.BlockSpec((None, C, 3 * D_MODEL), lambda b, c: (b, chunk_of(d, c), rkv))
        prev = pl.BlockSpec((None, halo, 3 * D_MODEL),
                            lambda b, c: (b, jnp.maximum(chunk_of(d, c) * per - 1, 0), rkv))
        nxt = pl.BlockSpec((None, halo, 3 * D_MODEL),
                           lambda b, c: (b, jnp.minimum((chunk_of(d, c) + 1) * per, T // halo - 1), rkv))
        lora = pl.BlockSpec((None, C, 4 * LORA), lambda b, c: (b, chunk_of(d, c), COL_LORA // (4 * LORA)))
        tab = pl.BlockSpec((C, 2 * LANES), lambda b, c: (chunk_of(d, c), 0))
        return [main, prev, nxt, lora, tab]

    def whole(shape):
        return pl.BlockSpec(shape, lambda b, c: (0,) * len(shape))

    rope_tab = jnp.concatenate(_rope_tables(T), axis=1)
    state = pl.BlockSpec((None, 2, HEAD_DIM, D_MODEL), lambda b, c: (b, 0, 0, 0))
    param_specs = [whole((3, D_MODEL)), whole((2, D_MODEL)), whole((2, 2 * LORA, D_MODEL)),
                   whole((2, D_MODEL)), whole((2, 2 * LORA, D_MODEL)),
                   whole((1, D_MODEL)), whole((1, D_MODEL)), whole((1, D_MODEL))]
    seq = jax.ShapeDtypeStruct((B, T, D_MODEL), BF16)
    y_f = pl.BlockSpec((None, C, D_MODEL), lambda b, c: (b, done_of(0, c), 0))
    y_b = pl.BlockSpec((None, C, D_MODEL), lambda b, c: (b, done_of(1, c), 0))
    bon_f = pl.BlockSpec((None, C, D_MODEL), lambda b, c: (b, chunk_of(0, c), 0))
    bon_b = pl.BlockSpec((None, C, D_MODEL), lambda b, c: (b, chunk_of(1, c), 0))
    seq_in = (u,) * 4 + (rope_tab,)
    return pl.pallas_call(
        functools.partial(_rwkv_kernel, n_chunks=n_chunks, rotary=rotary, emit_y=emit_y),
        out_shape=(seq, seq, seq, seq, jax.ShapeDtypeStruct((B, 2, HEAD_DIM, D_MODEL), F32)),
        grid=(B, n_chunks + 1),
        in_specs=specs(0) + specs(1) + param_specs + [state],
        out_specs=(y_f, y_b, bon_f, bon_b, state),
        scratch_shapes=[pltpu.VMEM((HEAD_DIM, 2 * D_MODEL), F32),
                        pltpu.VMEM((2, 2 * len(OPERANDS), C, D_MODEL), BF16),
                        pltpu.VMEM((2, 2, 8, D_MODEL), F32)],
        compiler_params=_cparams(("arbitrary", "arbitrary")),
        name="rwkv7_chunk_scan_rot" if rotary else "rwkv7_chunk_scan_ctx",
    )(*seq_in, *seq_in, *params, state0)


def _final_kernel(x_ref, ya_ref, yf_ref, yb_ref, bf_ref, bb_ref, z_ref, ga_ref, gb_ref, gate_ref,
                  gng_ref, gnb_ref, wa_ref, wb_ref, wo_ref, lng_ref, lnb_ref, o_ref):
    same_head = _block_diag_masks()
    ones_bd = jnp.where(same_head, 1.0, 0.0).astype(BF16)
    y = yf_ref[...].astype(F32) + yb_ref[...].astype(F32)
    mu = _head_sum(y, ones_bd) * (1.0 / HEAD_DIM)
    yc = y - mu
    var = _head_sum(yc * yc, ones_bd) * (1.0 / HEAD_DIM)
    yn = yc * lax.rsqrt(var + GN_EPS) * gng_ref[...] + gnb_ref[...]
    yn = yn + bf_ref[...].astype(F32) + bb_ref[...].astype(F32)
    y_b = (yn * _silu(z_ref[...].astype(F32))).astype(BF16)
    p_a = jnp.dot(ya_ref[...], wa_ref[...], preferred_element_type=F32)
    p_b = jnp.dot(y_b, wb_ref[...], preferred_element_type=F32)
    merged = (jax.nn.sigmoid(ga_ref[...].astype(F32)) * p_a
              + jax.nn.sigmoid(gb_ref[...].astype(F32)) * p_b)
    out = jnp.dot(merged.astype(BF16), wo_ref[...], preferred_element_type=F32)
    t = ALPHA * x_ref[...] + gate_ref[...] * out
    m = jnp.mean(t, axis=-1, keepdims=True)
    tc = t - m
    v = jnp.mean(tc * tc, axis=-1, keepdims=True)
    o_ref[...] = tc * lax.rsqrt(v + LN_EPS) * lng_ref[...] + lnb_ref[...]


def _final_stage(x, y_a, y_f, y_b, bonus_f, bonus_b, u, mod3, gn_g, gn_b, wa, wb, wo, ln_g, ln_b, tm):
    B, T, _ = x.shape

    def tok():
        return pl.BlockSpec((None, tm, D_MODEL), lambda b, i: (b, i, 0))

    def ucol(blk):
        return pl.BlockSpec((None, tm, D_MODEL), lambda b, i: (b, i, blk))

    def vec():
        return pl.BlockSpec((1, D_MODEL), lambda b, i: (0, 0))

    def mat():
        return pl.BlockSpec((D_MODEL, D_MODEL), lambda b, i: (0, 0))

    return pl.pallas_call(
        _final_kernel,
        out_shape=jax.ShapeDtypeStruct((B, T, D_MODEL), F32),
        grid=(B, T // tm),
        in_specs=[tok(), tok(), tok(), tok(), tok(), tok(), ucol(7), ucol(8), ucol(9),
                  pl.BlockSpec((None, 1, D_MODEL), lambda b, i: (b, 0, 2)),
                  vec(), vec(), mat(), mat(), mat(), vec(), vec()],
        out_specs=tok(),
        compiler_params=_cparams(("arbitrary", "arbitrary")),
        name="readout_merge_out_proj",
    )(x, y_a, y_f, y_b, bonus_f, bonus_b, u, u, u, mod3, gn_g, gn_b, wa, wb, wo, ln_g, ln_b)


def _layer(x, c, ctx, c_ctx, w_ada, b_ada, w_in, na_rpb, rw_mu, rw_w0, rw_w_up, rw_a0, rw_a_up,
           rw_k_k, rw_k_a, rw_r_k, rw_gn_g, rw_gn_b, w_branch_a, w_branch_b, w_out, ln_g, ln_b):
    B, T, _ = x.shape
    L = ctx.shape[1]
    rows = T // GRID_W

    n_cond = -(-(B + 1) // 8) * 8
    cond = jnp.zeros((n_cond, D_MODEL), F32).at[:B].set(c).at[B].set(c_ctx)
    mod3 = _modulation(cond, w_ada, b_ada).reshape(n_cond, 1, 3 * D_MODEL)

    w_bf = jnp.concatenate(
        [w_in[:, 4 * D_MODEL:7 * D_MODEL], w_in[:, :4 * D_MODEL], w_in[:, 7 * D_MODEL:N_MAIN],
         w_in[:, N_MAIN + 4 * LORA:], w_in[:, N_MAIN:N_MAIN + 4 * LORA],
         jnp.zeros((D_MODEL, N_PAD - N_PROJ), w_in.dtype)], axis=1).astype(BF16)
    u = _in_projection(x, mod3, w_bf, lambda b: b, tm=min(T, 1024), tn=1536)
    u_ctx = _in_projection(ctx, mod3, w_bf, lambda b: B, tm=L, tn=1536)

    y_a = _neighbourhood_attention(u, u_ctx, _na_bias_table(na_rpb, rows))

    def lora_pad(w_up):
        out = jnp.zeros((2, 2 * LORA, D_MODEL), F32)
        for d in range(2):
            out = out.at[d, d * LORA:(d + 1) * LORA].set(w_up[d])
        return out.astype(BF16)

    params = (rw_mu, rw_w0, lora_pad(rw_w_up), rw_a0, lora_pad(rw_a_up), rw_k_k.reshape(1, D_MODEL),
              rw_k_a.reshape(1, D_MODEL), rw_r_k.reshape(1, D_MODEL))
    state0 = jnp.zeros((B, 2, HEAD_DIM, D_MODEL), F32)
    state_c = _rwkv_scan(u_ctx, state0, params, rotary=False, emit_y=False)[-1]
    y_f, y_b, bonus_f, bonus_b, _ = _rwkv_scan(u, state_c, params, rotary=True, emit_y=True)

    return _final_stage(x, y_a, y_f, y_b, bonus_f, bonus_b, u, mod3, rw_gn_g.reshape(1, D_MODEL),
                        rw_gn_b.reshape(1, D_MODEL), w_branch_a.astype(BF16), w_branch_b.astype(BF16),
                        w_out.astype(BF16), ln_g.reshape(1, D_MODEL), ln_b.reshape(1, D_MODEL),
                        tm=min(T, 512))


def kernel(x, c, ctx, c_ctx, w_ada, b_ada, w_in, na_rpb, rw_mu, rw_w0, rw_w_up, rw_a0, rw_a_up, rw_k_k, rw_k_a, rw_r_k, rw_gn_g, rw_gn_b, w_branch_a, w_branch_b, w_out, ln_g, ln_b):
    assert w_ada.shape[0] == DEPTH
    return _layer(x, c, ctx, c_ctx, w_ada[0], b_ada[0], w_in[0], na_rpb[0], rw_mu[0], rw_w0[0],
                  rw_w_up[0], rw_a0[0], rw_a_up[0], rw_k_k[0], rw_k_a[0], rw_r_k[0], rw_gn_g[0],
                  rw_gn_b[0], w_branch_a[0], w_branch_b[0], w_out[0], ln_g[0], ln_b[0])
```

```python
import functools

import numpy as np
import jax
import jax.numpy as jnp
from jax import lax
from jax.experimental import pallas as pl
from jax.experimental.pallas import tpu as pltpu

F32 = jnp.float32
BF16 = jnp.bfloat16
HIGHEST = lax.Precision.HIGHEST

D_MODEL = 1024
GRID_W = 64
HEADS = 16
HEAD_DIM = 64
NA_MAX_ROWS = 8
NA_COLS = 16
LORA = 64
DEPTH = 1
ROPE_THETA = 10000.0
LN_EPS = 1e-5
GN_EPS = 64e-5
ALPHA = (2 * DEPTH) ** 0.25

LANES = 128
GROUP = 256
HEADS_PER_GROUP = GROUP // HEAD_DIM
N_GROUPS = D_MODEL // GROUP
CHUNK = 64

COL_RKV = 0
COL_NA = 3 * D_MODEL
COL_ZB = 7 * D_MODEL
N_MAIN = 8 * D_MODEL
COL_MG = N_MAIN
COL_LORA = N_MAIN + 2 * D_MODEL
N_PROJ = COL_LORA + 4 * LORA
N_PAD = 21 * 512
VMEM_LIMIT = 56 * 1024 * 1024


def _cparams(sem):
    return pltpu.CompilerParams(dimension_semantics=sem, vmem_limit_bytes=VMEM_LIMIT)


def _silu(x):
    return x * jax.nn.sigmoid(x)


def _mod_kernel(c_ref, w_ref, b_ref, o_ref):
    s = _silu(c_ref[...])
    o_ref[...] = jnp.dot(s, w_ref[...], preferred_element_type=F32, precision=HIGHEST) + b_ref[...]


def _modulation(cond, w_ada, b_ada):
    rows = cond.shape[0]
    return pl.pallas_call(
        _mod_kernel,
        out_shape=jax.ShapeDtypeStruct((rows, 3 * D_MODEL), F32),
        grid=(3,),
        in_specs=[pl.BlockSpec((rows, D_MODEL), lambda j: (0, 0)),
                  pl.BlockSpec((D_MODEL, D_MODEL), lambda j: (0, j)),
                  pl.BlockSpec((1, D_MODEL), lambda j: (0, j))],
        out_specs=pl.BlockSpec((rows, D_MODEL), lambda j: (0, j)),
        compiler_params=_cparams(("arbitrary",)),
        name="ada_modulation",
    )(cond, w_ada, b_ada.reshape(1, 3 * D_MODEL))


def _inproj_kernel(x_ref, sh_ref, sc_ref, w_ref, o_ref, h_scr):
    @pl.when(pl.program_id(2) == 0)
    def _():
        x = x_ref[...]
        mu = jnp.mean(x, axis=-1, keepdims=True)
        xc = x - mu
        var = jnp.mean(xc * xc, axis=-1, keepdims=True)
        y = xc * lax.rsqrt(var + LN_EPS)
        h_scr[...] = (y * (1.0 + sc_ref[...]) + sh_ref[...]).astype(BF16)

    o_ref[...] = jnp.dot(h_scr[...], w_ref[...], preferred_element_type=F32).astype(BF16)


def _in_projection(x, mod3, w_bf, row_of_batch, tm, tn):
    B, T, _ = x.shape
    return pl.pallas_call(
        _inproj_kernel,
        out_shape=jax.ShapeDtypeStruct((B, T, N_PAD), BF16),
        grid=(B, T // tm, N_PAD // tn),
        in_specs=[pl.BlockSpec((None, tm, D_MODEL), lambda b, i, j: (b, i, 0)),
                  pl.BlockSpec((None, 1, D_MODEL), lambda b, i, j: (row_of_batch(b), 0, 0)),
                  pl.BlockSpec((None, 1, D_MODEL), lambda b, i, j: (row_of_batch(b), 0, 1)),
                  pl.BlockSpec((D_MODEL, tn), lambda b, i, j: (0, j))],
        out_specs=pl.BlockSpec((None, tm, tn), lambda b, i, j: (b, i, j)),
        scratch_shapes=[pltpu.VMEM((tm, D_MODEL), BF16)],
        compiler_params=_cparams(("arbitrary", "arbitrary", "arbitrary")),
        name="ln_mod_in_proj",
    )(x, mod3, mod3, w_bf)


NA_HEADS_PER_STEP = GROUP // HEAD_DIM
NA_ROWS_PER_ITER = 8


def _na_kernel(q_ref, k_ref, v_ref, z_ref, kc_ref, vc_ref, bias_ref, o_ref, *, rows):
    kh = min(NA_MAX_ROWS, rows)
    band = kh * GRID_W
    nh = NA_HEADS_PER_STEP
    lane = lax.broadcasted_iota(jnp.int32, (1, GROUP), 1)
    head_lanes = [(lane // HEAD_DIM) == h for h in range(nh)]
    kc = kc_ref[...]
    vc = vc_ref[...]
    nt = (((1,), (1,)), ((), ()))

    n_rows = NA_ROWS_PER_ITER if rows % NA_ROWS_PER_ITER == 0 else 1
    stack = nh * GRID_W

    def rows_body(it, carry):
        qs, s, vb, q0s = [], [], [], []
        for r in range(n_rows):
            i = it * n_rows + r
            rs = jnp.clip(i - kh // 2, 0, rows - kh)
            q0 = pl.multiple_of(i * GRID_W, GRID_W)
            k0 = pl.multiple_of(rs * GRID_W, GRID_W)
            q = q_ref[pl.ds(q0, GRID_W), :] * jnp.asarray(HEAD_DIM ** -0.5, BF16)
            q_st = jnp.concatenate([jnp.where(head_lanes[h], q, jnp.zeros_like(q)) for h in range(nh)], axis=0)
            s.append(lax.dot_general(q_st, k_ref[pl.ds(k0, band), :], nt, preferred_element_type=F32)
                     + bias_ref[i - rs])
            vb.append(v_ref[pl.ds(k0, band), :])
            qs.append(q_st)
            q0s.append(q0)
        sc = lax.dot_general(jnp.concatenate(qs, axis=0), kc, nt, preferred_element_type=F32)
        p, pcs, denom = [], [], []
        for r in range(n_rows):
            sc_r = sc[r * stack:(r + 1) * stack]
            m = jnp.maximum(jnp.max(s[r], axis=-1, keepdims=True), jnp.max(sc_r, axis=-1, keepdims=True))
            p_r = jnp.exp(s[r] - m)
            pc_r = jnp.exp(sc_r - m)
            denom.append(jnp.sum(p_r, axis=-1, keepdims=True) + jnp.sum(pc_r, axis=-1, keepdims=True))
            p.append(p_r.astype(BF16))
            pcs.append(pc_r.astype(BF16))
        o_ctx = jnp.dot(jnp.concatenate(pcs, axis=0), vc, preferred_element_type=F32)
        for r in range(n_rows):
            o_all = (jnp.dot(p[r], vb[r], preferred_element_type=F32)
                     + o_ctx[r * stack:(r + 1) * stack]) / denom[r]
            o = o_all[(nh - 1) * GRID_W:]
            for h in range(nh - 2, -1, -1):
                o = jnp.where(head_lanes[h], o_all[h * GRID_W:(h + 1) * GRID_W], o)
            z = z_ref[pl.ds(q0s[r], GRID_W), :].astype(F32)
            o_ref[pl.ds(q0s[r], GRID_W), :] = (o * _silu(z)).astype(BF16)
        return carry

    lax.fori_loop(0, rows // n_rows, rows_body, 0)


def _na_bias_table(rpb, rows):
    kh = min(NA_MAX_ROWS, rows)
    nh = NA_HEADS_PER_STEP
    n_dj = 2 * NA_COLS - 1
    cols = np.arange(GRID_W)
    cstart = np.clip(cols - NA_COLS // 2, 0, GRID_W - NA_COLS)
    col_mask = (cols[None, :] >= cstart[:, None]) & (cols[None, :] < cstart[:, None] + NA_COLS)
    dj = np.clip(cols[None, :] - cols[:, None] + NA_COLS - 1, 0, n_dj - 1)
    onehot = (np.arange(n_dj)[:, None] == dj.reshape(1, -1)).astype(np.float32)
    exp = jnp.dot(rpb.reshape(-1, n_dj), jnp.asarray(onehot), precision=HIGHEST)
    exp = exp.reshape(HEADS, 2 * NA_MAX_ROWS - 1, GRID_W, GRID_W)
    exp = jnp.where(col_mask[None, None], exp, -jnp.inf)
    lo = NA_MAX_ROWS - 1
    tab = jnp.stack([exp[:, lo - o: lo - o + kh] for o in range(kh)], axis=1)
    tab = tab.reshape(HEADS // nh, nh, kh, kh, GRID_W, GRID_W)
    return tab.transpose(0, 2, 1, 4, 3, 5).reshape(HEADS // nh, kh, nh * GRID_W, kh * GRID_W)


def _neighbourhood_attention(u, u_ctx, bias_tab):
    B, T, _ = u.shape
    L = u_ctx.shape[1]
    rows = T // GRID_W
    kh = min(NA_MAX_ROWS, rows)
    blk = D_MODEL // GROUP
    first = COL_NA // GROUP

    def col(part):
        return pl.BlockSpec((None, T, GROUP), lambda b, g: (b, 0, first + part * blk + g))

    def col_ctx(part):
        return pl.BlockSpec((None, L, GROUP), lambda b, g: (b, 0, first + part * blk + g))

    return pl.pallas_call(
        functools.partial(_na_kernel, rows=rows),
        out_shape=jax.ShapeDtypeStruct((B, T, D_MODEL), BF16),
        grid=(B, blk),
        in_specs=[col(0), col(1), col(2), col(3), col_ctx(1), col_ctx(2),
                  pl.BlockSpec((None, kh, NA_HEADS_PER_STEP * GRID_W, kh * GRID_W), lambda b, g: (g, 0, 0, 0))],
        out_specs=pl.BlockSpec((None, T, GROUP), lambda b, g: (b, 0, g)),
        compiler_params=_cparams(("arbitrary", "arbitrary")),
        name="neighbourhood_attention",
    )(u, u, u, u, u_ctx, u_ctx, bias_tab)


def _block_diag_masks():
    r = lax.broadcasted_iota(jnp.int32, (GROUP, GROUP), 0)
    c = lax.broadcasted_iota(jnp.int32, (GROUP, GROUP), 1)
    return (r // HEAD_DIM) == (c // HEAD_DIM)


def _head_sum(x, ones_bd):
    rows = x.shape[0]
    xb = x.astype(BF16)
    stacked = jnp.concatenate([xb[:, g * GROUP:(g + 1) * GROUP] for g in range(N_GROUPS)], axis=0)
    sums = jnp.dot(stacked, ones_bd, preferred_element_type=F32)
    return jnp.concatenate([sums[g * rows:(g + 1) * rows] for g in range(N_GROUPS)], axis=1)


def _rwkv_prepare(d, cidx, n_chunks, refs, prm, ones_bd, rotary, ops_ref, pc_ref):
    rkv_ref, prev_ref, next_ref, lo_ref, rope_ref, bonus_ref = refs
    mu_ref, w0_ref, wup_ref, a0_ref, aup_ref, kk_ref, ka_ref, rk_ref = prm
    C = CHUNK
    halo = prev_ref.shape[0]

    def store(name, value):
        ops_ref[d * len(OPERANDS) + OPERANDS.index(name)] = value.astype(BF16)

    tt = lax.broadcasted_iota(jnp.int32, (C, C + 2 * halo), 0)
    ss = lax.broadcasted_iota(jnp.int32, (C, C + 2 * halo), 1) - halo
    first = jnp.where(cidx > 0, -1, 0)
    last = jnp.where(cidx < n_chunks - 1, C, C - 1)
    nb = (jnp.abs(ss - tt) == 1) & (ss >= first) & (ss <= last)
    nb = jnp.where(nb, 1.0, 0.0).astype(BF16)

    def shifted(i):
        lanes = slice(i * D_MODEL, (i + 1) * D_MODEL)
        x = rkv_ref[:, lanes]
        both = jnp.dot(nb, jnp.concatenate([prev_ref[:, lanes], x, next_ref[:, lanes]], axis=0),
                       preferred_element_type=F32)
        mu = mu_ref[i:i + 1, :]
        return (1.0 - mu) * x.astype(F32) + (0.5 * mu) * both

    r_s = shifted(0)
    yield
    k_s = shifted(1)
    yield
    v_s = shifted(2)
    store("v", v_s)
    yield

    if rotary:
        lane = lax.broadcasted_iota(jnp.int32, (1, D_MODEL), 1)
        low = (lane % 32) < 16
        cos_t = jnp.tile(rope_ref[:, :LANES], (1, D_MODEL // LANES))
        sin_t = jnp.tile(rope_ref[:, LANES:], (1, D_MODEL // LANES))

        def rope(x):
            partner = jnp.where(low, pltpu.roll(x, D_MODEL - 16, axis=1), pltpu.roll(x, 16, axis=1))
            return x * cos_t + partner * sin_t

        r_s = rope(r_s)
        yield
        k_s = rope(k_s)
        yield

    lo = lo_ref[...]
    lw = w0_ref[d:d + 1, :] + jnp.dot(jnp.tanh(lo[:, :2 * LORA].astype(F32)).astype(BF16), wup_ref[d],
                                     preferred_element_type=F32)
    ld = (-np.exp(-0.5)) * jax.nn.sigmoid(lw)
    yield
    a = jax.nn.sigmoid(a0_ref[d:d + 1, :] + jnp.dot(lo[:, 2 * LORA:], aup_ref[d], preferred_element_type=F32))
    yield

    kk = k_s * kk_ref[...]
    kk = kk * jnp.minimum(lax.rsqrt(_head_sum(kk * kk, ones_bd)), 1e12)
    yield
    k_dir = k_s * (1.0 + (a - 1.0) * ka_ref[...])
    b_vec = kk * a
    bonus_ref[...] = (_head_sum(r_s * rk_ref[...] * k_dir, ones_bd) * v_s).astype(bonus_ref.dtype)
    yield

    tt = lax.broadcasted_iota(jnp.int32, (C, C), 0)
    ss = lax.broadcasted_iota(jnp.int32, (C, C), 1)
    tri = jnp.where((ss <= tt) if d == 0 else (ss >= tt), 1.0, 0.0).astype(BF16)
    ld_hi = ld.astype(BF16)
    ld_lo = (ld - ld_hi.astype(F32)).astype(BF16)
    cl = (jnp.dot(tri, ld_hi, preferred_element_type=F32) + jnp.dot(tri, ld_lo, preferred_element_type=F32))
    cl_tot = cl[C - 1:C, :] if d == 0 else cl[0:1, :]
    yield
    e_neg = jnp.exp(-cl)
    p_c = jnp.exp(cl_tot)
    pc_ref[d] = jnp.broadcast_to(p_c, pc_ref.shape[1:])
    b_t = b_vec * e_neg
    store("b", b_t)
    store("bh", b_t * p_c)
    yield
    k_t = k_dir * e_neg
    store("k", k_t)
    store("kh", k_t * p_c)
    yield
    store("a", -kk * jnp.exp(cl - ld))
    store("r", r_s * jnp.exp(cl))


OPERANDS = ("a", "r", "b", "k", "bh", "kh", "v")


def _interleave(*gens):
    live = list(gens)
    while live:
        for gen in list(live):
            try:
                next(gen)
            except StopIteration:
                live.remove(gen)


def _rwkv_chunk_matmuls(ops_ref, pc_ref, h_scr, yf_ref, yb_ref, same_head, emit_y):
    C = CHUNK
    n_ops = len(OPERANDS)
    n_g = 2 * N_GROUPS
    dirs = [g // N_GROUPS for g in range(n_g)]

    def lanes_of(g):
        return slice((g % N_GROUPS) * GROUP, (g % N_GROUPS + 1) * GROUP)

    def grp(name):
        i = OPERANDS.index(name)
        return [ops_ref[dirs[g] * n_ops + i, :, lanes_of(g)] for g in range(n_g)]

    t_i = lax.broadcasted_iota(jnp.int32, (C, GROUP), 0)
    s_i = lax.broadcasted_iota(jnp.int32, (C, GROUP), 1) % C
    before = [s_i < t_i, s_i > t_i]
    upto = [s_i <= t_i, s_i >= t_i]
    eye = jnp.where(t_i == s_i, 1.0, 0.0).astype(F32)
    nt = (((1,), (1,)), ((), ()))
    tn = (((0,), (0,)), ((), ()))

    def bd(x):
        return jnp.where(same_head, jnp.tile(x, (HEADS_PER_GROUP, 1)), jnp.zeros((), x.dtype))

    def hdot(a_side, x):
        return jnp.dot(a_side, bd(x), preferred_element_type=F32)

    def masked(m, keep):
        return jnp.where(keep, m, 0.0).astype(BF16)

    at, rt = grp("a"), grp("r")
    lhs = [jnp.concatenate([at[g], rt[g]], axis=0) for g in range(n_g)] if emit_y else at
    bt = grp("b")
    m_b = [lax.dot_general(lhs[g], bd(bt[g]), nt, preferred_element_type=F32) for g in range(n_g)]
    yield
    kt = grp("k")
    m_k = [lax.dot_general(lhs[g], bd(kt[g]), nt, preferred_element_type=F32) for g in range(n_g)]
    yield
    a_ab = [masked(m_b[g][:C], before[dirs[g]]) for g in range(n_g)]
    a_ak = [masked(m_k[g][:C], before[dirs[g]]) for g in range(n_g)]
    vv = grp("v")
    if emit_y:
        a_rb = [masked(m_b[g][C:], upto[dirs[g]]) for g in range(n_g)]
        a_rk = [masked(m_k[g][C:], upto[dirs[g]]) for g in range(n_g)]
        w_v = [hdot(jnp.concatenate([a_ak[g], a_rk[g]], axis=0), vv[g]) for g in range(n_g)]
    else:
        w_v = [hdot(a_ak[g], vv[g]) for g in range(n_g)]
    yield

    npow = [hdot(a_ab[g], a_ab[g]) for g in range(n_g)]
    inv = [eye + a_ab[g].astype(F32) for g in range(n_g)]
    yield
    n_factors = C.bit_length() - 1
    for j in range(1, n_factors):
        last = j == n_factors - 1
        pw = [npow[g].astype(BF16) for g in range(n_g)]
        lhs_j = [inv[g].astype(BF16) if last else jnp.concatenate([inv[g].astype(BF16), pw[g]], axis=0)
                 for g in range(n_g)]
        prod = [hdot(lhs_j[g], pw[g]) for g in range(n_g)]
        inv = [inv[g] + prod[g][:C] for g in range(n_g)]
        if not last:
            npow = [prod[g][C:] for g in range(n_g)]
        yield
    inv = [inv[g].astype(BF16) for g in range(n_g)]

    a_p = [hdot(inv[g], at[g]).astype(BF16) for g in range(n_g)]
    yield
    u0 = [hdot(inv[g], w_v[g][:C].astype(BF16)) for g in range(n_g)]
    yield

    h_old = [h_scr[:, g * GROUP:(g + 1) * GROUP] for g in range(n_g)]
    decay_rows = [(eye * pc_ref[dirs[g], 0:1, lanes_of(g)]).astype(BF16) for g in range(n_g)]
    stack = [jnp.concatenate([a_p[g]] + ([rt[g]] if emit_y else []) + [decay_rows[g]], axis=0)
             for g in range(n_g)]
    s1 = [hdot(stack[g], h_old[g].astype(BF16)) for g in range(n_g)]
    u = [(s1[g][:C] + u0[g]).astype(BF16) for g in range(n_g)]
    yield
    if emit_y:
        y_u = [hdot(a_rb[g], u[g]) for g in range(n_g)]
        for g in range(n_g):
            y_ref = yf_ref if dirs[g] == 0 else yb_ref
            y_ref[:, lanes_of(g)] = (s1[g][C:2 * C] + y_u[g] + w_v[g][C:]).astype(y_ref.dtype)
    else:
        yf_ref[...] = jnp.zeros(yf_ref.shape, yf_ref.dtype)
        yb_ref[...] = jnp.zeros(yb_ref.shape, yb_ref.dtype)
    yield

    bh, kh = grp("bh"), grp("kh")
    full = [lax.dot_general(jnp.concatenate([bh[g], kh[g]], axis=0),
                            jnp.concatenate([u[g], vv[g]], axis=0), tn, preferred_element_type=F32)
            for g in range(n_g)]
    lane_head = lax.broadcasted_iota(jnp.int32, (HEAD_DIM, GROUP), 1) // HEAD_DIM
    for g in range(n_g):
        upd = full[g][(HEADS_PER_GROUP - 1) * HEAD_DIM:]
        for j in range(HEADS_PER_GROUP - 2, -1, -1):
            upd = jnp.where(lane_head == j, full[g][j * HEAD_DIM:(j + 1) * HEAD_DIM], upd)
        h_scr[:, g * GROUP:(g + 1) * GROUP] = s1[g][-C:] + upd


def _rwkv_kernel(*refs, n_chunks, rotary, emit_y):
    per_dir = 5
    in_f, in_b = refs[:per_dir], refs[per_dir:2 * per_dir]
    prm = refs[2 * per_dir:2 * per_dir + 8]
    s0_ref = refs[2 * per_dir + 8]
    yf_ref, yb_ref, bonf_ref, bonb_ref, sout_ref, h_scr, ops_scr, pc_scr = refs[2 * per_dir + 9:]
    step = pl.program_id(1)

    @pl.when(step == 0)
    def _():
        h_scr[:, :D_MODEL] = s0_ref[0]
        h_scr[:, D_MODEL:] = s0_ref[1]
        ops_scr[0] = jnp.zeros(ops_scr.shape[1:], ops_scr.dtype)
        pc_scr[0] = jnp.ones(pc_scr.shape[1:], pc_scr.dtype)

    nxt = jnp.minimum(step, n_chunks - 1)

    def body(read, write):
        same_head = _block_diag_masks()
        ones_bd = jnp.where(same_head, 1.0, 0.0).astype(BF16)
        _interleave(
            _rwkv_chunk_matmuls(ops_scr.at[read], pc_scr.at[read], h_scr, yf_ref, yb_ref, same_head, emit_y),
            _rwkv_prepare(0, nxt, n_chunks, in_f + (bonf_ref,), prm, ones_bd, rotary,
                          ops_scr.at[write], pc_scr.at[write]),
            _rwkv_prepare(1, n_chunks - 1 - nxt, n_chunks, in_b + (bonb_ref,), prm, ones_bd, rotary,
                          ops_scr.at[write], pc_scr.at[write]))

    for parity in range(2):
        pl.when(step % 2 == parity)(functools.partial(body, parity, 1 - parity))

    @pl.when(step == n_chunks)
    def _():
        sout_ref[0] = h_scr[:, :D_MODEL]
        sout_ref[1] = h_scr[:, D_MODEL:]


def _rope_tables(T):
    half = HEAD_DIM // 2
    n_freq = half // 2
    t = np.arange(T)[:, None]
    lane = np.arange(LANES)[None, :]
    inv_freq = ROPE_THETA ** (-(np.arange(n_freq, dtype=np.float32)) / n_freq)
    pos = np.where((lane % HEAD_DIM) < half, t // GRID_W, t % GRID_W).astype(np.float32)
    ang = (pos * inv_freq[lane % n_freq].astype(np.float32)).astype(np.float32)
    sign = np.where((lane % half) < n_freq, -1.0, 1.0)
    return jnp.asarray(np.cos(ang), F32), jnp.asarray(np.sin(ang) * sign, F32)


def _rwkv_scan(u, state0, params, rotary, emit_y):
    B, T, _ = u.shape
    C = CHUNK
    n_chunks = T // C
    halo = 16
    per = C // halo

    def mirror(d, c):
        return c if d == 0 else n_chunks - 1 - c

    def chunk_of(d, s):
        return mirror(d, jnp.minimum(s, n_chunks - 1))

    def done_of(d, s):
        return mirror(d, jnp.maximum(s - 1, 0))

    def specs(d):
        rkv = COL_RKV // (3 * D_MODEL)
        main = pl.BlockSpec((None, C, 3 * D_MODEL), lambda b, c: (b, chunk_of(d, c), rkv))
        prev = pl.BlockSpec((None, halo, 3 * D_MODEL),
                            lambda b, c: (b, jnp.maximum(chunk_of(d, c) * per - 1, 0), rkv))
        nxt = pl.BlockSpec((None, halo, 3 * D_MODEL),
                           lambda b, c: (b, jnp.minimum((chunk_of(d, c) + 1) * per, T // halo - 1), rkv))
        lora = pl.BlockSpec((None, C, 4 * LORA), lambda b, c: (b, chunk_of(d, c), COL_LORA // (4 * LORA)))
        tab = pl.BlockSpec((C, 2 * LANES), lambda b, c: (chunk_of(d, c), 0))
        return [main, prev, nxt, lora, tab]

    def whole(shape):
        return pl.BlockSpec(shape, lambda b, c: (0,) * len(shape))

    rope_tab = jnp.concatenate(_rope_tables(T), axis=1)
    state = pl.BlockSpec((None, 2, HEAD_DIM, D_MODEL), lambda b, c: (b, 0, 0, 0))
    param_specs = [whole((3, D_MODEL)), whole((2, D_MODEL)), whole((2, 2 * LORA, D_MODEL)),
                   whole((2, D_MODEL)), whole((2, 2 * LORA, D_MODEL)),
                   whole((1, D_MODEL)), whole((1, D_MODEL)), whole((1, D_MODEL))]
    seq = jax.ShapeDtypeStruct((B, T, D_MODEL), BF16)
    y_f = pl.BlockSpec((None, C, D_MODEL), lambda b, c: (b, done_of(0, c), 0))
    y_b = pl.BlockSpec((None, C, D_MODEL), lambda b, c: (b, done_of(1, c), 0))
    bon_f = pl.BlockSpec((None, C, D_MODEL), lambda b, c: (b, chunk_of(0, c), 0))
    bon_b = pl.BlockSpec((None, C, D_MODEL), lambda b, c: (b, chunk_of(1, c), 0))
    seq_in = (u,) * 4 + (rope_tab,)
    return pl.pallas_call(
        functools.partial(_rwkv_kernel, n_chunks=n_chunks, rotary=rotary, emit_y=emit_y),
        out_shape=(seq, seq, seq, seq, jax.ShapeDtypeStruct((B, 2, HEAD_DIM, D_MODEL), F32)),
        grid=(B, n_chunks + 1),
        in_specs=specs(0) + specs(1) + param_specs + [state],
        out_specs=(y_f, y_b, bon_f, bon_b, state),
        scratch_shapes=[pltpu.VMEM((HEAD_DIM, 2 * D_MODEL), F32),
                        pltpu.VMEM((2, 2 * len(OPERANDS), C, D_MODEL), BF16),
                        pltpu.VMEM((2, 2, 8, D_MODEL), F32)],
        compiler_params=_cparams(("arbitrary", "arbitrary")),
        name="rwkv7_chunk_scan_rot" if rotary else "rwkv7_chunk_scan_ctx",
    )(*seq_in, *seq_in, *params, state0)


def _final_kernel(x_ref, ya_ref, yf_ref, yb_ref, bf_ref, bb_ref, z_ref, ga_ref, gb_ref, gate_ref,
                  gng_ref, gnb_ref, wa_ref, wb_ref, wo_ref, lng_ref, lnb_ref, o_ref):
    same_head = _block_diag_masks()
    ones_bd = jnp.where(same_head, 1.0, 0.0).astype(BF16)
    y = yf_ref[...].astype(F32) + yb_ref[...].astype(F32)
    mu = _head_sum(y, ones_bd) * (1.0 / HEAD_DIM)
    yc = y - mu
    var = _head_sum(yc * yc, ones_bd) * (1.0 / HEAD_DIM)
    yn = yc * lax.rsqrt(var + GN_EPS) * gng_ref[...] + gnb_ref[...]
    yn = yn + bf_ref[...].astype(F32) + bb_ref[...].astype(F32)
    y_b = (yn * _silu(z_ref[...].astype(F32))).astype(BF16)
    p_a = jnp.dot(ya_ref[...], wa_ref[...], preferred_element_type=F32)
    p_b = jnp.dot(y_b, wb_ref[...], preferred_element_type=F32)
    merged = (jax.nn.sigmoid(ga_ref[...].astype(F32)) * p_a
              + jax.nn.sigmoid(gb_ref[...].astype(F32)) * p_b)
    out = jnp.dot(merged.astype(BF16), wo_ref[...], preferred_element_type=F32)
    t = ALPHA * x_ref[...] + gate_ref[...] * out
    m = jnp.mean(t, axis=-1, keepdims=True)
    tc = t - m
    v = jnp.mean(tc * tc, axis=-1, keepdims=True)
    o_ref[...] = tc * lax.rsqrt(v + LN_EPS) * lng_ref[...] + lnb_ref[...]


def _final_stage(x, y_a, y_f, y_b, bonus_f, bonus_b, u, mod3, gn_g, gn_b, wa, wb, wo, ln_g, ln_b, tm):
    B, T, _ = x.shape

    def tok():
        return pl.BlockSpec((None, tm, D_MODEL), lambda b, i: (b, i, 0))

    def ucol(blk):
        return pl.BlockSpec((None, tm, D_MODEL), lambda b, i: (b, i, blk))

    def vec():
        return pl.BlockSpec((1, D_MODEL), lambda b, i: (0, 0))

    def mat():
        return pl.BlockSpec((D_MODEL, D_MODEL), lambda b, i: (0, 0))

    return pl.pallas_call(
        _final_kernel,
        out_shape=jax.ShapeDtypeStruct((B, T, D_MODEL), F32),
        grid=(B, T // tm),
        in_specs=[tok(), tok(), tok(), tok(), tok(), tok(), ucol(7), ucol(8), ucol(9),
                  pl.BlockSpec((None, 1, D_MODEL), lambda b, i: (b, 0, 2)),
                  vec(), vec(), mat(), mat(), mat(), vec(), vec()],
        out_specs=tok(),
        compiler_params=_cparams(("arbitrary", "arbitrary")),
        name="readout_merge_out_proj",
    )(x, y_a, y_f, y_b, bonus_f, bonus_b, u, u, u, mod3, gn_g, gn_b, wa, wb, wo, ln_g, ln_b)


def _layer(x, c, ctx, c_ctx, w_ada, b_ada, w_in, na_rpb, rw_mu, rw_w0, rw_w_up, rw_a0, rw_a_up,
           rw_k_k, rw_k_a, rw_r_k, rw_gn_g, rw_gn_b, w_branch_a, w_branch_b, w_out, ln_g, ln_b):
    B, T, _ = x.shape
    L = ctx.shape[1]
    rows = T // GRID_W

    n_cond = -(-(B + 1) // 8) * 8
    cond = jnp.zeros((n_cond, D_MODEL), F32).at[:B].set(c).at[B].set(c_ctx)
    mod3 = _modulation(cond, w_ada, b_ada).reshape(n_cond, 1, 3 * D_MODEL)

    w_bf = jnp.concatenate(
        [w_in[:, 4 * D_MODEL:7 * D_MODEL], w_in[:, :4 * D_MODEL], w_in[:, 7 * D_MODEL:N_MAIN],
         w_in[:, N_MAIN + 4 * LORA:], w_in[:, N_MAIN:N_MAIN + 4 * LORA],
         jnp.zeros((D_MODEL, N_PAD - N_PROJ), w_in.dtype)], axis=1).astype(BF16)
    u = _in_projection(x, mod3, w_bf, lambda b: b, tm=min(T, 2048), tn=1536)
    u_ctx = _in_projection(ctx, mod3, w_bf, lambda b: B, tm=L, tn=1536)

    y_a = _neighbourhood_attention(u, u_ctx, _na_bias_table(na_rpb, rows))

    def lora_pad(w_up):
        out = jnp.zeros((2, 2 * LORA, D_MODEL), F32)
        for d in range(2):
            out = out.at[d, d * LORA:(d + 1) * LORA].set(w_up[d])
        return out.astype(BF16)

    params = (rw_mu, rw_w0, lora_pad(rw_w_up), rw_a0, lora_pad(rw_a_up), rw_k_k.reshape(1, D_MODEL),
              rw_k_a.reshape(1, D_MODEL), rw_r_k.reshape(1, D_MODEL))
    state0 = jnp.zeros((B, 2, HEAD_DIM, D_MODEL), F32)
    state_c = _rwkv_scan(u_ctx, state0, params, rotary=False, emit_y=False)[-1]
    y_f, y_b, bonus_f, bonus_b, _ = _rwkv_scan(u, state_c, params, rotary=True, emit_y=True)

    return _final_stage(x, y_a, y_f, y_b, bonus_f, bonus_b, u, mod3, rw_gn_g.reshape(1, D_MODEL),
                        rw_gn_b.reshape(1, D_MODEL), w_branch_a.astype(BF16), w_branch_b.astype(BF16),
                        w_out.astype(BF16), ln_g.reshape(1, D_MODEL), ln_b.reshape(1, D_MODEL),
                        tm=min(T, 512))


def kernel(x, c, ctx, c_ctx, w_ada, b_ada, w_in, na_rpb, rw_mu, rw_w0, rw_w_up, rw_a0, rw_a_up, rw_k_k, rw_k_a, rw_r_k, rw_gn_g, rw_gn_b, w_branch_a, w_branch_b, w_out, ln_g, ln_b):
    assert w_ada.shape[0] == DEPTH
    return _layer(x, c, ctx, c_ctx, w_ada[0], b_ada[0], w_in[0], na_rpb[0], rw_mu[0], rw_w0[0],
                  rw_w_up[0], rw_a0[0], rw_a_up[0], rw_k_k[0], rw_k_a[0], rw_r_k[0], rw_gn_g[0],
                  rw_gn_b[0], w_branch_a[0], w_branch_b[0], w_out[0], ln_g[0], ln_b[0])
```

```python
import functools

import numpy as np
import jax
import jax.numpy as jnp
from jax import lax
from jax.experimental import pallas as pl
from jax.experimental.pallas import tpu as pltpu

F32 = jnp.float32
BF16 = jnp.bfloat16
HIGHEST = lax.Precision.HIGHEST

D_MODEL = 1024
GRID_W = 64
HEADS = 16
HEAD_DIM = 64
NA_MAX_ROWS = 8
NA_COLS = 16
LORA = 64
DEPTH = 1
ROPE_THETA = 10000.0
LN_EPS = 1e-5
GN_EPS = 64e-5
ALPHA = (2 * DEPTH) ** 0.25

LANES = 128
GROUP = 256
HEADS_PER_GROUP = GROUP // HEAD_DIM
N_GROUPS = D_MODEL // GROUP
CHUNK = 64
RWKV_BLOCK_CHUNKS = 8

COL_RKV = 0
COL_NA = 3 * D_MODEL
COL_ZB = 7 * D_MODEL
N_MAIN = 8 * D_MODEL
COL_MG = N_MAIN
COL_LORA = N_MAIN + 2 * D_MODEL
N_PROJ = COL_LORA + 4 * LORA
N_PAD = 21 * 512
VMEM_LIMIT = 56 * 1024 * 1024


def _cparams(sem):
    return pltpu.CompilerParams(dimension_semantics=sem, vmem_limit_bytes=VMEM_LIMIT)


def _silu(x):
    return x * jax.nn.sigmoid(x)


def _mod_kernel(c_ref, w_ref, b_ref, o_ref):
    s = _silu(c_ref[...])
    o_ref[...] = jnp.dot(s, w_ref[...], preferred_element_type=F32, precision=HIGHEST) + b_ref[...]


def _modulation(cond, w_ada, b_ada):
    rows = cond.shape[0]
    return pl.pallas_call(
        _mod_kernel,
        out_shape=jax.ShapeDtypeStruct((rows, 3 * D_MODEL), F32),
        grid=(3,),
        in_specs=[pl.BlockSpec((rows, D_MODEL), lambda j: (0, 0)),
                  pl.BlockSpec((D_MODEL, D_MODEL), lambda j: (0, j)),
                  pl.BlockSpec((1, D_MODEL), lambda j: (0, j))],
        out_specs=pl.BlockSpec((rows, D_MODEL), lambda j: (0, j)),
        compiler_params=_cparams(("arbitrary",)),
        name="ada_modulation",
    )(cond, w_ada, b_ada.reshape(1, 3 * D_MODEL))


def _inproj_kernel(x_ref, sh_ref, sc_ref, w_ref, o_ref, h_scr):
    @pl.when(pl.program_id(2) == 0)
    def _():
        x = x_ref[...]
        mu = jnp.mean(x, axis=-1, keepdims=True)
        xc = x - mu
        var = jnp.mean(xc * xc, axis=-1, keepdims=True)
        y = xc * lax.rsqrt(var + LN_EPS)
        h_scr[...] = (y * (1.0 + sc_ref[...]) + sh_ref[...]).astype(BF16)

    o_ref[...] = jnp.dot(h_scr[...], w_ref[...], preferred_element_type=F32).astype(BF16)


def _in_projection(x, mod3, w_bf, row_of_batch, tm, tn):
    B, T, _ = x.shape
    return pl.pallas_call(
        _inproj_kernel,
        out_shape=jax.ShapeDtypeStruct((B, T, N_PAD), BF16),
        grid=(B, T // tm, N_PAD // tn),
        in_specs=[pl.BlockSpec((None, tm, D_MODEL), lambda b, i, j: (b, i, 0)),
                  pl.BlockSpec((None, 1, D_MODEL), lambda b, i, j: (row_of_batch(b), 0, 0)),
                  pl.BlockSpec((None, 1, D_MODEL), lambda b, i, j: (row_of_batch(b), 0, 1)),
                  pl.BlockSpec((D_MODEL, tn), lambda b, i, j: (0, j))],
        out_specs=pl.BlockSpec((None, tm, tn), lambda b, i, j: (b, i, j)),
        scratch_shapes=[pltpu.VMEM((tm, D_MODEL), BF16)],
        compiler_params=_cparams(("arbitrary", "arbitrary", "arbitrary")),
        name="ln_mod_in_proj",
    )(x, mod3, mod3, w_bf)


NA_HEADS_PER_STEP = GROUP // HEAD_DIM
NA_ROWS_PER_ITER = 8


def _na_kernel(q_ref, k_ref, v_ref, z_ref, kc_ref, vc_ref, bias_ref, o_ref, *, rows):
    kh = min(NA_MAX_ROWS, rows)
    band = kh * GRID_W
    nh = NA_HEADS_PER_STEP
    lane = lax.broadcasted_iota(jnp.int32, (1, GROUP), 1)
    head_lanes = [(lane // HEAD_DIM) == h for h in range(nh)]
    kc = kc_ref[...]
    vc = vc_ref[...]
    nt = (((1,), (1,)), ((), ()))

    n_rows = NA_ROWS_PER_ITER if rows % NA_ROWS_PER_ITER == 0 else 1
    stack = nh * GRID_W

    def rows_body(it, carry):
        qs, s, vb, q0s = [], [], [], []
        for r in range(n_rows):
            i = it * n_rows + r
            rs = jnp.clip(i - kh // 2, 0, rows - kh)
            q0 = pl.multiple_of(i * GRID_W, GRID_W)
            k0 = pl.multiple_of(rs * GRID_W, GRID_W)
            q = q_ref[pl.ds(q0, GRID_W), :] * jnp.asarray(HEAD_DIM ** -0.5, BF16)
            q_st = jnp.concatenate([jnp.where(head_lanes[h], q, jnp.zeros_like(q)) for h in range(nh)], axis=0)
            s.append(lax.dot_general(q_st, k_ref[pl.ds(k0, band), :], nt, preferred_element_type=F32)
                     + bias_ref[i - rs])
            vb.append(v_ref[pl.ds(k0, band), :])
            qs.append(q_st)
            q0s.append(q0)
        sc = lax.dot_general(jnp.concatenate(qs, axis=0), kc, nt, preferred_element_type=F32)
        p, pcs, denom = [], [], []
        for r in range(n_rows):
            sc_r = sc[r * stack:(r + 1) * stack]
            m = jnp.maximum(jnp.max(s[r], axis=-1, keepdims=True), jnp.max(sc_r, axis=-1, keepdims=True))
            p_r = jnp.exp(s[r] - m)
            pc_r = jnp.exp(sc_r - m)
            denom.append(jnp.sum(p_r, axis=-1, keepdims=True) + jnp.sum(pc_r, axis=-1, keepdims=True))
            p.append(p_r.astype(BF16))
            pcs.append(pc_r.astype(BF16))
        o_ctx = jnp.dot(jnp.concatenate(pcs, axis=0), vc, preferred_element_type=F32)
        for r in range(n_rows):
            o_all = (jnp.dot(p[r], vb[r], preferred_element_type=F32)
                     + o_ctx[r * stack:(r + 1) * stack]) / denom[r]
            o = o_all[(nh - 1) * GRID_W:]
            for h in range(nh - 2, -1, -1):
                o = jnp.where(head_lanes[h], o_all[h * GRID_W:(h + 1) * GRID_W], o)
            z = z_ref[pl.ds(q0s[r], GRID_W), :].astype(F32)
            o_ref[pl.ds(q0s[r], GRID_W), :] = (o * _silu(z)).astype(BF16)
        return carry

    lax.fori_loop(0, rows // n_rows, rows_body, 0)


def _na_bias_table(rpb, rows):
    kh = min(NA_MAX_ROWS, rows)
    nh = NA_HEADS_PER_STEP
    n_dj = 2 * NA_COLS - 1
    cols = np.arange(GRID_W)
    cstart = np.clip(cols - NA_COLS // 2, 0, GRID_W - NA_COLS)
    col_mask = (cols[None, :] >= cstart[:, None]) & (cols[None, :] < cstart[:, None] + NA_COLS)
    dj = np.clip(cols[None, :] - cols[:, None] + NA_COLS - 1, 0, n_dj - 1)
    onehot = (np.arange(n_dj)[:, None] == dj.reshape(1, -1)).astype(np.float32)
    exp = jnp.dot(rpb.reshape(-1, n_dj), jnp.asarray(onehot), precision=HIGHEST)
    exp = exp.reshape(HEADS, 2 * NA_MAX_ROWS - 1, GRID_W, GRID_W)
    exp = jnp.where(col_mask[None, None], exp, -jnp.inf)
    lo = NA_MAX_ROWS - 1
    tab = jnp.stack([exp[:, lo - o: lo - o + kh] for o in range(kh)], axis=1)
    tab = tab.reshape(HEADS // nh, nh, kh, kh, GRID_W, GRID_W)
    return tab.transpose(0, 2, 1, 4, 3, 5).reshape(HEADS // nh, kh, nh * GRID_W, kh * GRID_W)


def _neighbourhood_attention(u, u_ctx, bias_tab):
    B, T, _ = u.shape
    L = u_ctx.shape[1]
    rows = T // GRID_W
    kh = min(NA_MAX_ROWS, rows)
    blk = D_MODEL // GROUP
    first = COL_NA // GROUP

    def col(part):
        return pl.BlockSpec((None, T, GROUP), lambda b, g: (b, 0, first + part * blk + g))

    def col_ctx(part):
        return pl.BlockSpec((None, L, GROUP), lambda b, g: (b, 0, first + part * blk + g))

    return pl.pallas_call(
        functools.partial(_na_kernel, rows=rows),
        out_shape=jax.ShapeDtypeStruct((B, T, D_MODEL), BF16),
        grid=(B, blk),
        in_specs=[col(0), col(1), col(2), col(3), col_ctx(1), col_ctx(2),
                  pl.BlockSpec((None, kh, NA_HEADS_PER_STEP * GRID_W, kh * GRID_W), lambda b, g: (g, 0, 0, 0))],
        out_specs=pl.BlockSpec((None, T, GROUP), lambda b, g: (b, 0, g)),
        compiler_params=_cparams(("arbitrary", "arbitrary")),
        name="neighbourhood_attention",
    )(u, u, u, u, u_ctx, u_ctx, bias_tab)


def _block_diag_masks():
    r = lax.broadcasted_iota(jnp.int32, (GROUP, GROUP), 0)
    c = lax.broadcasted_iota(jnp.int32, (GROUP, GROUP), 1)
    return (r // HEAD_DIM) == (c // HEAD_DIM)


def _head_sum(x, ones_bd):
    rows = x.shape[0]
    xb = x.astype(BF16)
    stacked = jnp.concatenate([xb[:, g * GROUP:(g + 1) * GROUP] for g in range(N_GROUPS)], axis=0)
    sums = jnp.dot(stacked, ones_bd, preferred_element_type=F32)
    return jnp.concatenate([sums[g * rows:(g + 1) * rows] for g in range(N_GROUPS)], axis=1)


def _rwkv_prepare(d, local, cidx, n_chunks, refs, prm, ones_bd, rotary, ops_ref, pc_ref):
    rkv_ref, prev_ref, next_ref, lo_ref, rope_ref, bonus_ref = refs
    mu_ref, w0_ref, wup_ref, a0_ref, aup_ref, kk_ref, ka_ref, rk_ref = prm
    C = CHUNK
    halo = prev_ref.shape[0]
    block = rkv_ref.shape[0]
    rows = pl.ds(pl.multiple_of(local * C, C), C)
    rows_before = pl.ds(pl.multiple_of(jnp.maximum(local * C - halo, 0), halo), halo)
    rows_after = pl.ds(pl.multiple_of(jnp.minimum(local * C + C, block - halo), halo), halo)
    chunk_id = jnp.full((halo, D_MODEL), local, jnp.int32)
    at_block_start = chunk_id == 0
    at_block_end = chunk_id == block // C - 1

    def store(name, value):
        ops_ref[d * len(OPERANDS) + OPERANDS.index(name)] = value.astype(BF16)

    tt = lax.broadcasted_iota(jnp.int32, (C, C + 2 * halo), 0)
    ss = lax.broadcasted_iota(jnp.int32, (C, C + 2 * halo), 1) - halo
    first = jnp.where(cidx > 0, -1, 0)
    last = jnp.where(cidx < n_chunks - 1, C, C - 1)
    nb = (jnp.abs(ss - tt) == 1) & (ss >= first) & (ss <= last)
    nb = jnp.where(nb, 1.0, 0.0).astype(BF16)

    def shifted(i):
        lanes = slice(i * D_MODEL, (i + 1) * D_MODEL)
        x = rkv_ref[rows, lanes]
        before = jnp.where(at_block_start, prev_ref[:, lanes], rkv_ref[rows_before, lanes])
        after = jnp.where(at_block_end, next_ref[:, lanes], rkv_ref[rows_after, lanes])
        both = jnp.dot(nb, jnp.concatenate([before, x, after], axis=0), preferred_element_type=F32)
        mu = mu_ref[i:i + 1, :]
        return (1.0 - mu) * x.astype(F32) + (0.5 * mu) * both

    r_s = shifted(0)
    yield
    k_s = shifted(1)
    yield
    v_s = shifted(2)
    store("v", v_s)
    yield

    if rotary:
        lane = lax.broadcasted_iota(jnp.int32, (1, D_MODEL), 1)
        low = (lane % 32) < 16
        cos_t = jnp.tile(rope_ref[rows, :LANES], (1, D_MODEL // LANES))
        sin_t = jnp.tile(rope_ref[rows, LANES:], (1, D_MODEL // LANES))

        def rope(x):
            partner = jnp.where(low, pltpu.roll(x, D_MODEL - 16, axis=1), pltpu.roll(x, 16, axis=1))
            return x * cos_t + partner * sin_t

        r_s = rope(r_s)
        yield
        k_s = rope(k_s)
        yield

    lo = lo_ref[rows, :]
    lw = w0_ref[d:d + 1, :] + jnp.dot(jnp.tanh(lo[:, :2 * LORA].astype(F32)).astype(BF16), wup_ref[d],
                                     preferred_element_type=F32)
    ld = (-np.exp(-0.5)) * jax.nn.sigmoid(lw)
    yield
    a = jax.nn.sigmoid(a0_ref[d:d + 1, :] + jnp.dot(lo[:, 2 * LORA:], aup_ref[d], preferred_element_type=F32))
    yield

    kk = k_s * kk_ref[...]
    kk = kk * jnp.minimum(lax.rsqrt(_head_sum(kk * kk, ones_bd)), 1e12)
    yield
    k_dir = k_s * (1.0 + (a - 1.0) * ka_ref[...])
    b_vec = kk * a
    bonus_ref[rows, :] = (_head_sum(r_s * rk_ref[...] * k_dir, ones_bd) * v_s).astype(bonus_ref.dtype)
    yield

    tt = lax.broadcasted_iota(jnp.int32, (C, C), 0)
    ss = lax.broadcasted_iota(jnp.int32, (C, C), 1)
    tri = jnp.where((ss <= tt) if d == 0 else (ss >= tt), 1.0, 0.0).astype(BF16)
    ld_hi = ld.astype(BF16)
    ld_lo = (ld - ld_hi.astype(F32)).astype(BF16)
    cl = (jnp.dot(tri, ld_hi, preferred_element_type=F32) + jnp.dot(tri, ld_lo, preferred_element_type=F32))
    cl_tot = cl[C - 1:C, :] if d == 0 else cl[0:1, :]
    yield
    e_neg = jnp.exp(-cl)
    p_c = jnp.exp(cl_tot)
    pc_ref[d] = jnp.broadcast_to(p_c, pc_ref.shape[1:])
    b_t = b_vec * e_neg
    store("b", b_t)
    store("bh", b_t * p_c)
    yield
    k_t = k_dir * e_neg
    store("k", k_t)
    store("kh", k_t * p_c)
    yield
    store("a", -kk * jnp.exp(cl - ld))
    store("r", r_s * jnp.exp(cl))


OPERANDS = ("a", "r", "b", "k", "bh", "kh", "v")


def _interleave(*gens):
    live = list(gens)
    while live:
        for gen in list(live):
            try:
                next(gen)
            except StopIteration:
                live.remove(gen)


def _rwkv_chunk_matmuls(ops_ref, pc_ref, h_scr, yf_ref, yb_ref, local_f, local_b, same_head, emit_y):
    C = CHUNK
    n_ops = len(OPERANDS)
    n_g = 2 * N_GROUPS
    dirs = [g // N_GROUPS for g in range(n_g)]

    def lanes_of(g):
        return slice((g % N_GROUPS) * GROUP, (g % N_GROUPS + 1) * GROUP)

    def grp(name):
        i = OPERANDS.index(name)
        return [ops_ref[dirs[g] * n_ops + i, :, lanes_of(g)] for g in range(n_g)]

    t_i = lax.broadcasted_iota(jnp.int32, (C, GROUP), 0)
    s_i = lax.broadcasted_iota(jnp.int32, (C, GROUP), 1) % C
    before = [s_i < t_i, s_i > t_i]
    upto = [s_i <= t_i, s_i >= t_i]
    eye = jnp.where(t_i == s_i, 1.0, 0.0).astype(F32)
    nt = (((1,), (1,)), ((), ()))
    tn = (((0,), (0,)), ((), ()))

    def bd(x):
        return jnp.where(same_head, jnp.tile(x, (HEADS_PER_GROUP, 1)), jnp.zeros((), x.dtype))

    def hdot(a_side, x):
        return jnp.dot(a_side, bd(x), preferred_element_type=F32)

    def masked(m, keep):
        return jnp.where(keep, m, 0.0).astype(BF16)

    at, rt = grp("a"), grp("r")
    lhs = [jnp.concatenate([at[g], rt[g]], axis=0) for g in range(n_g)] if emit_y else at
    bt = grp("b")
    m_b = [lax.dot_general(lhs[g], bd(bt[g]), nt, preferred_element_type=F32) for g in range(n_g)]
    yield
    kt = grp("k")
    m_k = [lax.dot_general(lhs[g], bd(kt[g]), nt, preferred_element_type=F32) for g in range(n_g)]
    yield
    a_ab = [masked(m_b[g][:C], before[dirs[g]]) for g in range(n_g)]
    a_ak = [masked(m_k[g][:C], before[dirs[g]]) for g in range(n_g)]
    vv = grp("v")
    if emit_y:
        a_rb = [masked(m_b[g][C:], upto[dirs[g]]) for g in range(n_g)]
        a_rk = [masked(m_k[g][C:], upto[dirs[g]]) for g in range(n_g)]
        w_v = [hdot(jnp.concatenate([a_ak[g], a_rk[g]], axis=0), vv[g]) for g in range(n_g)]
    else:
        w_v = [hdot(a_ak[g], vv[g]) for g in range(n_g)]
    yield

    npow = [hdot(a_ab[g], a_ab[g]) for g in range(n_g)]
    inv = [eye + a_ab[g].astype(F32) for g in range(n_g)]
    yield
    n_factors = C.bit_length() - 1
    for j in range(1, n_factors):
        last = j == n_factors - 1
        pw = [npow[g].astype(BF16) for g in range(n_g)]
        lhs_j = [inv[g].astype(BF16) if last else jnp.concatenate([inv[g].astype(BF16), pw[g]], axis=0)
                 for g in range(n_g)]
        prod = [hdot(lhs_j[g], pw[g]) for g in range(n_g)]
        inv = [inv[g] + prod[g][:C] for g in range(n_g)]
        if not last:
            npow = [prod[g][C:] for g in range(n_g)]
        yield
    inv = [inv[g].astype(BF16) for g in range(n_g)]

    a_p = [hdot(inv[g], at[g]).astype(BF16) for g in range(n_g)]
    yield
    u0 = [hdot(inv[g], w_v[g][:C].astype(BF16)) for g in range(n_g)]
    yield

    h_old = [h_scr[:, g * GROUP:(g + 1) * GROUP] for g in range(n_g)]
    decay_rows = [(eye * pc_ref[dirs[g], 0:1, lanes_of(g)]).astype(BF16) for g in range(n_g)]
    stack = [jnp.concatenate([a_p[g]] + ([rt[g]] if emit_y else []) + [decay_rows[g]], axis=0)
             for g in range(n_g)]
    s1 = [hdot(stack[g], h_old[g].astype(BF16)) for g in range(n_g)]
    u = [(s1[g][:C] + u0[g]).astype(BF16) for g in range(n_g)]
    yield
    if emit_y:
        y_u = [hdot(a_rb[g], u[g]) for g in range(n_g)]
        out_rows = [pl.ds(pl.multiple_of(local * C, C), C) for local in (local_f, local_b)]
        for g in range(n_g):
            y_ref = yf_ref if dirs[g] == 0 else yb_ref
            y_ref[out_rows[dirs[g]], lanes_of(g)] = (s1[g][C:2 * C] + y_u[g] + w_v[g][C:]).astype(y_ref.dtype)
    yield

    bh, kh = grp("bh"), grp("kh")
    full = [lax.dot_general(jnp.concatenate([bh[g], kh[g]], axis=0),
                            jnp.concatenate([u[g], vv[g]], axis=0), tn, preferred_element_type=F32)
            for g in range(n_g)]
    lane_head = lax.broadcasted_iota(jnp.int32, (HEAD_DIM, GROUP), 1) // HEAD_DIM
    for g in range(n_g):
        upd = full[g][(HEADS_PER_GROUP - 1) * HEAD_DIM:]
        for j in range(HEADS_PER_GROUP - 2, -1, -1):
            upd = jnp.where(lane_head == j, full[g][j * HEAD_DIM:(j + 1) * HEAD_DIM], upd)
        h_scr[:, g * GROUP:(g + 1) * GROUP] = s1[g][-C:] + upd


def _rwkv_kernel(*refs, n_blocks, rotary, emit_y):
    per_dir = 5
    in_f, in_b = refs[:per_dir], refs[per_dir:2 * per_dir]
    prm = refs[2 * per_dir:2 * per_dir + 8]
    s0_ref = refs[2 * per_dir + 8]
    yf_ref, yb_ref, bonf_ref, bonb_ref, sout_ref, h_scr, ops_scr, pc_scr = refs[2 * per_dir + 9:]
    j = pl.program_id(1)
    per_block = in_f[0].shape[0] // CHUNK
    n_chunks = n_blocks * per_block

    @pl.when(j == 0)
    def _():
        h_scr[:, :D_MODEL] = s0_ref[0]
        h_scr[:, D_MODEL:] = s0_ref[1]

    if not emit_y:
        yf_ref[...] = jnp.zeros(yf_ref.shape, yf_ref.dtype)
        yb_ref[...] = jnp.zeros(yb_ref.shape, yb_ref.dtype)

    same_head = _block_diag_masks()
    ones_bd = jnp.where(same_head, 1.0, 0.0).astype(BF16)

    def local(d, s):
        return s if d == 0 else per_block - 1 - s

    def prepare(d, s, buf):
        block = j if d == 0 else n_blocks - 1 - j
        ins, bonus_ref = (in_f, bonf_ref) if d == 0 else (in_b, bonb_ref)
        return _rwkv_prepare(d, local(d, s), block * per_block + local(d, s), n_chunks, ins + (bonus_ref,), prm,
                             ones_bd, rotary, ops_scr.at[buf], pc_scr.at[buf])

    def matmuls(s, buf):
        return _rwkv_chunk_matmuls(ops_scr.at[buf], pc_scr.at[buf], h_scr, yf_ref, yb_ref,
                                   local(0, s), local(1, s), same_head, emit_y)

    _interleave(prepare(0, 0, 0), prepare(1, 0, 0))

    def chunk_pair(i, carry):
        s = 2 * i
        _interleave(matmuls(s, 0), prepare(0, s + 1, 1), prepare(1, s + 1, 1))
        ahead = jnp.minimum(s + 2, per_block - 1)
        _interleave(matmuls(s + 1, 1), prepare(0, ahead, 0), prepare(1, ahead, 0))
        return carry

    lax.fori_loop(0, per_block // 2, chunk_pair, 0)

    @pl.when(j == n_blocks - 1)
    def _():
        sout_ref[0] = h_scr[:, :D_MODEL]
        sout_ref[1] = h_scr[:, D_MODEL:]


def _rope_tables(T):
    half = HEAD_DIM // 2
    n_freq = half // 2
    t = np.arange(T)[:, None]
    lane = np.arange(LANES)[None, :]
    inv_freq = ROPE_THETA ** (-(np.arange(n_freq, dtype=np.float32)) / n_freq)
    pos = np.where((lane % HEAD_DIM) < half, t // GRID_W, t % GRID_W).astype(np.float32)
    ang = (pos * inv_freq[lane % n_freq].astype(np.float32)).astype(np.float32)
    sign = np.where((lane % half) < n_freq, -1.0, 1.0)
    return jnp.asarray(np.cos(ang), F32), jnp.asarray(np.sin(ang) * sign, F32)


def _rwkv_scan(u, state0, params, rotary, emit_y):
    B, T, _ = u.shape
    block = min(T, RWKV_BLOCK_CHUNKS * CHUNK)
    n_blocks = T // block
    assert T % block == 0 and (block // CHUNK) % 2 == 0
    halo = 16
    per = block // halo

    def block_of(d, j):
        return j if d == 0 else n_blocks - 1 - j

    def specs(d):
        rkv = COL_RKV // (3 * D_MODEL)
        main = pl.BlockSpec((None, block, 3 * D_MODEL), lambda b, j: (b, block_of(d, j), rkv))
        prev = pl.BlockSpec((None, halo, 3 * D_MODEL),
                            lambda b, j: (b, jnp.maximum(block_of(d, j) * per - 1, 0), rkv))
        nxt = pl.BlockSpec((None, halo, 3 * D_MODEL),
                           lambda b, j: (b, jnp.minimum((block_of(d, j) + 1) * per, T // halo - 1), rkv))
        lora = pl.BlockSpec((None, block, 4 * LORA), lambda b, j: (b, block_of(d, j), COL_LORA // (4 * LORA)))
        tab = pl.BlockSpec((block, 2 * LANES), lambda b, j: (block_of(d, j), 0))
        return [main, prev, nxt, lora, tab]

    def whole(shape):
        return pl.BlockSpec(shape, lambda b, c: (0,) * len(shape))

    rope_tab = jnp.concatenate(_rope_tables(T), axis=1)
    state = pl.BlockSpec((None, 2, HEAD_DIM, D_MODEL), lambda b, c: (b, 0, 0, 0))
    param_specs = [whole((3, D_MODEL)), whole((2, D_MODEL)), whole((2, 2 * LORA, D_MODEL)),
                   whole((2, D_MODEL)), whole((2, 2 * LORA, D_MODEL)),
                   whole((1, D_MODEL)), whole((1, D_MODEL)), whole((1, D_MODEL))]
    seq = jax.ShapeDtypeStruct((B, T, D_MODEL), BF16)
    out_f = pl.BlockSpec((None, block, D_MODEL), lambda b, j: (b, block_of(0, j), 0))
    out_b = pl.BlockSpec((None, block, D_MODEL), lambda b, j: (b, block_of(1, j), 0))
    seq_in = (u,) * 4 + (rope_tab,)
    return pl.pallas_call(
        functools.partial(_rwkv_kernel, n_blocks=n_blocks, rotary=rotary, emit_y=emit_y),
        out_shape=(seq, seq, seq, seq, jax.ShapeDtypeStruct((B, 2, HEAD_DIM, D_MODEL), F32)),
        grid=(B, n_blocks),
        in_specs=specs(0) + specs(1) + param_specs + [state],
        out_specs=(out_f, out_b, out_f, out_b, state),
        scratch_shapes=[pltpu.VMEM((HEAD_DIM, 2 * D_MODEL), F32),
                        pltpu.VMEM((2, 2 * len(OPERANDS), CHUNK, D_MODEL), BF16),
                        pltpu.VMEM((2, 2, 8, D_MODEL), F32)],
        compiler_params=_cparams(("arbitrary", "arbitrary")),
        name="rwkv7_chunk_scan_rot" if rotary else "rwkv7_chunk_scan_ctx",
    )(*seq_in, *seq_in, *params, state0)


def _final_kernel(x_ref, ya_ref, yf_ref, yb_ref, bf_ref, bb_ref, z_ref, ga_ref, gb_ref, gate_ref,
                  gng_ref, gnb_ref, wa_ref, wb_ref, wo_ref, lng_ref, lnb_ref, o_ref):
    same_head = _block_diag_masks()
    ones_bd = jnp.where(same_head, 1.0, 0.0).astype(BF16)
    y = yf_ref[...].astype(F32) + yb_ref[...].astype(F32)
    mu = _head_sum(y, ones_bd) * (1.0 / HEAD_DIM)
    yc = y - mu
    var = _head_sum(yc * yc, ones_bd) * (1.0 / HEAD_DIM)
    yn = yc * lax.rsqrt(var + GN_EPS) * gng_ref[...] + gnb_ref[...]
    yn = yn + bf_ref[...].astype(F32) + bb_ref[...].astype(F32)
    y_b = (yn * _silu(z_ref[...].astype(F32))).astype(BF16)
    p_a = jnp.dot(ya_ref[...], wa_ref[...], preferred_element_type=F32)
    p_b = jnp.dot(y_b, wb_ref[...], preferred_element_type=F32)
    merged = (jax.nn.sigmoid(ga_ref[...].astype(F32)) * p_a
              + jax.nn.sigmoid(gb_ref[...].astype(F32)) * p_b)
    out = jnp.dot(merged.astype(BF16), wo_ref[...], preferred_element_type=F32)
    t = ALPHA * x_ref[...] + gate_ref[...] * out
    m = jnp.mean(t, axis=-1, keepdims=True)
    tc = t - m
    v = jnp.mean(tc * tc, axis=-1, keepdims=True)
    o_ref[...] = tc * lax.rsqrt(v + LN_EPS) * lng_ref[...] + lnb_ref[...]


def _final_stage(x, y_a, y_f, y_b, bonus_f, bonus_b, u, mod3, gn_g, gn_b, wa, wb, wo, ln_g, ln_b, tm):
    B, T, _ = x.shape

    def tok():
        return pl.BlockSpec((None, tm, D_MODEL), lambda b, i: (b, i, 0))

    def ucol(blk):
        return pl.BlockSpec((None, tm, D_MODEL), lambda b, i: (b, i, blk))

    def vec():
        return pl.BlockSpec((1, D_MODEL), lambda b, i: (0, 0))

    def mat():
        return pl.BlockSpec((D_MODEL, D_MODEL), lambda b, i: (0, 0))

    return pl.pallas_call(
        _final_kernel,
        out_shape=jax.ShapeDtypeStruct((B, T, D_MODEL), F32),
        grid=(B, T // tm),
        in_specs=[tok(), tok(), tok(), tok(), tok(), tok(), ucol(7), ucol(8), ucol(9),
                  pl.BlockSpec((None, 1, D_MODEL), lambda b, i: (b, 0, 2)),
                  vec(), vec(), mat(), mat(), mat(), vec(), vec()],
        out_specs=tok(),
        compiler_params=_cparams(("arbitrary", "arbitrary")),
        name="readout_merge_out_proj",
    )(x, y_a, y_f, y_b, bonus_f, bonus_b, u, u, u, mod3, gn_g, gn_b, wa, wb, wo, ln_g, ln_b)


def _layer(x, c, ctx, c_ctx, w_ada, b_ada, w_in, na_rpb, rw_mu, rw_w0, rw_w_up, rw_a0, rw_a_up,
           rw_k_k, rw_k_a, rw_r_k, rw_gn_g, rw_gn_b, w_branch_a, w_branch_b, w_out, ln_g, ln_b):
    B, T, _ = x.shape
    L = ctx.shape[1]
    rows = T // GRID_W

    n_cond = -(-(B + 1) // 8) * 8
    cond = jnp.zeros((n_cond, D_MODEL), F32).at[:B].set(c).at[B].set(c_ctx)
    mod3 = _modulation(cond, w_ada, b_ada).reshape(n_cond, 1, 3 * D_MODEL)

    w_bf = jnp.concatenate(
        [w_in[:, 4 * D_MODEL:7 * D_MODEL], w_in[:, :4 * D_MODEL], w_in[:, 7 * D_MODEL:N_MAIN],
         w_in[:, N_MAIN + 4 * LORA:], w_in[:, N_MAIN:N_MAIN + 4 * LORA],
         jnp.zeros((D_MODEL, N_PAD - N_PROJ), w_in.dtype)], axis=1).astype(BF16)
    u = _in_projection(x, mod3, w_bf, lambda b: b, tm=min(T, 2048), tn=1536)
    u_ctx = _in_projection(ctx, mod3, w_bf, lambda b: B, tm=L, tn=1536)

    y_a = _neighbourhood_attention(u, u_ctx, _na_bias_table(na_rpb, rows))

    def lora_pad(w_up):
        out = jnp.zeros((2, 2 * LORA, D_MODEL), F32)
        for d in range(2):
            out = out.at[d, d * LORA:(d + 1) * LORA].set(w_up[d])
        return out.astype(BF16)

    params = (rw_mu, rw_w0, lora_pad(rw_w_up), rw_a0, lora_pad(rw_a_up), rw_k_k.reshape(1, D_MODEL),
              rw_k_a.reshape(1, D_MODEL), rw_r_k.reshape(1, D_MODEL))
    state0 = jnp.zeros((B, 2, HEAD_DIM, D_MODEL), F32)
    state_c = _rwkv_scan(u_ctx, state0, params, rotary=False, emit_y=False)[-1]
    y_f, y_b, bonus_f, bonus_b, _ = _rwkv_scan(u, state_c, params, rotary=True, emit_y=True)

    return _final_stage(x, y_a, y_f, y_b, bonus_f, bonus_b, u, mod3, rw_gn_g.reshape(1, D_MODEL),
                        rw_gn_b.reshape(1, D_MODEL), w_branch_a.astype(BF16), w_branch_b.astype(BF16),
                        w_out.astype(BF16), ln_g.reshape(1, D_MODEL), ln_b.reshape(1, D_MODEL),
                        tm=min(T, 512))


def kernel(x, c, ctx, c_ctx, w_ada, b_ada, w_in, na_rpb, rw_mu, rw_w0, rw_w_up, rw_a0, rw_a_up, rw_k_k, rw_k_a, rw_r_k, rw_gn_g, rw_gn_b, w_branch_a, w_branch_b, w_out, ln_g, ln_b):
    assert w_ada.shape[0] == DEPTH
    return _layer(x, c, ctx, c_ctx, w_ada[0], b_ada[0], w_in[0], na_rpb[0], rw_mu[0], rw_w0[0],
                  rw_w_up[0], rw_a0[0], rw_a_up[0], rw_k_k[0], rw_k_a[0], rw_r_k[0], rw_gn_g[0],
                  rw_gn_b[0], w_branch_a[0], w_branch_b[0], w_out[0], ln_g[0], ln_b[0])
```

```python
import functools

import numpy as np
import jax
import jax.numpy as jnp
from jax import lax
from jax.experimental import pallas as pl
from jax.experimental.pallas import tpu as pltpu

F32 = jnp.float32
BF16 = jnp.bfloat16
HIGHEST = lax.Precision.HIGHEST

D_MODEL = 1024
GRID_W = 64
HEADS = 16
HEAD_DIM = 64
NA_MAX_ROWS = 8
NA_COLS = 16
LORA = 64
DEPTH = 1
ROPE_THETA = 10000.0
LN_EPS = 1e-5
GN_EPS = 64e-5
ALPHA = (2 * DEPTH) ** 0.25

LANES = 128
GROUP = 256
HEADS_PER_GROUP = GROUP // HEAD_DIM
N_GROUPS = D_MODEL // GROUP
CHUNK = 64
RWKV_BLOCK_CHUNKS = 8

COL_RKV = 0
COL_NA = 3 * D_MODEL
COL_ZB = 7 * D_MODEL
N_MAIN = 8 * D_MODEL
COL_MG = N_MAIN
COL_LORA = N_MAIN + 2 * D_MODEL
N_PROJ = COL_LORA + 4 * LORA
N_PAD = 21 * 512
VMEM_LIMIT = 56 * 1024 * 1024


def _cparams(sem):
    return pltpu.CompilerParams(dimension_semantics=sem, vmem_limit_bytes=VMEM_LIMIT)


def _silu(x):
    return x * jax.nn.sigmoid(x)


def _mod_kernel(c_ref, w_ref, b_ref, o_ref):
    s = _silu(c_ref[...])
    o_ref[...] = jnp.dot(s, w_ref[...], preferred_element_type=F32, precision=HIGHEST) + b_ref[...]


def _modulation(cond, w_ada, b_ada):
    rows = cond.shape[0]
    return pl.pallas_call(
        _mod_kernel,
        out_shape=jax.ShapeDtypeStruct((rows, 3 * D_MODEL), F32),
        grid=(3,),
        in_specs=[pl.BlockSpec((rows, D_MODEL), lambda j: (0, 0)),
                  pl.BlockSpec((D_MODEL, D_MODEL), lambda j: (0, j)),
                  pl.BlockSpec((1, D_MODEL), lambda j: (0, j))],
        out_specs=pl.BlockSpec((rows, D_MODEL), lambda j: (0, j)),
        compiler_params=_cparams(("arbitrary",)),
        name="ada_modulation",
    )(cond, w_ada, b_ada.reshape(1, 3 * D_MODEL))


def _inproj_kernel(x_ref, sh_ref, sc_ref, w_ref, o_ref, h_scr):
    @pl.when(pl.program_id(2) == 0)
    def _():
        x = x_ref[...]
        mu = jnp.mean(x, axis=-1, keepdims=True)
        xc = x - mu
        var = jnp.mean(xc * xc, axis=-1, keepdims=True)
        y = xc * lax.rsqrt(var + LN_EPS)
        h_scr[...] = (y * (1.0 + sc_ref[...]) + sh_ref[...]).astype(BF16)

    o_ref[...] = jnp.dot(h_scr[...], w_ref[...], preferred_element_type=F32).astype(BF16)


def _in_projection(x, mod3, w_bf, row_of_batch, tm, tn):
    B, T, _ = x.shape
    return pl.pallas_call(
        _inproj_kernel,
        out_shape=jax.ShapeDtypeStruct((B, T, N_PAD), BF16),
        grid=(B, T // tm, N_PAD // tn),
        in_specs=[pl.BlockSpec((None, tm, D_MODEL), lambda b, i, j: (b, i, 0)),
                  pl.BlockSpec((None, 1, D_MODEL), lambda b, i, j: (row_of_batch(b), 0, 0)),
                  pl.BlockSpec((None, 1, D_MODEL), lambda b, i, j: (row_of_batch(b), 0, 1)),
                  pl.BlockSpec((D_MODEL, tn), lambda b, i, j: (0, j))],
        out_specs=pl.BlockSpec((None, tm, tn), lambda b, i, j: (b, i, j)),
        scratch_shapes=[pltpu.VMEM((tm, D_MODEL), BF16)],
        compiler_params=_cparams(("arbitrary", "arbitrary", "arbitrary")),
        name="ln_mod_in_proj",
    )(x, mod3, mod3, w_bf)


NA_HEADS_PER_STEP = GROUP // HEAD_DIM
NA_ROWS_PER_ITER = 8


def _na_kernel(q_ref, k_ref, v_ref, z_ref, kc_ref, vc_ref, bias_ref, o_ref, *, rows):
    kh = min(NA_MAX_ROWS, rows)
    band = kh * GRID_W
    nh = NA_HEADS_PER_STEP
    lane = lax.broadcasted_iota(jnp.int32, (1, GROUP), 1)
    head_lanes = [(lane // HEAD_DIM) == h for h in range(nh)]
    kc = kc_ref[...]
    vc = vc_ref[...]
    nt = (((1,), (1,)), ((), ()))

    n_rows = NA_ROWS_PER_ITER if rows % NA_ROWS_PER_ITER == 0 else 1
    stack = nh * GRID_W

    def rows_body(it, carry):
        qs, s, vb, q0s = [], [], [], []
        for r in range(n_rows):
            i = it * n_rows + r
            rs = jnp.clip(i - kh // 2, 0, rows - kh)
            q0 = pl.multiple_of(i * GRID_W, GRID_W)
            k0 = pl.multiple_of(rs * GRID_W, GRID_W)
            q = q_ref[pl.ds(q0, GRID_W), :] * jnp.asarray(HEAD_DIM ** -0.5, BF16)
            q_st = jnp.concatenate([jnp.where(head_lanes[h], q, jnp.zeros_like(q)) for h in range(nh)], axis=0)
            s.append(lax.dot_general(q_st, k_ref[pl.ds(k0, band), :], nt, preferred_element_type=F32)
                     + bias_ref[i - rs])
            vb.append(v_ref[pl.ds(k0, band), :])
            qs.append(q_st)
            q0s.append(q0)
        sc = lax.dot_general(jnp.concatenate(qs, axis=0), kc, nt, preferred_element_type=F32)
        p, pcs, denom = [], [], []
        for r in range(n_rows):
            sc_r = sc[r * stack:(r + 1) * stack]
            m = jnp.maximum(jnp.max(s[r], axis=-1, keepdims=True), jnp.max(sc_r, axis=-1, keepdims=True))
            p_r = jnp.exp(s[r] - m)
            pc_r = jnp.exp(sc_r - m)
            denom.append(jnp.sum(p_r, axis=-1, keepdims=True) + jnp.sum(pc_r, axis=-1, keepdims=True))
            p.append(p_r.astype(BF16))
            pcs.append(pc_r.astype(BF16))
        o_ctx = jnp.dot(jnp.concatenate(pcs, axis=0), vc, preferred_element_type=F32)
        for r in range(n_rows):
            o_all = (jnp.dot(p[r], vb[r], preferred_element_type=F32)
                     + o_ctx[r * stack:(r + 1) * stack]) / denom[r]
            o = o_all[(nh - 1) * GRID_W:]
            for h in range(nh - 2, -1, -1):
                o = jnp.where(head_lanes[h], o_all[h * GRID_W:(h + 1) * GRID_W], o)
            z = z_ref[pl.ds(q0s[r], GRID_W), :].astype(F32)
            o_ref[pl.ds(q0s[r], GRID_W), :] = (o * _silu(z)).astype(BF16)
        return carry

    lax.fori_loop(0, rows // n_rows, rows_body, 0)


def _na_bias_table(rpb, rows):
    kh = min(NA_MAX_ROWS, rows)
    nh = NA_HEADS_PER_STEP
    n_dj = 2 * NA_COLS - 1
    cols = np.arange(GRID_W)
    cstart = np.clip(cols - NA_COLS // 2, 0, GRID_W - NA_COLS)
    col_mask = (cols[None, :] >= cstart[:, None]) & (cols[None, :] < cstart[:, None] + NA_COLS)
    dj = np.clip(cols[None, :] - cols[:, None] + NA_COLS - 1, 0, n_dj - 1)
    onehot = (np.arange(n_dj)[:, None] == dj.reshape(1, -1)).astype(np.float32)
    exp = jnp.dot(rpb.reshape(-1, n_dj), jnp.asarray(onehot), precision=HIGHEST)
    exp = exp.reshape(HEADS, 2 * NA_MAX_ROWS - 1, GRID_W, GRID_W)
    exp = jnp.where(col_mask[None, None], exp, -jnp.inf)
    lo = NA_MAX_ROWS - 1
    tab = jnp.stack([exp[:, lo - o: lo - o + kh] for o in range(kh)], axis=1)
    tab = tab.reshape(HEADS // nh, nh, kh, kh, GRID_W, GRID_W)
    return tab.transpose(0, 2, 1, 4, 3, 5).reshape(HEADS // nh, kh, nh * GRID_W, kh * GRID_W)


def _neighbourhood_attention(u, u_ctx, bias_tab):
    B, T, _ = u.shape
    L = u_ctx.shape[1]
    rows = T // GRID_W
    kh = min(NA_MAX_ROWS, rows)
    blk = D_MODEL // GROUP
    first = COL_NA // GROUP

    def col(part):
        return pl.BlockSpec((None, T, GROUP), lambda b, g: (b, 0, first + part * blk + g))

    def col_ctx(part):
        return pl.BlockSpec((None, L, GROUP), lambda b, g: (b, 0, first + part * blk + g))

    return pl.pallas_call(
        functools.partial(_na_kernel, rows=rows),
        out_shape=jax.ShapeDtypeStruct((B, T, D_MODEL), BF16),
        grid=(B, blk),
        in_specs=[col(0), col(1), col(2), col(3), col_ctx(1), col_ctx(2),
                  pl.BlockSpec((None, kh, NA_HEADS_PER_STEP * GRID_W, kh * GRID_W), lambda b, g: (g, 0, 0, 0))],
        out_specs=pl.BlockSpec((None, T, GROUP), lambda b, g: (b, 0, g)),
        compiler_params=_cparams(("arbitrary", "arbitrary")),
        name="neighbourhood_attention",
    )(u, u, u, u, u_ctx, u_ctx, bias_tab)


def _block_diag_masks():
    r = lax.broadcasted_iota(jnp.int32, (GROUP, GROUP), 0)
    c = lax.broadcasted_iota(jnp.int32, (GROUP, GROUP), 1)
    return (r // HEAD_DIM) == (c // HEAD_DIM)


def _head_sum(x, ones_bd):
    rows = x.shape[0]
    xb = x.astype(BF16)
    stacked = jnp.concatenate([xb[:, g * GROUP:(g + 1) * GROUP] for g in range(N_GROUPS)], axis=0)
    sums = jnp.dot(stacked, ones_bd, preferred_element_type=F32)
    return jnp.concatenate([sums[g * rows:(g + 1) * rows] for g in range(N_GROUPS)], axis=1)


def _rwkv_prepare(d, local, cidx, n_chunks, refs, prm, ones_bd, rotary, ops_ref, pc_ref):
    rkv_ref, prev_ref, next_ref, lo_ref, rope_ref, bonus_ref = refs
    mu_ref, w0_ref, wup_ref, a0_ref, aup_ref, kk_ref, ka_ref, rk_ref = prm
    C = CHUNK
    halo = prev_ref.shape[0]
    block = rkv_ref.shape[0]
    rows = pl.ds(pl.multiple_of(local * C, C), C)
    rows_before = pl.ds(pl.multiple_of(jnp.maximum(local * C - halo, 0), halo), halo)
    rows_after = pl.ds(pl.multiple_of(jnp.minimum(local * C + C, block - halo), halo), halo)
    chunk_id = jnp.full((halo, D_MODEL), local, jnp.int32)
    at_block_start = chunk_id == 0
    at_block_end = chunk_id == block // C - 1

    def store(name, value):
        ops_ref[d * len(OPERANDS) + OPERANDS.index(name)] = value.astype(BF16)

    tt = lax.broadcasted_iota(jnp.int32, (C, C + 2 * halo), 0)
    ss = lax.broadcasted_iota(jnp.int32, (C, C + 2 * halo), 1) - halo
    first = jnp.where(cidx > 0, -1, 0)
    last = jnp.where(cidx < n_chunks - 1, C, C - 1)
    nb = (jnp.abs(ss - tt) == 1) & (ss >= first) & (ss <= last)
    nb = jnp.where(nb, 1.0, 0.0).astype(BF16)

    def shifted(i):
        lanes = slice(i * D_MODEL, (i + 1) * D_MODEL)
        x = rkv_ref[rows, lanes]
        before = jnp.where(at_block_start, prev_ref[:, lanes], rkv_ref[rows_before, lanes])
        after = jnp.where(at_block_end, next_ref[:, lanes], rkv_ref[rows_after, lanes])
        both = jnp.dot(nb, jnp.concatenate([before, x, after], axis=0), preferred_element_type=F32)
        mu = mu_ref[i:i + 1, :]
        return (1.0 - mu) * x.astype(F32) + (0.5 * mu) * both

    if rotary:
        lane = lax.broadcasted_iota(jnp.int32, (1, D_MODEL), 1)
        low = (lane % 32) < 16
        cos_t = jnp.tile(rope_ref[rows, :LANES], (1, D_MODEL // LANES))
        sin_t = jnp.tile(rope_ref[rows, LANES:], (1, D_MODEL // LANES))

    def rope(x):
        if not rotary:
            return x
        partner = jnp.where(low, pltpu.roll(x, D_MODEL - 16, axis=1), pltpu.roll(x, 16, axis=1))
        return x * cos_t + partner * sin_t

    lo = lo_ref[rows, :]
    lw = w0_ref[d:d + 1, :] + jnp.dot(jnp.tanh(lo[:, :2 * LORA].astype(F32)).astype(BF16), wup_ref[d],
                                     preferred_element_type=F32)
    ld = (-np.exp(-0.5)) * jax.nn.sigmoid(lw)
    yield

    tt = lax.broadcasted_iota(jnp.int32, (C, C), 0)
    ss = lax.broadcasted_iota(jnp.int32, (C, C), 1)
    tri = jnp.where((ss <= tt) if d == 0 else (ss >= tt), 1.0, 0.0).astype(BF16)
    ld_hi = ld.astype(BF16)
    ld_lo = (ld - ld_hi.astype(F32)).astype(BF16)
    cl = (jnp.dot(tri, ld_hi, preferred_element_type=F32) + jnp.dot(tri, ld_lo, preferred_element_type=F32))
    cl_tot = cl[C - 1:C, :] if d == 0 else cl[0:1, :]
    p_c = jnp.exp(cl_tot)
    pc_ref[d] = jnp.broadcast_to(p_c, pc_ref.shape[1:])
    yield
    a = jax.nn.sigmoid(a0_ref[d:d + 1, :] + jnp.dot(lo[:, 2 * LORA:], aup_ref[d], preferred_element_type=F32))
    yield

    k_s = rope(shifted(1))
    yield
    kk = k_s * kk_ref[...]
    kk = kk * jnp.minimum(lax.rsqrt(_head_sum(kk * kk, ones_bd)), 1e12)
    store("a", -kk * jnp.exp(cl - ld))
    yield
    e_neg = jnp.exp(-cl)
    b_t = kk * a * e_neg
    store("b", b_t)
    store("bh", b_t * p_c)
    yield
    k_dir = k_s * (1.0 + (a - 1.0) * ka_ref[...])
    k_t = k_dir * e_neg
    store("k", k_t)
    store("kh", k_t * p_c)
    yield

    r_s = rope(shifted(0))
    yield
    store("r", r_s * jnp.exp(cl))
    bonus_scale = _head_sum(r_s * rk_ref[...] * k_dir, ones_bd)
    yield
    v_s = shifted(2)
    store("v", v_s)
    bonus_ref[rows, :] = (bonus_scale * v_s).astype(bonus_ref.dtype)


OPERANDS = ("a", "r", "b", "k", "bh", "kh", "v")


def _interleave(*gens):
    live = list(gens)
    while live:
        for gen in list(live):
            try:
                next(gen)
            except StopIteration:
                live.remove(gen)


def _rwkv_chunk_matmuls(ops_ref, pc_ref, h_scr, yf_ref, yb_ref, local_f, local_b, same_head, emit_y):
    C = CHUNK
    n_ops = len(OPERANDS)
    n_g = 2 * N_GROUPS
    dirs = [g // N_GROUPS for g in range(n_g)]

    def lanes_of(g):
        return slice((g % N_GROUPS) * GROUP, (g % N_GROUPS + 1) * GROUP)

    def grp(name):
        i = OPERANDS.index(name)
        return [ops_ref[dirs[g] * n_ops + i, :, lanes_of(g)] for g in range(n_g)]

    t_i = lax.broadcasted_iota(jnp.int32, (C, GROUP), 0)
    s_i = lax.broadcasted_iota(jnp.int32, (C, GROUP), 1) % C
    before = [s_i < t_i, s_i > t_i]
    upto = [s_i <= t_i, s_i >= t_i]
    eye = jnp.where(t_i == s_i, 1.0, 0.0).astype(F32)
    nt = (((1,), (1,)), ((), ()))
    tn = (((0,), (0,)), ((), ()))

    def bd(x):
        return jnp.where(same_head, jnp.tile(x, (HEADS_PER_GROUP, 1)), jnp.zeros((), x.dtype))

    def hdot(a_side, x):
        return jnp.dot(a_side, bd(x), preferred_element_type=F32)

    def masked(m, keep):
        return jnp.where(keep, m, 0.0).astype(BF16)

    at, rt, bt = grp("a"), grp("r"), grp("b")
    lhs = [jnp.concatenate([at[g], rt[g]], axis=0) for g in range(n_g)] if emit_y else at
    a_ab, a_rb, a_ak, a_rk = [], [], [], []
    for g in range(n_g):
        m_b = lax.dot_general(lhs[g], bd(bt[g]), nt, preferred_element_type=F32)
        a_ab.append(masked(m_b[:C], before[dirs[g]]))
        if emit_y:
            a_rb.append(masked(m_b[C:], upto[dirs[g]]))
    yield
    kt = grp("k")
    for g in range(n_g):
        m_k = lax.dot_general(lhs[g], bd(kt[g]), nt, preferred_element_type=F32)
        a_ak.append(masked(m_k[:C], before[dirs[g]]))
        if emit_y:
            a_rk.append(masked(m_k[C:], upto[dirs[g]]))
    yield
    vv = grp("v")
    w_1, y_acc = [], []
    for g in range(n_g):
        if emit_y:
            w_v = hdot(jnp.concatenate([a_ak[g], a_rk[g]], axis=0), vv[g])
            y_acc.append(w_v[C:])
        else:
            w_v = hdot(a_ak[g], vv[g])
        w_1.append(w_v[:C].astype(BF16))
    yield

    pw = [hdot(a_ab[g], a_ab[g]).astype(BF16) for g in range(n_g)]
    inv = [eye + a_ab[g].astype(F32) for g in range(n_g)]
    yield
    n_factors = C.bit_length() - 1
    for j in range(1, n_factors):
        last = j == n_factors - 1
        for g in range(n_g):
            inv_bf = inv[g].astype(BF16)
            prod = hdot(inv_bf if last else jnp.concatenate([inv_bf, pw[g]], axis=0), pw[g])
            inv[g] = inv[g] + prod[:C]
            if not last:
                pw[g] = prod[C:].astype(BF16)
        yield
    inv = [inv[g].astype(BF16) for g in range(n_g)]

    a_p = [hdot(inv[g], at[g]).astype(BF16) for g in range(n_g)]
    yield
    u0 = [hdot(inv[g], w_1[g]) for g in range(n_g)]
    yield

    u, h_dec = [], []
    for g in range(n_g):
        decay_rows = (eye * pc_ref[dirs[g], 0:1, lanes_of(g)]).astype(BF16)
        stack = jnp.concatenate([a_p[g]] + ([rt[g]] if emit_y else []) + [decay_rows], axis=0)
        s1 = hdot(stack, h_scr[:, g * GROUP:(g + 1) * GROUP].astype(BF16))
        u.append((s1[:C] + u0[g]).astype(BF16))
        if emit_y:
            y_acc[g] = y_acc[g] + s1[C:2 * C]
        h_dec.append(s1[-C:])
    yield
    if emit_y:
        out_rows = [pl.ds(pl.multiple_of(local * C, C), C) for local in (local_f, local_b)]
        for g in range(n_g):
            y_ref = yf_ref if dirs[g] == 0 else yb_ref
            y_ref[out_rows[dirs[g]], lanes_of(g)] = (y_acc[g] + hdot(a_rb[g], u[g])).astype(y_ref.dtype)
    yield

    bh, kh = grp("bh"), grp("kh")
    lane_head = lax.broadcasted_iota(jnp.int32, (HEAD_DIM, GROUP), 1) // HEAD_DIM
    for g in range(n_g):
        full = lax.dot_general(jnp.concatenate([bh[g], kh[g]], axis=0),
                               jnp.concatenate([u[g], vv[g]], axis=0), tn, preferred_element_type=F32)
        upd = full[(HEADS_PER_GROUP - 1) * HEAD_DIM:]
        for j in range(HEADS_PER_GROUP - 2, -1, -1):
            upd = jnp.where(lane_head == j, full[j * HEAD_DIM:(j + 1) * HEAD_DIM], upd)
        h_scr[:, g * GROUP:(g + 1) * GROUP] = h_dec[g] + upd


def _rwkv_kernel(*refs, n_blocks, rotary, emit_y):
    per_dir = 5
    in_f, in_b = refs[:per_dir], refs[per_dir:2 * per_dir]
    prm = refs[2 * per_dir:2 * per_dir + 8]
    s0_ref = refs[2 * per_dir + 8]
    yf_ref, yb_ref, bonf_ref, bonb_ref, sout_ref, h_scr, ops_scr, pc_scr = refs[2 * per_dir + 9:]
    j = pl.program_id(1)
    per_block = in_f[0].shape[0] // CHUNK
    n_chunks = n_blocks * per_block

    @pl.when(j == 0)
    def _():
        h_scr[:, :D_MODEL] = s0_ref[0]
        h_scr[:, D_MODEL:] = s0_ref[1]

    if not emit_y:
        yf_ref[...] = jnp.zeros(yf_ref.shape, yf_ref.dtype)
        yb_ref[...] = jnp.zeros(yb_ref.shape, yb_ref.dtype)

    same_head = _block_diag_masks()
    ones_bd = jnp.where(same_head, 1.0, 0.0).astype(BF16)

    def local(d, s):
        return s if d == 0 else per_block - 1 - s

    def prepare(d, s, buf):
        block = j if d == 0 else n_blocks - 1 - j
        ins, bonus_ref = (in_f, bonf_ref) if d == 0 else (in_b, bonb_ref)
        return _rwkv_prepare(d, local(d, s), block * per_block + local(d, s), n_chunks, ins + (bonus_ref,), prm,
                             ones_bd, rotary, ops_scr.at[buf], pc_scr.at[buf])

    def matmuls(s, buf):
        return _rwkv_chunk_matmuls(ops_scr.at[buf], pc_scr.at[buf], h_scr, yf_ref, yb_ref,
                                   local(0, s), local(1, s), same_head, emit_y)

    _interleave(prepare(0, 0, 0), prepare(1, 0, 0))

    def chunk_pair(i, carry):
        s = 2 * i
        _interleave(matmuls(s, 0), prepare(0, s + 1, 1), prepare(1, s + 1, 1))
        ahead = jnp.minimum(s + 2, per_block - 1)
        _interleave(matmuls(s + 1, 1), prepare(0, ahead, 0), prepare(1, ahead, 0))
        return carry

    lax.fori_loop(0, per_block // 2, chunk_pair, 0)

    @pl.when(j == n_blocks - 1)
    def _():
        sout_ref[0] = h_scr[:, :D_MODEL]
        sout_ref[1] = h_scr[:, D_MODEL:]


def _rope_tables(T):
    half = HEAD_DIM // 2
    n_freq = half // 2
    t = np.arange(T)[:, None]
    lane = np.arange(LANES)[None, :]
    inv_freq = ROPE_THETA ** (-(np.arange(n_freq, dtype=np.float32)) / n_freq)
    pos = np.where((lane % HEAD_DIM) < half, t // GRID_W, t % GRID_W).astype(np.float32)
    ang = (pos * inv_freq[lane % n_freq].astype(np.float32)).astype(np.float32)
    sign = np.where((lane % half) < n_freq, -1.0, 1.0)
    return jnp.asarray(np.cos(ang), F32), jnp.asarray(np.sin(ang) * sign, F32)


def _rwkv_scan(u, state0, params, rotary, emit_y):
    B, T, _ = u.shape
    block = min(T, RWKV_BLOCK_CHUNKS * CHUNK)
    n_blocks = T // block
    assert T % block == 0 and (block // CHUNK) % 2 == 0
    halo = 16
    per = block // halo

    def block_of(d, j):
        return j if d == 0 else n_blocks - 1 - j

    def specs(d):
        rkv = COL_RKV // (3 * D_MODEL)
        main = pl.BlockSpec((None, block, 3 * D_MODEL), lambda b, j: (b, block_of(d, j), rkv))
        prev = pl.BlockSpec((None, halo, 3 * D_MODEL),
                            lambda b, j: (b, jnp.maximum(block_of(d, j) * per - 1, 0), rkv))
        nxt = pl.BlockSpec((None, halo, 3 * D_MODEL),
                           lambda b, j: (b, jnp.minimum((block_of(d, j) + 1) * per, T // halo - 1), rkv))
        lora = pl.BlockSpec((None, block, 4 * LORA), lambda b, j: (b, block_of(d, j), COL_LORA // (4 * LORA)))
        tab = pl.BlockSpec((block, 2 * LANES), lambda b, j: (block_of(d, j), 0))
        return [main, prev, nxt, lora, tab]

    def whole(shape):
        return pl.BlockSpec(shape, lambda b, c: (0,) * len(shape))

    rope_tab = jnp.concatenate(_rope_tables(T), axis=1)
    state = pl.BlockSpec((None, 2, HEAD_DIM, D_MODEL), lambda b, c: (b, 0, 0, 0))
    param_specs = [whole((3, D_MODEL)), whole((2, D_MODEL)), whole((2, 2 * LORA, D_MODEL)),
                   whole((2, D_MODEL)), whole((2, 2 * LORA, D_MODEL)),
                   whole((1, D_MODEL)), whole((1, D_MODEL)), whole((1, D_MODEL))]
    seq = jax.ShapeDtypeStruct((B, T, D_MODEL), BF16)
    out_f = pl.BlockSpec((None, block, D_MODEL), lambda b, j: (b, block_of(0, j), 0))
    out_b = pl.BlockSpec((None, block, D_MODEL), lambda b, j: (b, block_of(1, j), 0))
    seq_in = (u,) * 4 + (rope_tab,)
    return pl.pallas_call(
        functools.partial(_rwkv_kernel, n_blocks=n_blocks, rotary=rotary, emit_y=emit_y),
        out_shape=(seq, seq, seq, seq, jax.ShapeDtypeStruct((B, 2, HEAD_DIM, D_MODEL), F32)),
        grid=(B, n_blocks),
        in_specs=specs(0) + specs(1) + param_specs + [state],
        out_specs=(out_f, out_b, out_f, out_b, state),
        scratch_shapes=[pltpu.VMEM((HEAD_DIM, 2 * D_MODEL), F32),
                        pltpu.VMEM((2, 2 * len(OPERANDS), CHUNK, D_MODEL), BF16),
                        pltpu.VMEM((2, 2, 8, D_MODEL), F32)],
        compiler_params=_cparams(("arbitrary", "arbitrary")),
        name="rwkv7_chunk_scan_rot" if rotary else "rwkv7_chunk_scan_ctx",
    )(*seq_in, *seq_in, *params, state0)


def _final_kernel(x_ref, ya_ref, yf_ref, yb_ref, bf_ref, bb_ref, z_ref, ga_ref, gb_ref, gate_ref,
                  gng_ref, gnb_ref, wa_ref, wb_ref, wo_ref, lng_ref, lnb_ref, o_ref):
    same_head = _block_diag_masks()
    ones_bd = jnp.where(same_head, 1.0, 0.0).astype(BF16)
    y = yf_ref[...].astype(F32) + yb_ref[...].astype(F32)
    mu = _head_sum(y, ones_bd) * (1.0 / HEAD_DIM)
    yc = y - mu
    var = _head_sum(yc * yc, ones_bd) * (1.0 / HEAD_DIM)
    yn = yc * lax.rsqrt(var + GN_EPS) * gng_ref[...] + gnb_ref[...]
    yn = yn + bf_ref[...].astype(F32) + bb_ref[...].astype(F32)
    y_b = (yn * _silu(z_ref[...].astype(F32))).astype(BF16)
    p_a = jnp.dot(ya_ref[...], wa_ref[...], preferred_element_type=F32)
    p_b = jnp.dot(y_b, wb_ref[...], preferred_element_type=F32)
    merged = (jax.nn.sigmoid(ga_ref[...].astype(F32)) * p_a
              + jax.nn.sigmoid(gb_ref[...].astype(F32)) * p_b)
    out = jnp.dot(merged.astype(BF16), wo_ref[...], preferred_element_type=F32)
    t = ALPHA * x_ref[...] + gate_ref[...] * out
    m = jnp.mean(t, axis=-1, keepdims=True)
    tc = t - m
    v = jnp.mean(tc * tc, axis=-1, keepdims=True)
    o_ref[...] = tc * lax.rsqrt(v + LN_EPS) * lng_ref[...] + lnb_ref[...]


def _final_stage(x, y_a, y_f, y_b, bonus_f, bonus_b, u, mod3, gn_g, gn_b, wa, wb, wo, ln_g, ln_b, tm):
    B, T, _ = x.shape

    def tok():
        return pl.BlockSpec((None, tm, D_MODEL), lambda b, i: (b, i, 0))

    def ucol(blk):
        return pl.BlockSpec((None, tm, D_MODEL), lambda b, i: (b, i, blk))

    def vec():
        return pl.BlockSpec((1, D_MODEL), lambda b, i: (0, 0))

    def mat():
        return pl.BlockSpec((D_MODEL, D_MODEL), lambda b, i: (0, 0))

    return pl.pallas_call(
        _final_kernel,
        out_shape=jax.ShapeDtypeStruct((B, T, D_MODEL), F32),
        grid=(B, T // tm),
        in_specs=[tok(), tok(), tok(), tok(), tok(), tok(), ucol(7), ucol(8), ucol(9),
                  pl.BlockSpec((None, 1, D_MODEL), lambda b, i: (b, 0, 2)),
                  vec(), vec(), mat(), mat(), mat(), vec(), vec()],
        out_specs=tok(),
        compiler_params=_cparams(("arbitrary", "arbitrary")),
        name="readout_merge_out_proj",
    )(x, y_a, y_f, y_b, bonus_f, bonus_b, u, u, u, mod3, gn_g, gn_b, wa, wb, wo, ln_g, ln_b)


def _layer(x, c, ctx, c_ctx, w_ada, b_ada, w_in, na_rpb, rw_mu, rw_w0, rw_w_up, rw_a0, rw_a_up,
           rw_k_k, rw_k_a, rw_r_k, rw_gn_g, rw_gn_b, w_branch_a, w_branch_b, w_out, ln_g, ln_b):
    B, T, _ = x.shape
    L = ctx.shape[1]
    rows = T // GRID_W

    n_cond = -(-(B + 1) // 8) * 8
    cond = jnp.zeros((n_cond, D_MODEL), F32).at[:B].set(c).at[B].set(c_ctx)
    mod3 = _modulation(cond, w_ada, b_ada).reshape(n_cond, 1, 3 * D_MODEL)

    w_bf = jnp.concatenate(
        [w_in[:, 4 * D_MODEL:7 * D_MODEL], w_in[:, :4 * D_MODEL], w_in[:, 7 * D_MODEL:N_MAIN],
         w_in[:, N_MAIN + 4 * LORA:], w_in[:, N_MAIN:N_MAIN + 4 * LORA],
         jnp.zeros((D_MODEL, N_PAD - N_PROJ), w_in.dtype)], axis=1).astype(BF16)
    u = _in_projection(x, mod3, w_bf, lambda b: b, tm=min(T, 2048), tn=1536)
    u_ctx = _in_projection(ctx, mod3, w_bf, lambda b: B, tm=L, tn=1536)

    y_a = _neighbourhood_attention(u, u_ctx, _na_bias_table(na_rpb, rows))

    def lora_pad(w_up):
        out = jnp.zeros((2, 2 * LORA, D_MODEL), F32)
        for d in range(2):
            out = out.at[d, d * LORA:(d + 1) * LORA].set(w_up[d])
        return out.astype(BF16)

    params = (rw_mu, rw_w0, lora_pad(rw_w_up), rw_a0, lora_pad(rw_a_up), rw_k_k.reshape(1, D_MODEL),
              rw_k_a.reshape(1, D_MODEL), rw_r_k.reshape(1, D_MODEL))
    state0 = jnp.zeros((B, 2, HEAD_DIM, D_MODEL), F32)
    state_c = _rwkv_scan(u_ctx, state0, params, rotary=False, emit_y=False)[-1]
    y_f, y_b, bonus_f, bonus_b, _ = _rwkv_scan(u, state_c, params, rotary=True, emit_y=True)

    return _final_stage(x, y_a, y_f, y_b, bonus_f, bonus_b, u, mod3, rw_gn_g.reshape(1, D_MODEL),
                        rw_gn_b.reshape(1, D_MODEL), w_branch_a.astype(BF16), w_branch_b.astype(BF16),
                        w_out.astype(BF16), ln_g.reshape(1, D_MODEL), ln_b.reshape(1, D_MODEL),
                        tm=min(T, 512))


def kernel(x, c, ctx, c_ctx, w_ada, b_ada, w_in, na_rpb, rw_mu, rw_w0, rw_w_up, rw_a0, rw_a_up, rw_k_k, rw_k_a, rw_r_k, rw_gn_g, rw_gn_b, w_branch_a, w_branch_b, w_out, ln_g, ln_b):
    assert w_ada.shape[0] == DEPTH
    return _layer(x, c, ctx, c_ctx, w_ada[0], b_ada[0], w_in[0], na_rpb[0], rw_mu[0], rw_w0[0],
                  rw_w_up[0], rw_a0[0], rw_a_up[0], rw_k_k[0], rw_k_a[0], rw_r_k[0], rw_gn_g[0],
                  rw_gn_b[0], w_branch_a[0], w_branch_b[0], w_out[0], ln_g[0], ln_b[0])
```

```python
import functools
import itertools

import numpy as np
import jax
import jax.numpy as jnp
from jax import lax
from jax.experimental import pallas as pl
from jax.experimental.pallas import tpu as pltpu

F32 = jnp.float32
BF16 = jnp.bfloat16
HIGHEST = lax.Precision.HIGHEST

D_MODEL = 1024
GRID_W = 64
HEADS = 16
HEAD_DIM = 64
NA_MAX_ROWS = 8
NA_COLS = 16
LORA = 64
DEPTH = 1
ROPE_THETA = 10000.0
LN_EPS = 1e-5
GN_EPS = 64e-5
ALPHA = (2 * DEPTH) ** 0.25

LANES = 128
GROUP = 256
HEADS_PER_GROUP = GROUP // HEAD_DIM
N_GROUPS = D_MODEL // GROUP
CHUNK = 64
RWKV_BLOCK_CHUNKS = 8

COL_RKV = 0
COL_NA = 3 * D_MODEL
COL_ZB = 7 * D_MODEL
N_MAIN = 8 * D_MODEL
COL_MG = N_MAIN
COL_LORA = N_MAIN + 2 * D_MODEL
N_PROJ = COL_LORA + 4 * LORA
N_PAD = 21 * 512
VMEM_LIMIT = 56 * 1024 * 1024


def _cparams(sem):
    return pltpu.CompilerParams(dimension_semantics=sem, vmem_limit_bytes=VMEM_LIMIT)


def _silu(x):
    return x * jax.nn.sigmoid(x)


def _mod_kernel(c_ref, w_ref, b_ref, o_ref):
    s = _silu(c_ref[...])
    o_ref[...] = jnp.dot(s, w_ref[...], preferred_element_type=F32, precision=HIGHEST) + b_ref[...]


def _modulation(cond, w_ada, b_ada):
    rows = cond.shape[0]
    return pl.pallas_call(
        _mod_kernel,
        out_shape=jax.ShapeDtypeStruct((rows, 3 * D_MODEL), F32),
        grid=(3,),
        in_specs=[pl.BlockSpec((rows, D_MODEL), lambda j: (0, 0)),
                  pl.BlockSpec((D_MODEL, D_MODEL), lambda j: (0, j)),
                  pl.BlockSpec((1, D_MODEL), lambda j: (0, j))],
        out_specs=pl.BlockSpec((rows, D_MODEL), lambda j: (0, j)),
        compiler_params=_cparams(("arbitrary",)),
        name="ada_modulation",
    )(cond, w_ada, b_ada.reshape(1, 3 * D_MODEL))


def _inproj_kernel(x_ref, sh_ref, sc_ref, w_ref, o_ref, h_scr):
    @pl.when(pl.program_id(2) == 0)
    def _():
        x = x_ref[...]
        mu = jnp.mean(x, axis=-1, keepdims=True)
        xc = x - mu
        var = jnp.mean(xc * xc, axis=-1, keepdims=True)
        y = xc * lax.rsqrt(var + LN_EPS)
        h_scr[...] = (y * (1.0 + sc_ref[...]) + sh_ref[...]).astype(BF16)

    o_ref[...] = jnp.dot(h_scr[...], w_ref[...], preferred_element_type=F32).astype(BF16)


def _in_projection(x, mod3, w_bf, row_of_batch, tm, tn):
    B, T, _ = x.shape
    return pl.pallas_call(
        _inproj_kernel,
        out_shape=jax.ShapeDtypeStruct((B, T, N_PAD), BF16),
        grid=(B, T // tm, N_PAD // tn),
        in_specs=[pl.BlockSpec((None, tm, D_MODEL), lambda b, i, j: (b, i, 0)),
                  pl.BlockSpec((None, 1, D_MODEL), lambda b, i, j: (row_of_batch(b), 0, 0)),
                  pl.BlockSpec((None, 1, D_MODEL), lambda b, i, j: (row_of_batch(b), 0, 1)),
                  pl.BlockSpec((D_MODEL, tn), lambda b, i, j: (0, j))],
        out_specs=pl.BlockSpec((None, tm, tn), lambda b, i, j: (b, i, j)),
        scratch_shapes=[pltpu.VMEM((tm, D_MODEL), BF16)],
        compiler_params=_cparams(("arbitrary", "arbitrary", "arbitrary")),
        name="ln_mod_in_proj",
    )(x, mod3, mod3, w_bf)


NA_HEADS_PER_STEP = GROUP // HEAD_DIM
NA_ROWS_PER_ITER = 8


def _na_kernel(q_ref, k_ref, v_ref, z_ref, kc_ref, vc_ref, bias_ref, o_ref, *, rows):
    kh = min(NA_MAX_ROWS, rows)
    band = kh * GRID_W
    nh = NA_HEADS_PER_STEP
    lane = lax.broadcasted_iota(jnp.int32, (1, GROUP), 1)
    head_lanes = [(lane // HEAD_DIM) == h for h in range(nh)]
    kc = kc_ref[...]
    vc = vc_ref[...]
    nt = (((1,), (1,)), ((), ()))

    n_rows = NA_ROWS_PER_ITER if rows % NA_ROWS_PER_ITER == 0 else 1
    stack = nh * GRID_W

    def rows_body(it, carry):
        qs, s, vb, q0s = [], [], [], []
        for r in range(n_rows):
            i = it * n_rows + r
            rs = jnp.clip(i - kh // 2, 0, rows - kh)
            q0 = pl.multiple_of(i * GRID_W, GRID_W)
            k0 = pl.multiple_of(rs * GRID_W, GRID_W)
            q = q_ref[pl.ds(q0, GRID_W), :] * jnp.asarray(HEAD_DIM ** -0.5, BF16)
            q_st = jnp.concatenate([jnp.where(head_lanes[h], q, jnp.zeros_like(q)) for h in range(nh)], axis=0)
            s.append(lax.dot_general(q_st, k_ref[pl.ds(k0, band), :], nt, preferred_element_type=F32)
                     + bias_ref[:, i - rs].reshape(stack, band))
            vb.append(v_ref[pl.ds(k0, band), :])
            qs.append(q_st)
            q0s.append(q0)
        sc = lax.dot_general(jnp.concatenate(qs, axis=0), kc, nt, preferred_element_type=F32)
        p, pcs, denom = [], [], []
        for r in range(n_rows):
            sc_r = sc[r * stack:(r + 1) * stack]
            m = jnp.maximum(jnp.max(s[r], axis=-1, keepdims=True), jnp.max(sc_r, axis=-1, keepdims=True))
            p_r = jnp.exp(s[r] - m)
            pc_r = jnp.exp(sc_r - m)
            denom.append(jnp.sum(p_r, axis=-1, keepdims=True) + jnp.sum(pc_r, axis=-1, keepdims=True))
            p.append(p_r.astype(BF16))
            pcs.append(pc_r.astype(BF16))
        o_ctx = jnp.dot(jnp.concatenate(pcs, axis=0), vc, preferred_element_type=F32)
        for r in range(n_rows):
            o_all = (jnp.dot(p[r], vb[r], preferred_element_type=F32)
                     + o_ctx[r * stack:(r + 1) * stack]) / denom[r]
            o = o_all[(nh - 1) * GRID_W:]
            for h in range(nh - 2, -1, -1):
                o = jnp.where(head_lanes[h], o_all[h * GRID_W:(h + 1) * GRID_W], o)
            z = z_ref[pl.ds(q0s[r], GRID_W), :].astype(F32)
            o_ref[pl.ds(q0s[r], GRID_W), :] = (o * _silu(z)).astype(BF16)
        return carry

    lax.fori_loop(0, rows // n_rows, rows_body, 0)


def _na_bias_table(rpb, rows):
    kh = min(NA_MAX_ROWS, rows)
    nh = NA_HEADS_PER_STEP
    n_dj = 2 * NA_COLS - 1
    cols = np.arange(GRID_W)
    cstart = np.clip(cols - NA_COLS // 2, 0, GRID_W - NA_COLS)
    col_mask = (cols[None, :] >= cstart[:, None]) & (cols[None, :] < cstart[:, None] + NA_COLS)
    dj = np.clip(cols[None, :] - cols[:, None] + NA_COLS - 1, 0, n_dj - 1)
    onehot = (np.arange(n_dj)[:, None] == dj.reshape(1, -1)).astype(np.float32)
    exp = jnp.dot(rpb.reshape(-1, n_dj), jnp.asarray(onehot), precision=HIGHEST)
    exp = exp.reshape(HEADS, 2 * NA_MAX_ROWS - 1, GRID_W, GRID_W)
    exp = jnp.where(col_mask[None, None], exp, -jnp.inf).transpose(0, 2, 1, 3)
    lo = NA_MAX_ROWS - 1
    tab = jnp.stack([exp[:, :, lo - o: lo - o + kh].reshape(HEADS, GRID_W, kh * GRID_W) for o in range(kh)],
                    axis=1)
    return tab.reshape(HEADS // nh, nh, kh, GRID_W, kh * GRID_W)


def _neighbourhood_attention(u, u_ctx, bias_tab):
    B, T, _ = u.shape
    L = u_ctx.shape[1]
    rows = T // GRID_W
    kh = min(NA_MAX_ROWS, rows)
    blk = D_MODEL // GROUP
    first = COL_NA // GROUP

    def col(part):
        return pl.BlockSpec((None, T, GROUP), lambda b, g: (b, 0, first + part * blk + g))

    def col_ctx(part):
        return pl.BlockSpec((None, L, GROUP), lambda b, g: (b, 0, first + part * blk + g))

    return pl.pallas_call(
        functools.partial(_na_kernel, rows=rows),
        out_shape=jax.ShapeDtypeStruct((B, T, D_MODEL), BF16),
        grid=(B, blk),
        in_specs=[col(0), col(1), col(2), col(3), col_ctx(1), col_ctx(2),
                  pl.BlockSpec((None, NA_HEADS_PER_STEP, kh, GRID_W, kh * GRID_W), lambda b, g: (g, 0, 0, 0, 0))],
        out_specs=pl.BlockSpec((None, T, GROUP), lambda b, g: (b, 0, g)),
        compiler_params=_cparams(("arbitrary", "arbitrary")),
        name="neighbourhood_attention",
    )(u, u, u, u, u_ctx, u_ctx, bias_tab)


def _block_diag_masks():
    r = lax.broadcasted_iota(jnp.int32, (GROUP, GROUP), 0)
    c = lax.broadcasted_iota(jnp.int32, (GROUP, GROUP), 1)
    return (r // HEAD_DIM) == (c // HEAD_DIM)


def _head_sum(x, ones_bd):
    rows = x.shape[0]
    xb = x.astype(BF16)
    stacked = jnp.concatenate([xb[:, g * GROUP:(g + 1) * GROUP] for g in range(N_GROUPS)], axis=0)
    sums = jnp.dot(stacked, ones_bd, preferred_element_type=F32)
    return jnp.concatenate([sums[g * rows:(g + 1) * rows] for g in range(N_GROUPS)], axis=1)


def _rwkv_prepare(d, local, cidx, n_chunks, refs, prm, ones_bd, rotary, ops_ref, pc_ref):
    rkv_ref, prev_ref, next_ref, lo_ref, rope_ref, bonus_ref = refs
    mu_ref, w0_ref, wup_ref, a0_ref, aup_ref, kk_ref, ka_ref, rk_ref = prm
    C = CHUNK
    halo = prev_ref.shape[0]
    block = rkv_ref.shape[0]
    rows = pl.ds(pl.multiple_of(local * C, C), C)
    rows_before = pl.ds(pl.multiple_of(jnp.maximum(local * C - halo, 0), halo), halo)
    rows_after = pl.ds(pl.multiple_of(jnp.minimum(local * C + C, block - halo), halo), halo)
    chunk_id = jnp.full((halo, D_MODEL), local, jnp.int32)
    at_block_start = chunk_id == 0
    at_block_end = chunk_id == block // C - 1

    def store(name, value):
        ops_ref[d * len(OPERANDS) + OPERANDS.index(name)] = value.astype(BF16)

    tt = lax.broadcasted_iota(jnp.int32, (C, C + 2 * halo), 0)
    ss = lax.broadcasted_iota(jnp.int32, (C, C + 2 * halo), 1) - halo
    first = jnp.where(cidx > 0, -1, 0)
    last = jnp.where(cidx < n_chunks - 1, C, C - 1)
    nb = (jnp.abs(ss - tt) == 1) & (ss >= first) & (ss <= last)
    nb = jnp.where(nb, 1.0, 0.0).astype(BF16)

    def shifted(i):
        lanes = slice(i * D_MODEL, (i + 1) * D_MODEL)
        x = rkv_ref[rows, lanes]
        before = jnp.where(at_block_start, prev_ref[:, lanes], rkv_ref[rows_before, lanes])
        after = jnp.where(at_block_end, next_ref[:, lanes], rkv_ref[rows_after, lanes])
        both = jnp.dot(nb, jnp.concatenate([before, x, after], axis=0), preferred_element_type=F32)
        mu = mu_ref[i:i + 1, :]
        return (1.0 - mu) * x.astype(F32) + (0.5 * mu) * both

    r_s = shifted(0)
    yield
    k_s = shifted(1)
    yield
    v_s = shifted(2)
    store("v", v_s)
    yield

    if rotary:
        lane = lax.broadcasted_iota(jnp.int32, (1, D_MODEL), 1)
        low = (lane % 32) < 16
        cos_t = jnp.tile(rope_ref[rows, :LANES], (1, D_MODEL // LANES))
        sin_t = jnp.tile(rope_ref[rows, LANES:], (1, D_MODEL // LANES))

        def rope(x):
            partner = jnp.where(low, pltpu.roll(x, D_MODEL - 16, axis=1), pltpu.roll(x, 16, axis=1))
            return x * cos_t + partner * sin_t

        r_s = rope(r_s)
        yield
        k_s = rope(k_s)
        yield

    lo = lo_ref[rows, :]
    lw = w0_ref[d:d + 1, :] + jnp.dot(jnp.tanh(lo[:, :2 * LORA].astype(F32)).astype(BF16), wup_ref[d],
                                     preferred_element_type=F32)
    ld = (-np.exp(-0.5)) * jax.nn.sigmoid(lw)
    yield
    a = jax.nn.sigmoid(a0_ref[d:d + 1, :] + jnp.dot(lo[:, 2 * LORA:], aup_ref[d], preferred_element_type=F32))
    yield

    kk = k_s * kk_ref[...]
    kk = kk * jnp.minimum(lax.rsqrt(_head_sum(kk * kk, ones_bd)), 1e12)
    yield
    k_dir = k_s * (1.0 + (a - 1.0) * ka_ref[...])
    b_vec = kk * a
    bonus_ref[rows, :] = (_head_sum(r_s * rk_ref[...] * k_dir, ones_bd) * v_s).astype(bonus_ref.dtype)
    yield

    tt = lax.broadcasted_iota(jnp.int32, (C, C), 0)
    ss = lax.broadcasted_iota(jnp.int32, (C, C), 1)
    tri = jnp.where((ss <= tt) if d == 0 else (ss >= tt), 1.0, 0.0).astype(BF16)
    ld_hi = ld.astype(BF16)
    ld_lo = (ld - ld_hi.astype(F32)).astype(BF16)
    cl = (jnp.dot(tri, ld_hi, preferred_element_type=F32) + jnp.dot(tri, ld_lo, preferred_element_type=F32))
    cl_tot = cl[C - 1:C, :] if d == 0 else cl[0:1, :]
    yield
    e_neg = jnp.exp(-cl)
    p_c = jnp.exp(cl_tot)
    pc_ref[d] = jnp.broadcast_to(p_c, pc_ref.shape[1:])
    b_t = b_vec * e_neg
    store("b", b_t)
    store("bh", b_t * p_c)
    yield
    k_t = k_dir * e_neg
    store("k", k_t)
    store("kh", k_t * p_c)
    yield
    store("a", -kk * jnp.exp(cl - ld))
    store("r", r_s * jnp.exp(cl))


OPERANDS = ("a", "r", "b", "k", "bh", "kh", "v")


def _interleave(*gens):
    live = list(gens)
    while live:
        for gen in list(live):
            try:
                next(gen)
            except StopIteration:
                live.remove(gen)


def _rwkv_chunk_matmuls(ops_ref, pc_ref, h_scr, yf_ref, yb_ref, local_f, local_b, same_head, emit_y):
    C = CHUNK
    n_ops = len(OPERANDS)
    n_g = 2 * N_GROUPS
    dirs = [g // N_GROUPS for g in range(n_g)]

    def lanes_of(g):
        return slice((g % N_GROUPS) * GROUP, (g % N_GROUPS + 1) * GROUP)

    def grp(name):
        i = OPERANDS.index(name)
        return [ops_ref[dirs[g] * n_ops + i, :, lanes_of(g)] for g in range(n_g)]

    t_i = lax.broadcasted_iota(jnp.int32, (C, GROUP), 0)
    s_i = lax.broadcasted_iota(jnp.int32, (C, GROUP), 1) % C
    before = [s_i < t_i, s_i > t_i]
    upto = [s_i <= t_i, s_i >= t_i]
    eye = jnp.where(t_i == s_i, 1.0, 0.0).astype(F32)
    nt = (((1,), (1,)), ((), ()))
    tn = (((0,), (0,)), ((), ()))

    def bd(x):
        return jnp.where(same_head, jnp.tile(x, (HEADS_PER_GROUP, 1)), jnp.zeros((), x.dtype))

    def hdot(a_side, x):
        return jnp.dot(a_side, bd(x), preferred_element_type=F32)

    def masked(m, keep):
        return jnp.where(keep, m, 0.0).astype(BF16)

    at, rt, bt = grp("a"), grp("r"), grp("b")
    lhs = [jnp.concatenate([at[g], rt[g]], axis=0) for g in range(n_g)] if emit_y else at
    a_ab, a_rb, a_ak, a_rk = [], [], [], []
    for g in range(n_g):
        m_b = lax.dot_general(lhs[g], bd(bt[g]), nt, preferred_element_type=F32)
        a_ab.append(masked(m_b[:C], before[dirs[g]]))
        if emit_y:
            a_rb.append(masked(m_b[C:], upto[dirs[g]]))
    yield
    kt = grp("k")
    for g in range(n_g):
        m_k = lax.dot_general(lhs[g], bd(kt[g]), nt, preferred_element_type=F32)
        a_ak.append(masked(m_k[:C], before[dirs[g]]))
        if emit_y:
            a_rk.append(masked(m_k[C:], upto[dirs[g]]))
    yield
    vv = grp("v")
    w_1, y_acc = [], []
    for g in range(n_g):
        if emit_y:
            w_v = hdot(jnp.concatenate([a_ak[g], a_rk[g]], axis=0), vv[g])
            y_acc.append(w_v[C:])
        else:
            w_v = hdot(a_ak[g], vv[g])
        w_1.append(w_v[:C].astype(BF16))
    yield

    pw = [hdot(a_ab[g], a_ab[g]).astype(BF16) for g in range(n_g)]
    inv = [eye + a_ab[g].astype(F32) for g in range(n_g)]
    yield
    n_factors = C.bit_length() - 1
    for j in range(1, n_factors):
        last = j == n_factors - 1
        for g in range(n_g):
            inv_bf = inv[g].astype(BF16)
            prod = hdot(inv_bf if last else jnp.concatenate([inv_bf, pw[g]], axis=0), pw[g])
            inv[g] = inv[g] + prod[:C]
            if not last:
                pw[g] = prod[C:].astype(BF16)
        yield
    inv = [inv[g].astype(BF16) for g in range(n_g)]

    a_p = [hdot(inv[g], at[g]).astype(BF16) for g in range(n_g)]
    yield
    u0 = [hdot(inv[g], w_1[g]) for g in range(n_g)]
    yield

    u, h_dec = [], []
    for g in range(n_g):
        decay_rows = (eye * pc_ref[dirs[g], 0:1, lanes_of(g)]).astype(BF16)
        stack = jnp.concatenate([a_p[g]] + ([rt[g]] if emit_y else []) + [decay_rows], axis=0)
        s1 = hdot(stack, h_scr[:, g * GROUP:(g + 1) * GROUP].astype(BF16))
        u.append((s1[:C] + u0[g]).astype(BF16))
        if emit_y:
            y_acc[g] = y_acc[g] + s1[C:2 * C]
        h_dec.append(s1[-C:])
    yield
    if emit_y:
        out_rows = [pl.ds(pl.multiple_of(local * C, C), C) for local in (local_f, local_b)]
        for g in range(n_g):
            y_ref = yf_ref if dirs[g] == 0 else yb_ref
            y_ref[out_rows[dirs[g]], lanes_of(g)] = (y_acc[g] + hdot(a_rb[g], u[g])).astype(y_ref.dtype)
    yield

    bh, kh = grp("bh"), grp("kh")
    lane_head = lax.broadcasted_iota(jnp.int32, (HEAD_DIM, GROUP), 1) // HEAD_DIM
    for g in range(n_g):
        full = lax.dot_general(jnp.concatenate([bh[g], kh[g]], axis=0),
                               jnp.concatenate([u[g], vv[g]], axis=0), tn, preferred_element_type=F32)
        upd = full[(HEADS_PER_GROUP - 1) * HEAD_DIM:]
        for j in range(HEADS_PER_GROUP - 2, -1, -1):
            upd = jnp.where(lane_head == j, full[j * HEAD_DIM:(j + 1) * HEAD_DIM], upd)
        h_scr[:, g * GROUP:(g + 1) * GROUP] = h_dec[g] + upd


def _rwkv_kernel(*refs, n_blocks, rotary, emit_y):
    per_dir = 5
    in_f, in_b = refs[:per_dir], refs[per_dir:2 * per_dir]
    prm = refs[2 * per_dir:2 * per_dir + 8]
    s0_ref = refs[2 * per_dir + 8]
    yf_ref, yb_ref, bonf_ref, bonb_ref, sout_ref, h_scr, ops_scr, pc_scr = refs[2 * per_dir + 9:]
    j = pl.program_id(1)
    per_block = in_f[0].shape[0] // CHUNK
    n_chunks = n_blocks * per_block

    @pl.when(j == 0)
    def _():
        h_scr[:, :D_MODEL] = s0_ref[0]
        h_scr[:, D_MODEL:] = s0_ref[1]

    if not emit_y:
        yf_ref[...] = jnp.zeros(yf_ref.shape, yf_ref.dtype)
        yb_ref[...] = jnp.zeros(yb_ref.shape, yb_ref.dtype)

    same_head = _block_diag_masks()
    ones_bd = jnp.where(same_head, 1.0, 0.0).astype(BF16)

    def local(d, s):
        return s if d == 0 else per_block - 1 - s

    def prepare(d, s, buf):
        block = j if d == 0 else n_blocks - 1 - j
        ins, bonus_ref = (in_f, bonf_ref) if d == 0 else (in_b, bonb_ref)
        return _rwkv_prepare(d, local(d, s), block * per_block + local(d, s), n_chunks, ins + (bonus_ref,), prm,
                             ones_bd, rotary, ops_scr.at[buf], pc_scr.at[buf])

    def matmuls(s, buf):
        return _rwkv_chunk_matmuls(ops_scr.at[buf], pc_scr.at[buf], h_scr, yf_ref, yb_ref,
                                   local(0, s), local(1, s), same_head, emit_y)

    _interleave(prepare(0, 0, 0), prepare(1, 0, 0))

    def chunk_pair(i, carry):
        s = 2 * i
        _interleave(matmuls(s, 0), itertools.chain(prepare(0, s + 1, 1), prepare(1, s + 1, 1)))
        ahead = jnp.minimum(s + 2, per_block - 1)
        _interleave(matmuls(s + 1, 1), itertools.chain(prepare(0, ahead, 0), prepare(1, ahead, 0)))
        return carry

    lax.fori_loop(0, per_block // 2, chunk_pair, 0)

    @pl.when(j == n_blocks - 1)
    def _():
        sout_ref[0] = h_scr[:, :D_MODEL]
        sout_ref[1] = h_scr[:, D_MODEL:]


def _rope_tables(T):
    half = HEAD_DIM // 2
    n_freq = half // 2
    t = np.arange(T)[:, None]
    lane = np.arange(LANES)[None, :]
    inv_freq = ROPE_THETA ** (-(np.arange(n_freq, dtype=np.float32)) / n_freq)
    pos = np.where((lane % HEAD_DIM) < half, t // GRID_W, t % GRID_W).astype(np.float32)
    ang = (pos * inv_freq[lane % n_freq].astype(np.float32)).astype(np.float32)
    sign = np.where((lane % half) < n_freq, -1.0, 1.0)
    return jnp.asarray(np.cos(ang), F32), jnp.asarray(np.sin(ang) * sign, F32)


def _rwkv_scan(u, state0, params, rotary, emit_y):
    B, T, _ = u.shape
    block = min(T, RWKV_BLOCK_CHUNKS * CHUNK)
    n_blocks = T // block
    assert T % block == 0 and (block // CHUNK) % 2 == 0
    halo = 16
    per = block // halo

    def block_of(d, j):
        return j if d == 0 else n_blocks - 1 - j

    def specs(d):
        rkv = COL_RKV // (3 * D_MODEL)
        main = pl.BlockSpec((None, block, 3 * D_MODEL), lambda b, j: (b, block_of(d, j), rkv))
        prev = pl.BlockSpec((None, halo, 3 * D_MODEL),
                            lambda b, j: (b, jnp.maximum(block_of(d, j) * per - 1, 0), rkv))
        nxt = pl.BlockSpec((None, halo, 3 * D_MODEL),
                           lambda b, j: (b, jnp.minimum((block_of(d, j) + 1) * per, T // halo - 1), rkv))
        lora = pl.BlockSpec((None, block, 4 * LORA), lambda b, j: (b, block_of(d, j), COL_LORA // (4 * LORA)))
        tab = pl.BlockSpec((block, 2 * LANES), lambda b, j: (block_of(d, j), 0))
        return [main, prev, nxt, lora, tab]

    def whole(shape):
        return pl.BlockSpec(shape, lambda b, c: (0,) * len(shape))

    rope_tab = jnp.concatenate(_rope_tables(T), axis=1)
    state = pl.BlockSpec((None, 2, HEAD_DIM, D_MODEL), lambda b, c: (b, 0, 0, 0))
    param_specs = [whole((3, D_MODEL)), whole((2, D_MODEL)), whole((2, 2 * LORA, D_MODEL)),
                   whole((2, D_MODEL)), whole((2, 2 * LORA, D_MODEL)),
                   whole((1, D_MODEL)), whole((1, D_MODEL)), whole((1, D_MODEL))]
    seq = jax.ShapeDtypeStruct((B, T, D_MODEL), BF16)
    out_f = pl.BlockSpec((None, block, D_MODEL), lambda b, j: (b, block_of(0, j), 0))
    out_b = pl.BlockSpec((None, block, D_MODEL), lambda b, j: (b, block_of(1, j), 0))
    seq_in = (u,) * 4 + (rope_tab,)
    return pl.pallas_call(
        functools.partial(_rwkv_kernel, n_blocks=n_blocks, rotary=rotary, emit_y=emit_y),
        out_shape=(seq, seq, seq, seq, jax.ShapeDtypeStruct((B, 2, HEAD_DIM, D_MODEL), F32)),
        grid=(B, n_blocks),
        in_specs=specs(0) + specs(1) + param_specs + [state],
        out_specs=(out_f, out_b, out_f, out_b, state),
        scratch_shapes=[pltpu.VMEM((HEAD_DIM, 2 * D_MODEL), F32),
                        pltpu.VMEM((2, 2 * len(OPERANDS), CHUNK, D_MODEL), BF16),
                        pltpu.VMEM((2, 2, 8, D_MODEL), F32)],
        compiler_params=_cparams(("arbitrary", "arbitrary")),
        name="rwkv7_chunk_scan_rot" if rotary else "rwkv7_chunk_scan_ctx",
    )(*seq_in, *seq_in, *params, state0)


def _final_kernel(x_ref, ya_ref, yf_ref, yb_ref, bf_ref, bb_ref, z_ref, ga_ref, gb_ref, gate_ref,
                  gng_ref, gnb_ref, wa_ref, wb_ref, wo_ref, lng_ref, lnb_ref, o_ref):
    same_head = _block_diag_masks()
    ones_bd = jnp.where(same_head, 1.0, 0.0).astype(BF16)
    y = yf_ref[...].astype(F32) + yb_ref[...].astype(F32)
    mu = _head_sum(y, ones_bd) * (1.0 / HEAD_DIM)
    yc = y - mu
    var = _head_sum(yc * yc, ones_bd) * (1.0 / HEAD_DIM)
    yn = yc * lax.rsqrt(var + GN_EPS) * gng_ref[...] + gnb_ref[...]
    yn = yn + bf_ref[...].astype(F32) + bb_ref[...].astype(F32)
    y_b = (yn * _silu(z_ref[...].astype(F32))).astype(BF16)
    p_a = jnp.dot(ya_ref[...], wa_ref[...], preferred_element_type=F32)
    p_b = jnp.dot(y_b, wb_ref[...], preferred_element_type=F32)
    merged = (jax.nn.sigmoid(ga_ref[...].astype(F32)) * p_a
              + jax.nn.sigmoid(gb_ref[...].astype(F32)) * p_b)
    out = jnp.dot(merged.astype(BF16), wo_ref[...], preferred_element_type=F32)
    t = ALPHA * x_ref[...] + gate_ref[...] * out
    m = jnp.mean(t, axis=-1, keepdims=True)
    tc = t - m
    v = jnp.mean(tc * tc, axis=-1, keepdims=True)
    o_ref[...] = tc * lax.rsqrt(v + LN_EPS) * lng_ref[...] + lnb_ref[...]


def _final_stage(x, y_a, y_f, y_b, bonus_f, bonus_b, u, mod3, gn_g, gn_b, wa, wb, wo, ln_g, ln_b, tm):
    B, T, _ = x.shape

    def tok():
        return pl.BlockSpec((None, tm, D_MODEL), lambda b, i: (b, i, 0))

    def ucol(blk):
        return pl.BlockSpec((None, tm, D_MODEL), lambda b, i: (b, i, blk))

    def vec():
        return pl.BlockSpec((1, D_MODEL), lambda b, i: (0, 0))

    def mat():
        return pl.BlockSpec((D_MODEL, D_MODEL), lambda b, i: (0, 0))

    return pl.pallas_call(
        _final_kernel,
        out_shape=jax.ShapeDtypeStruct((B, T, D_MODEL), F32),
        grid=(B, T // tm),
        in_specs=[tok(), tok(), tok(), tok(), tok(), tok(), ucol(7), ucol(8), ucol(9),
                  pl.BlockSpec((None, 1, D_MODEL), lambda b, i: (b, 0, 2)),
                  vec(), vec(), mat(), mat(), mat(), vec(), vec()],
        out_specs=tok(),
        compiler_params=_cparams(("arbitrary", "arbitrary")),
        name="readout_merge_out_proj",
    )(x, y_a, y_f, y_b, bonus_f, bonus_b, u, u, u, mod3, gn_g, gn_b, wa, wb, wo, ln_g, ln_b)


def _layer(x, c, ctx, c_ctx, w_ada, b_ada, w_in, na_rpb, rw_mu, rw_w0, rw_w_up, rw_a0, rw_a_up,
           rw_k_k, rw_k_a, rw_r_k, rw_gn_g, rw_gn_b, w_branch_a, w_branch_b, w_out, ln_g, ln_b):
    B, T, _ = x.shape
    L = ctx.shape[1]
    rows = T // GRID_W

    n_cond = -(-(B + 1) // 8) * 8
    cond = jnp.zeros((n_cond, D_MODEL), F32).at[:B].set(c).at[B].set(c_ctx)
    mod3 = _modulation(cond, w_ada, b_ada).reshape(n_cond, 1, 3 * D_MODEL)

    w_bf = jnp.concatenate(
        [w_in[:, 4 * D_MODEL:7 * D_MODEL], w_in[:, :4 * D_MODEL], w_in[:, 7 * D_MODEL:N_MAIN],
         w_in[:, N_MAIN + 4 * LORA:], w_in[:, N_MAIN:N_MAIN + 4 * LORA],
         jnp.zeros((D_MODEL, N_PAD - N_PROJ), w_in.dtype)], axis=1).astype(BF16)
    u = _in_projection(x, mod3, w_bf, lambda b: b, tm=min(T, 2048), tn=1536)
    u_ctx = _in_projection(ctx, mod3, w_bf, lambda b: B, tm=L, tn=1536)

    y_a = _neighbourhood_attention(u, u_ctx, _na_bias_table(na_rpb, rows))

    def lora_pad(w_up):
        out = jnp.zeros((2, 2 * LORA, D_MODEL), F32)
        for d in range(2):
            out = out.at[d, d * LORA:(d + 1) * LORA].set(w_up[d])
        return out.astype(BF16)

    params = (rw_mu, rw_w0, lora_pad(rw_w_up), rw_a0, lora_pad(rw_a_up), rw_k_k.reshape(1, D_MODEL),
              rw_k_a.reshape(1, D_MODEL), rw_r_k.reshape(1, D_MODEL))
    state0 = jnp.zeros((B, 2, HEAD_DIM, D_MODEL), F32)
    state_c = _rwkv_scan(u_ctx, state0, params, rotary=False, emit_y=False)[-1]
    y_f, y_b, bonus_f, bonus_b, _ = _rwkv_scan(u, state_c, params, rotary=True, emit_y=True)

    return _final_stage(x, y_a, y_f, y_b, bonus_f, bonus_b, u, mod3, rw_gn_g.reshape(1, D_MODEL),
                        rw_gn_b.reshape(1, D_MODEL), w_branch_a.astype(BF16), w_branch_b.astype(BF16),
                        w_out.astype(BF16), ln_g.reshape(1, D_MODEL), ln_b.reshape(1, D_MODEL),
                        tm=min(T, 512))


def kernel(x, c, ctx, c_ctx, w_ada, b_ada, w_in, na_rpb, rw_mu, rw_w0, rw_w_up, rw_a0, rw_a_up, rw_k_k, rw_k_a, rw_r_k, rw_gn_g, rw_gn_b, w_branch_a, w_branch_b, w_out, ln_g, ln_b):
    assert w_ada.shape[0] == DEPTH
    return _layer(x, c, ctx, c_ctx, w_ada[0], b_ada[0], w_in[0], na_rpb[0], rw_mu[0], rw_w0[0],
                  rw_w_up[0], rw_a0[0], rw_a_up[0], rw_k_k[0], rw_k_a[0], rw_r_k[0], rw_gn_g[0],
                  rw_gn_b[0], w_branch_a[0], w_branch_b[0], w_out[0], ln_g[0], ln_b[0])
```

```python
import functools
import itertools

import numpy as np
import jax
import jax.numpy as jnp
from jax import lax
from jax.experimental import pallas as pl
from jax.experimental.pallas import tpu as pltpu

F32 = jnp.float32
BF16 = jnp.bfloat16
HIGHEST = lax.Precision.HIGHEST

D_MODEL = 1024
GRID_W = 64
HEADS = 16
HEAD_DIM = 64
NA_MAX_ROWS = 8
NA_COLS = 16
LORA = 64
DEPTH = 1
ROPE_THETA = 10000.0
LN_EPS = 1e-5
GN_EPS = 64e-5
ALPHA = (2 * DEPTH) ** 0.25

LANES = 128
GROUP = 256
HEADS_PER_GROUP = GROUP // HEAD_DIM
N_GROUPS = D_MODEL // GROUP
CHUNK = 64
RWKV_BLOCK_CHUNKS = 8

COL_RKV = 0
COL_NA = 3 * D_MODEL
COL_ZB = 7 * D_MODEL
N_MAIN = 8 * D_MODEL
COL_MG = N_MAIN
COL_LORA = N_MAIN + 2 * D_MODEL
N_PROJ = COL_LORA + 4 * LORA
N_PAD = 21 * 512
VMEM_LIMIT = 56 * 1024 * 1024


def _cparams(sem):
    return pltpu.CompilerParams(dimension_semantics=sem, vmem_limit_bytes=VMEM_LIMIT)


def _silu(x):
    return x * jax.nn.sigmoid(x)


def _mod_kernel(c_ref, w_ref, b_ref, o_ref):
    s = _silu(c_ref[...])
    o_ref[...] = jnp.dot(s, w_ref[...], preferred_element_type=F32, precision=HIGHEST) + b_ref[...]


def _modulation(cond, w_ada, b_ada):
    rows = cond.shape[0]
    return pl.pallas_call(
        _mod_kernel,
        out_shape=jax.ShapeDtypeStruct((rows, 3 * D_MODEL), F32),
        grid=(3,),
        in_specs=[pl.BlockSpec((rows, D_MODEL), lambda j: (0, 0)),
                  pl.BlockSpec((D_MODEL, D_MODEL), lambda j: (0, j)),
                  pl.BlockSpec((1, D_MODEL), lambda j: (0, j))],
        out_specs=pl.BlockSpec((rows, D_MODEL), lambda j: (0, j)),
        compiler_params=_cparams(("arbitrary",)),
        name="ada_modulation",
    )(cond, w_ada, b_ada.reshape(1, 3 * D_MODEL))


def _inproj_kernel(x_ref, sh_ref, sc_ref, w_ref, o_ref, h_scr):
    @pl.when(pl.program_id(2) == 0)
    def _():
        x = x_ref[...]
        mu = jnp.mean(x, axis=-1, keepdims=True)
        xc = x - mu
        var = jnp.mean(xc * xc, axis=-1, keepdims=True)
        y = xc * lax.rsqrt(var + LN_EPS)
        h_scr[...] = (y * (1.0 + sc_ref[...]) + sh_ref[...]).astype(BF16)

    o_ref[...] = jnp.dot(h_scr[...], w_ref[...], preferred_element_type=F32).astype(BF16)


def _in_projection(x, mod3, w_bf, row_of_batch, tm, tn):
    B, T, _ = x.shape
    return pl.pallas_call(
        _inproj_kernel,
        out_shape=jax.ShapeDtypeStruct((B, T, N_PAD), BF16),
        grid=(B, T // tm, N_PAD // tn),
        in_specs=[pl.BlockSpec((None, tm, D_MODEL), lambda b, i, j: (b, i, 0)),
                  pl.BlockSpec((None, 1, D_MODEL), lambda b, i, j: (row_of_batch(b), 0, 0)),
                  pl.BlockSpec((None, 1, D_MODEL), lambda b, i, j: (row_of_batch(b), 0, 1)),
                  pl.BlockSpec((D_MODEL, tn), lambda b, i, j: (0, j))],
        out_specs=pl.BlockSpec((None, tm, tn), lambda b, i, j: (b, i, j)),
        scratch_shapes=[pltpu.VMEM((tm, D_MODEL), BF16)],
        compiler_params=_cparams(("arbitrary", "arbitrary", "arbitrary")),
        name="ln_mod_in_proj",
    )(x, mod3, mod3, w_bf)


NA_HEADS_PER_STEP = GROUP // HEAD_DIM
NA_ROWS_PER_ITER = 8
NA_ROW_GROUPS = 2


def _na_kernel(q_ref, k_ref, v_ref, z_ref, kc_ref, vc_ref, bias_ref, o_ref, *, rows):
    kh = min(NA_MAX_ROWS, rows)
    band = kh * GRID_W
    nh = NA_HEADS_PER_STEP
    lane = lax.broadcasted_iota(jnp.int32, (1, GROUP), 1)
    head_lanes = [(lane // HEAD_DIM) == h for h in range(nh)]
    kc = kc_ref[...]
    vc = vc_ref[...]
    nt = (((1,), (1,)), ((), ()))

    n_rows = NA_ROWS_PER_ITER if rows % NA_ROWS_PER_ITER == 0 else 1
    stack = nh * GRID_W

    def row_group(first_row, count, delay):
        for _ in range(delay):
            yield
        qs, s, vb, q0s = [], [], [], []
        for r in range(count):
            i = first_row + r
            rs = jnp.clip(i - kh // 2, 0, rows - kh)
            q0 = pl.multiple_of(i * GRID_W, GRID_W)
            k0 = pl.multiple_of(rs * GRID_W, GRID_W)
            q = q_ref[pl.ds(q0, GRID_W), :] * jnp.asarray(HEAD_DIM ** -0.5, BF16)
            q_st = jnp.concatenate([jnp.where(head_lanes[h], q, jnp.zeros_like(q)) for h in range(nh)], axis=0)
            s.append(lax.dot_general(q_st, k_ref[pl.ds(k0, band), :], nt, preferred_element_type=F32)
                     + bias_ref[:, i - rs].reshape(stack, band))
            vb.append(v_ref[pl.ds(k0, band), :])
            qs.append(q_st)
            q0s.append(q0)
        sc = lax.dot_general(jnp.concatenate(qs, axis=0), kc, nt, preferred_element_type=F32)
        yield
        p, pcs, denom = [], [], []
        for r in range(count):
            sc_r = sc[r * stack:(r + 1) * stack]
            m = jnp.maximum(jnp.max(s[r], axis=-1, keepdims=True), jnp.max(sc_r, axis=-1, keepdims=True))
            p_r = jnp.exp(s[r] - m)
            pc_r = jnp.exp(sc_r - m)
            denom.append(jnp.sum(p_r, axis=-1, keepdims=True) + jnp.sum(pc_r, axis=-1, keepdims=True))
            p.append(p_r.astype(BF16))
            pcs.append(pc_r.astype(BF16))
        yield
        o_ctx = jnp.dot(jnp.concatenate(pcs, axis=0), vc, preferred_element_type=F32)
        for r in range(count):
            o_all = (jnp.dot(p[r], vb[r], preferred_element_type=F32)
                     + o_ctx[r * stack:(r + 1) * stack]) / denom[r]
            o = o_all[(nh - 1) * GRID_W:]
            for h in range(nh - 2, -1, -1):
                o = jnp.where(head_lanes[h], o_all[h * GRID_W:(h + 1) * GRID_W], o)
            z = z_ref[pl.ds(q0s[r], GRID_W), :].astype(F32)
            o_ref[pl.ds(q0s[r], GRID_W), :] = (o * _silu(z)).astype(BF16)

    def rows_body(it, carry):
        per_group = n_rows // NA_ROW_GROUPS if n_rows % NA_ROW_GROUPS == 0 else n_rows
        groups = n_rows // per_group
        _interleave(*[row_group(it * n_rows + g * per_group, per_group, g) for g in range(groups)])
        return carry

    lax.fori_loop(0, rows // n_rows, rows_body, 0)


def _na_bias_table(rpb, rows):
    kh = min(NA_MAX_ROWS, rows)
    nh = NA_HEADS_PER_STEP
    n_dj = 2 * NA_COLS - 1
    cols = np.arange(GRID_W)
    cstart = np.clip(cols - NA_COLS // 2, 0, GRID_W - NA_COLS)
    col_mask = (cols[None, :] >= cstart[:, None]) & (cols[None, :] < cstart[:, None] + NA_COLS)
    dj = np.clip(cols[None, :] - cols[:, None] + NA_COLS - 1, 0, n_dj - 1)
    onehot = (np.arange(n_dj)[:, None] == dj.reshape(1, -1)).astype(np.float32)
    exp = jnp.dot(rpb.reshape(-1, n_dj), jnp.asarray(onehot), precision=HIGHEST)
    exp = exp.reshape(HEADS, 2 * NA_MAX_ROWS - 1, GRID_W, GRID_W)
    exp = jnp.where(col_mask[None, None], exp, -jnp.inf).transpose(0, 2, 1, 3)
    lo = NA_MAX_ROWS - 1
    tab = jnp.stack([exp[:, :, lo - o: lo - o + kh].reshape(HEADS, GRID_W, kh * GRID_W) for o in range(kh)],
                    axis=1)
    return tab.reshape(HEADS // nh, nh, kh, GRID_W, kh * GRID_W)


def _neighbourhood_attention(u, u_ctx, bias_tab):
    B, T, _ = u.shape
    L = u_ctx.shape[1]
    rows = T // GRID_W
    kh = min(NA_MAX_ROWS, rows)
    blk = D_MODEL // GROUP
    first = COL_NA // GROUP

    def col(part):
        return pl.BlockSpec((None, T, GROUP), lambda b, g: (b, 0, first + part * blk + g))

    def col_ctx(part):
        return pl.BlockSpec((None, L, GROUP), lambda b, g: (b, 0, first + part * blk + g))

    return pl.pallas_call(
        functools.partial(_na_kernel, rows=rows),
        out_shape=jax.ShapeDtypeStruct((B, T, D_MODEL), BF16),
        grid=(B, blk),
        in_specs=[col(0), col(1), col(2), col(3), col_ctx(1), col_ctx(2),
                  pl.BlockSpec((None, NA_HEADS_PER_STEP, kh, GRID_W, kh * GRID_W), lambda b, g: (g, 0, 0, 0, 0))],
        out_specs=pl.BlockSpec((None, T, GROUP), lambda b, g: (b, 0, g)),
        compiler_params=_cparams(("arbitrary", "arbitrary")),
        name="neighbourhood_attention",
    )(u, u, u, u, u_ctx, u_ctx, bias_tab)


def _block_diag_masks():
    r = lax.broadcasted_iota(jnp.int32, (GROUP, GROUP), 0)
    c = lax.broadcasted_iota(jnp.int32, (GROUP, GROUP), 1)
    return (r // HEAD_DIM) == (c // HEAD_DIM)


def _head_sum(x, ones_bd):
    rows = x.shape[0]
    xb = x.astype(BF16)
    stacked = jnp.concatenate([xb[:, g * GROUP:(g + 1) * GROUP] for g in range(N_GROUPS)], axis=0)
    sums = jnp.dot(stacked, ones_bd, preferred_element_type=F32)
    return jnp.concatenate([sums[g * rows:(g + 1) * rows] for g in range(N_GROUPS)], axis=1)


def _rwkv_prepare(d, local, cidx, n_chunks, refs, prm, ones_bd, rotary, ops_ref, pc_ref):
    rkv_ref, prev_ref, next_ref, lo_ref, rope_ref, bonus_ref = refs
    mu_ref, w0_ref, wup_ref, a0_ref, aup_ref, kk_ref, ka_ref, rk_ref = prm
    C = CHUNK
    halo = prev_ref.shape[0]
    block = rkv_ref.shape[0]
    rows = pl.ds(pl.multiple_of(local * C, C), C)
    rows_before = pl.ds(pl.multiple_of(jnp.maximum(local * C - halo, 0), halo), halo)
    rows_after = pl.ds(pl.multiple_of(jnp.minimum(local * C + C, block - halo), halo), halo)
    chunk_id = jnp.full((halo, D_MODEL), local, jnp.int32)
    at_block_start = chunk_id == 0
    at_block_end = chunk_id == block // C - 1

    def store(name, value):
        ops_ref[d * len(OPERANDS) + OPERANDS.index(name)] = value.astype(BF16)

    tt = lax.broadcasted_iota(jnp.int32, (C, C + 2 * halo), 0)
    ss = lax.broadcasted_iota(jnp.int32, (C, C + 2 * halo), 1) - halo
    first = jnp.where(cidx > 0, -1, 0)
    last = jnp.where(cidx < n_chunks - 1, C, C - 1)
    nb = (jnp.abs(ss - tt) == 1) & (ss >= first) & (ss <= last)
    nb = jnp.where(nb, 1.0, 0.0).astype(BF16)

    def shifted(i):
        lanes = slice(i * D_MODEL, (i + 1) * D_MODEL)
        x = rkv_ref[rows, lanes]
        before = jnp.where(at_block_start, prev_ref[:, lanes], rkv_ref[rows_before, lanes])
        after = jnp.where(at_block_end, next_ref[:, lanes], rkv_ref[rows_after, lanes])
        both = jnp.dot(nb, jnp.concatenate([before, x, after], axis=0), preferred_element_type=F32)
        mu = mu_ref[i:i + 1, :]
        return (1.0 - mu) * x.astype(F32) + (0.5 * mu) * both

    r_s = shifted(0)
    yield
    k_s = shifted(1)
    yield
    v_s = shifted(2)
    store("v", v_s)
    yield

    if rotary:
        lane = lax.broadcasted_iota(jnp.int32, (1, D_MODEL), 1)
        low = (lane % 32) < 16
        cos_t = jnp.tile(rope_ref[rows, :LANES], (1, D_MODEL // LANES))
        sin_t = jnp.tile(rope_ref[rows, LANES:], (1, D_MODEL // LANES))

        def rope(x):
            partner = jnp.where(low, pltpu.roll(x, D_MODEL - 16, axis=1), pltpu.roll(x, 16, axis=1))
            return x * cos_t + partner * sin_t

        r_s = rope(r_s)
        yield
        k_s = rope(k_s)
        yield

    lo = lo_ref[rows, :]
    lw = w0_ref[d:d + 1, :] + jnp.dot(jnp.tanh(lo[:, :2 * LORA].astype(F32)).astype(BF16), wup_ref[d],
                                     preferred_element_type=F32)
    ld = (-np.exp(-0.5)) * jax.nn.sigmoid(lw)
    yield
    a = jax.nn.sigmoid(a0_ref[d:d + 1, :] + jnp.dot(lo[:, 2 * LORA:], aup_ref[d], preferred_element_type=F32))
    yield

    kk = k_s * kk_ref[...]
    kk = kk * jnp.minimum(lax.rsqrt(_head_sum(kk * kk, ones_bd)), 1e12)
    yield
    k_dir = k_s * (1.0 + (a - 1.0) * ka_ref[...])
    b_vec = kk * a
    bonus_ref[rows, :] = (_head_sum(r_s * rk_ref[...] * k_dir, ones_bd) * v_s).astype(bonus_ref.dtype)
    yield

    tt = lax.broadcasted_iota(jnp.int32, (C, C), 0)
    ss = lax.broadcasted_iota(jnp.int32, (C, C), 1)
    tri = jnp.where((ss <= tt) if d == 0 else (ss >= tt), 1.0, 0.0).astype(BF16)
    ld_hi = ld.astype(BF16)
    ld_lo = (ld - ld_hi.astype(F32)).astype(BF16)
    cl = (jnp.dot(tri, ld_hi, preferred_element_type=F32) + jnp.dot(tri, ld_lo, preferred_element_type=F32))
    cl_tot = cl[C - 1:C, :] if d == 0 else cl[0:1, :]
    yield
    e_neg = jnp.exp(-cl)
    p_c = jnp.exp(cl_tot)
    pc_ref[d] = jnp.broadcast_to(p_c, pc_ref.shape[1:])
    b_t = b_vec * e_neg
    store("b", b_t)
    store("bh", b_t * p_c)
    yield
    k_t = k_dir * e_neg
    store("k", k_t)
    store("kh", k_t * p_c)
    yield
    store("a", -kk * jnp.exp(cl - ld))
    store("r", r_s * jnp.exp(cl))


OPERANDS = ("a", "r", "b", "k", "bh", "kh", "v")


def _interleave(*gens):
    live = list(gens)
    while live:
        for gen in list(live):
            try:
                next(gen)
            except StopIteration:
                live.remove(gen)


def _rwkv_chunk_matmuls(ops_ref, pc_ref, h_scr, yf_ref, yb_ref, local_f, local_b, same_head, emit_y):
    C = CHUNK
    n_ops = len(OPERANDS)
    n_g = 2 * N_GROUPS
    dirs = [g // N_GROUPS for g in range(n_g)]

    def lanes_of(g):
        return slice((g % N_GROUPS) * GROUP, (g % N_GROUPS + 1) * GROUP)

    def grp(name):
        i = OPERANDS.index(name)
        return [ops_ref[dirs[g] * n_ops + i, :, lanes_of(g)] for g in range(n_g)]

    t_i = lax.broadcasted_iota(jnp.int32, (C, GROUP), 0)
    s_i = lax.broadcasted_iota(jnp.int32, (C, GROUP), 1) % C
    before = [s_i < t_i, s_i > t_i]
    upto = [s_i <= t_i, s_i >= t_i]
    eye = jnp.where(t_i == s_i, 1.0, 0.0).astype(F32)
    nt = (((1,), (1,)), ((), ()))
    tn = (((0,), (0,)), ((), ()))

    def bd(x):
        return jnp.where(same_head, jnp.tile(x, (HEADS_PER_GROUP, 1)), jnp.zeros((), x.dtype))

    def hdot(a_side, x):
        return jnp.dot(a_side, bd(x), preferred_element_type=F32)

    def masked(m, keep):
        return jnp.where(keep, m, 0.0).astype(BF16)

    at, rt, bt = grp("a"), grp("r"), grp("b")
    lhs = [jnp.concatenate([at[g], rt[g]], axis=0) for g in range(n_g)] if emit_y else at
    a_ab, a_rb, a_ak, a_rk = [], [], [], []
    for g in range(n_g):
        m_b = lax.dot_general(lhs[g], bd(bt[g]), nt, preferred_element_type=F32)
        a_ab.append(masked(m_b[:C], before[dirs[g]]))
        if emit_y:
            a_rb.append(masked(m_b[C:], upto[dirs[g]]))
    yield
    kt = grp("k")
    for g in range(n_g):
        m_k = lax.dot_general(lhs[g], bd(kt[g]), nt, preferred_element_type=F32)
        a_ak.append(masked(m_k[:C], before[dirs[g]]))
        if emit_y:
            a_rk.append(masked(m_k[C:], upto[dirs[g]]))
    yield
    vv = grp("v")
    w_1, y_acc = [], []
    for g in range(n_g):
        if emit_y:
            w_v = hdot(jnp.concatenate([a_ak[g], a_rk[g]], axis=0), vv[g])
            y_acc.append(w_v[C:])
        else:
            w_v = hdot(a_ak[g], vv[g])
        w_1.append(w_v[:C].astype(BF16))
    yield

    pw = [hdot(a_ab[g], a_ab[g]).astype(BF16) for g in range(n_g)]
    inv = [eye + a_ab[g].astype(F32) for g in range(n_g)]
    yield
    n_factors = C.bit_length() - 1
    for j in range(1, n_factors):
        last = j == n_factors - 1
        for g in range(n_g):
            inv_bf = inv[g].astype(BF16)
            prod = hdot(inv_bf if last else jnp.concatenate([inv_bf, pw[g]], axis=0), pw[g])
            inv[g] = inv[g] + prod[:C]
            if not last:
                pw[g] = prod[C:].astype(BF16)
        yield
    inv = [inv[g].astype(BF16) for g in range(n_g)]

    a_p = [hdot(inv[g], at[g]).astype(BF16) for g in range(n_g)]
    yield
    u0 = [hdot(inv[g], w_1[g]) for g in range(n_g)]
    yield

    u, h_dec = [], []
    for g in range(n_g):
        decay_rows = (eye * pc_ref[dirs[g], 0:1, lanes_of(g)]).astype(BF16)
        stack = jnp.concatenate([a_p[g]] + ([rt[g]] if emit_y else []) + [decay_rows], axis=0)
        s1 = hdot(stack, h_scr[:, g * GROUP:(g + 1) * GROUP].astype(BF16))
        u.append((s1[:C] + u0[g]).astype(BF16))
        if emit_y:
            y_acc[g] = y_acc[g] + s1[C:2 * C]
        h_dec.append(s1[-C:])
    yield
    if emit_y:
        out_rows = [pl.ds(pl.multiple_of(local * C, C), C) for local in (local_f, local_b)]
        for g in range(n_g):
            y_ref = yf_ref if dirs[g] == 0 else yb_ref
            y_ref[out_rows[dirs[g]], lanes_of(g)] = (y_acc[g] + hdot(a_rb[g], u[g])).astype(y_ref.dtype)
    yield

    bh, kh = grp("bh"), grp("kh")
    lane_head = lax.broadcasted_iota(jnp.int32, (HEAD_DIM, GROUP), 1) // HEAD_DIM
    for g in range(n_g):
        full = lax.dot_general(jnp.concatenate([bh[g], kh[g]], axis=0),
                               jnp.concatenate([u[g], vv[g]], axis=0), tn, preferred_element_type=F32)
        upd = full[(HEADS_PER_GROUP - 1) * HEAD_DIM:]
        for j in range(HEADS_PER_GROUP - 2, -1, -1):
            upd = jnp.where(lane_head == j, full[j * HEAD_DIM:(j + 1) * HEAD_DIM], upd)
        h_scr[:, g * GROUP:(g + 1) * GROUP] = h_dec[g] + upd


def _rwkv_kernel(*refs, n_blocks, rotary, emit_y):
    per_dir = 5
    in_f, in_b = refs[:per_dir], refs[per_dir:2 * per_dir]
    prm = refs[2 * per_dir:2 * per_dir + 8]
    s0_ref = refs[2 * per_dir + 8]
    yf_ref, yb_ref, bonf_ref, bonb_ref, sout_ref, h_scr, ops_scr, pc_scr = refs[2 * per_dir + 9:]
    j = pl.program_id(1)
    per_block = in_f[0].shape[0] // CHUNK
    n_chunks = n_blocks * per_block

    @pl.when(j == 0)
    def _():
        h_scr[:, :D_MODEL] = s0_ref[0]
        h_scr[:, D_MODEL:] = s0_ref[1]

    if not emit_y:
        yf_ref[...] = jnp.zeros(yf_ref.shape, yf_ref.dtype)
        yb_ref[...] = jnp.zeros(yb_ref.shape, yb_ref.dtype)

    same_head = _block_diag_masks()
    ones_bd = jnp.where(same_head, 1.0, 0.0).astype(BF16)

    def local(d, s):
        return s if d == 0 else per_block - 1 - s

    def prepare(d, s, buf):
        block = j if d == 0 else n_blocks - 1 - j
        ins, bonus_ref = (in_f, bonf_ref) if d == 0 else (in_b, bonb_ref)
        return _rwkv_prepare(d, local(d, s), block * per_block + local(d, s), n_chunks, ins + (bonus_ref,), prm,
                             ones_bd, rotary, ops_scr.at[buf], pc_scr.at[buf])

    def matmuls(s, buf):
        return _rwkv_chunk_matmuls(ops_scr.at[buf], pc_scr.at[buf], h_scr, yf_ref, yb_ref,
                                   local(0, s), local(1, s), same_head, emit_y)

    _interleave(prepare(0, 0, 0), prepare(1, 0, 0))

    def chunk_pair(i, carry):
        s = 2 * i
        _interleave(matmuls(s, 0), itertools.chain(prepare(0, s + 1, 1), prepare(1, s + 1, 1)))
        ahead = jnp.minimum(s + 2, per_block - 1)
        _interleave(matmuls(s + 1, 1), itertools.chain(prepare(0, ahead, 0), prepare(1, ahead, 0)))
        return carry

    lax.fori_loop(0, per_block // 2, chunk_pair, 0)

    @pl.when(j == n_blocks - 1)
    def _():
        sout_ref[0] = h_scr[:, :D_MODEL]
        sout_ref[1] = h_scr[:, D_MODEL:]


def _rope_tables(T):
    half = HEAD_DIM // 2
    n_freq = half // 2
    t = np.arange(T)[:, None]
    lane = np.arange(LANES)[None, :]
    inv_freq = ROPE_THETA ** (-(np.arange(n_freq, dtype=np.float32)) / n_freq)
    pos = np.where((lane % HEAD_DIM) < half, t // GRID_W, t % GRID_W).astype(np.float32)
    ang = (pos * inv_freq[lane % n_freq].astype(np.float32)).astype(np.float32)
    sign = np.where((lane % half) < n_freq, -1.0, 1.0)
    return jnp.asarray(np.cos(ang), F32), jnp.asarray(np.sin(ang) * sign, F32)


def _rwkv_scan(u, state0, params, rotary, emit_y):
    B, T, _ = u.shape
    block = min(T, RWKV_BLOCK_CHUNKS * CHUNK)
    n_blocks = T // block
    assert T % block == 0 and (block // CHUNK) % 2 == 0
    halo = 16
    per = block // halo

    def block_of(d, j):
        return j if d == 0 else n_blocks - 1 - j

    def specs(d):
        rkv = COL_RKV // (3 * D_MODEL)
        main = pl.BlockSpec((None, block, 3 * D_MODEL), lambda b, j: (b, block_of(d, j), rkv))
        prev = pl.BlockSpec((None, halo, 3 * D_MODEL),
                            lambda b, j: (b, jnp.maximum(block_of(d, j) * per - 1, 0), rkv))
        nxt = pl.BlockSpec((None, halo, 3 * D_MODEL),
                           lambda b, j: (b, jnp.minimum((block_of(d, j) + 1) * per, T // halo - 1), rkv))
        lora = pl.BlockSpec((None, block, 4 * LORA), lambda b, j: (b, block_of(d, j), COL_LORA // (4 * LORA)))
        tab = pl.BlockSpec((block, 2 * LANES), lambda b, j: (block_of(d, j), 0))
        return [main, prev, nxt, lora, tab]

    def whole(shape):
        return pl.BlockSpec(shape, lambda b, c: (0,) * len(shape))

    rope_tab = jnp.concatenate(_rope_tables(T), axis=1)
    state = pl.BlockSpec((None, 2, HEAD_DIM, D_MODEL), lambda b, c: (b, 0, 0, 0))
    param_specs = [whole((3, D_MODEL)), whole((2, D_MODEL)), whole((2, 2 * LORA, D_MODEL)),
                   whole((2, D_MODEL)), whole((2, 2 * LORA, D_MODEL)),
                   whole((1, D_MODEL)), whole((1, D_MODEL)), whole((1, D_MODEL))]
    seq = jax.ShapeDtypeStruct((B, T, D_MODEL), BF16)
    out_f = pl.BlockSpec((None, block, D_MODEL), lambda b, j: (b, block_of(0, j), 0))
    out_b = pl.BlockSpec((None, block, D_MODEL), lambda b, j: (b, block_of(1, j), 0))
    seq_in = (u,) * 4 + (rope_tab,)
    return pl.pallas_call(
        functools.partial(_rwkv_kernel, n_blocks=n_blocks, rotary=rotary, emit_y=emit_y),
        out_shape=(seq, seq, seq, seq, jax.ShapeDtypeStruct((B, 2, HEAD_DIM, D_MODEL), F32)),
        grid=(B, n_blocks),
        in_specs=specs(0) + specs(1) + param_specs + [state],
        out_specs=(out_f, out_b, out_f, out_b, state),
        scratch_shapes=[pltpu.VMEM((HEAD_DIM, 2 * D_MODEL), F32),
                        pltpu.VMEM((2, 2 * len(OPERANDS), CHUNK, D_MODEL), BF16),
                        pltpu.VMEM((2, 2, 8, D_MODEL), F32)],
        compiler_params=_cparams(("arbitrary", "arbitrary")),
        name="rwkv7_chunk_scan_rot" if rotary else "rwkv7_chunk_scan_ctx",
    )(*seq_in, *seq_in, *params, state0)


def _final_kernel(x_ref, ya_ref, yf_ref, yb_ref, bf_ref, bb_ref, z_ref, ga_ref, gb_ref, gate_ref,
                  gng_ref, gnb_ref, wa_ref, wb_ref, wo_ref, lng_ref, lnb_ref, o_ref):
    same_head = _block_diag_masks()
    ones_bd = jnp.where(same_head, 1.0, 0.0).astype(BF16)
    y = yf_ref[...].astype(F32) + yb_ref[...].astype(F32)
    mu = _head_sum(y, ones_bd) * (1.0 / HEAD_DIM)
    yc = y - mu
    var = _head_sum(yc * yc, ones_bd) * (1.0 / HEAD_DIM)
    yn = yc * lax.rsqrt(var + GN_EPS) * gng_ref[...] + gnb_ref[...]
    yn = yn + bf_ref[...].astype(F32) + bb_ref[...].astype(F32)
    y_b = (yn * _silu(z_ref[...].astype(F32))).astype(BF16)
    p_a = jnp.dot(ya_ref[...], wa_ref[...], preferred_element_type=F32)
    p_b = jnp.dot(y_b, wb_ref[...], preferred_element_type=F32)
    merged = (jax.nn.sigmoid(ga_ref[...].astype(F32)) * p_a
              + jax.nn.sigmoid(gb_ref[...].astype(F32)) * p_b)
    out = jnp.dot(merged.astype(BF16), wo_ref[...], preferred_element_type=F32)
    t = ALPHA * x_ref[...] + gate_ref[...] * out
    m = jnp.mean(t, axis=-1, keepdims=True)
    tc = t - m
    v = jnp.mean(tc * tc, axis=-1, keepdims=True)
    o_ref[...] = tc * lax.rsqrt(v + LN_EPS) * lng_ref[...] + lnb_ref[...]


def _final_stage(x, y_a, y_f, y_b, bonus_f, bonus_b, u, mod3, gn_g, gn_b, wa, wb, wo, ln_g, ln_b, tm):
    B, T, _ = x.shape

    def tok():
        return pl.BlockSpec((None, tm, D_MODEL), lambda b, i: (b, i, 0))

    def ucol(blk):
        return pl.BlockSpec((None, tm, D_MODEL), lambda b, i: (b, i, blk))

    def vec():
        return pl.BlockSpec((1, D_MODEL), lambda b, i: (0, 0))

    def mat():
        return pl.BlockSpec((D_MODEL, D_MODEL), lambda b, i: (0, 0))

    return pl.pallas_call(
        _final_kernel,
        out_shape=jax.ShapeDtypeStruct((B, T, D_MODEL), F32),
        grid=(B, T // tm),
        in_specs=[tok(), tok(), tok(), tok(), tok(), tok(), ucol(7), ucol(8), ucol(9),
                  pl.BlockSpec((None, 1, D_MODEL), lambda b, i: (b, 0, 2)),
                  vec(), vec(), mat(), mat(), mat(), vec(), vec()],
        out_specs=tok(),
        compiler_params=_cparams(("arbitrary", "arbitrary")),
        name="readout_merge_out_proj",
    )(x, y_a, y_f, y_b, bonus_f, bonus_b, u, u, u, mod3, gn_g, gn_b, wa, wb, wo, ln_g, ln_b)


def _layer(x, c, ctx, c_ctx, w_ada, b_ada, w_in, na_rpb, rw_mu, rw_w0, rw_w_up, rw_a0, rw_a_up,
           rw_k_k, rw_k_a, rw_r_k, rw_gn_g, rw_gn_b, w_branch_a, w_branch_b, w_out, ln_g, ln_b):
    B, T, _ = x.shape
    L = ctx.shape[1]
    rows = T // GRID_W

    n_cond = -(-(B + 1) // 8) * 8
    cond = jnp.zeros((n_cond, D_MODEL), F32).at[:B].set(c).at[B].set(c_ctx)
    mod3 = _modulation(cond, w_ada, b_ada).reshape(n_cond, 1, 3 * D_MODEL)

    w_bf = jnp.concatenate(
        [w_in[:, 4 * D_MODEL:7 * D_MODEL], w_in[:, :4 * D_MODEL], w_in[:, 7 * D_MODEL:N_MAIN],
         w_in[:, N_MAIN + 4 * LORA:], w_in[:, N_MAIN:N_MAIN + 4 * LORA],
         jnp.zeros((D_MODEL, N_PAD - N_PROJ), w_in.dtype)], axis=1).astype(BF16)
    u = _in_projection(x, mod3, w_bf, lambda b: b, tm=min(T, 2048), tn=1536)
    u_ctx = _in_projection(ctx, mod3, w_bf, lambda b: B, tm=L, tn=1536)

    y_a = _neighbourhood_attention(u, u_ctx, _na_bias_table(na_rpb, rows))

    def lora_pad(w_up):
        out = jnp.zeros((2, 2 * LORA, D_MODEL), F32)
        for d in range(2):
            out = out.at[d, d * LORA:(d + 1) * LORA].set(w_up[d])
        return out.astype(BF16)

    params = (rw_mu, rw_w0, lora_pad(rw_w_up), rw_a0, lora_pad(rw_a_up), rw_k_k.reshape(1, D_MODEL),
              rw_k_a.reshape(1, D_MODEL), rw_r_k.reshape(1, D_MODEL))
    state0 = jnp.zeros((B, 2, HEAD_DIM, D_MODEL), F32)
    state_c = _rwkv_scan(u_ctx, state0, params, rotary=False, emit_y=False)[-1]
    y_f, y_b, bonus_f, bonus_b, _ = _rwkv_scan(u, state_c, params, rotary=True, emit_y=True)

    return _final_stage(x, y_a, y_f, y_b, bonus_f, bonus_b, u, mod3, rw_gn_g.reshape(1, D_MODEL),
                        rw_gn_b.reshape(1, D_MODEL), w_branch_a.astype(BF16), w_branch_b.astype(BF16),
                        w_out.astype(BF16), ln_g.reshape(1, D_MODEL), ln_b.reshape(1, D_MODEL),
                        tm=min(T, 512))


def kernel(x, c, ctx, c_ctx, w_ada, b_ada, w_in, na_rpb, rw_mu, rw_w0, rw_w_up, rw_a0, rw_a_up, rw_k_k, rw_k_a, rw_r_k, rw_gn_g, rw_gn_b, w_branch_a, w_branch_b, w_out, ln_g, ln_b):
    assert w_ada.shape[0] == DEPTH
    return _layer(x, c, ctx, c_ctx, w_ada[0], b_ada[0], w_in[0], na_rpb[0], rw_mu[0], rw_w0[0],
                  rw_w_up[0], rw_a0[0], rw_a_up[0], rw_k_k[0], rw_k_a[0], rw_r_k[0], rw_gn_g[0],
                  rw_gn_b[0], w_branch_a[0], w_branch_b[0], w_out[0], ln_g[0], ln_b[0])
```

```python
import functools
import itertools

import numpy as np
import jax
import jax.numpy as jnp
from jax import lax
from jax.experimental import pallas as pl
from jax.experimental.pallas import tpu as pltpu

F32 = jnp.float32
BF16 = jnp.bfloat16
HIGHEST = lax.Precision.HIGHEST

D_MODEL = 1024
GRID_W = 64
HEADS = 16
HEAD_DIM = 64
NA_MAX_ROWS = 8
NA_COLS = 16
LORA = 64
DEPTH = 1
ROPE_THETA = 10000.0
LN_EPS = 1e-5
GN_EPS = 64e-5
ALPHA = (2 * DEPTH) ** 0.25

LANES = 128
GROUP = 256
HEADS_PER_GROUP = GROUP // HEAD_DIM
N_GROUPS = D_MODEL // GROUP
CHUNK = 64
RWKV_BLOCK_CHUNKS = 8

COL_RKV = 0
COL_NA = 3 * D_MODEL
COL_ZB = 7 * D_MODEL
N_MAIN = 8 * D_MODEL
COL_MG = N_MAIN
COL_LORA = N_MAIN + 2 * D_MODEL
N_PROJ = COL_LORA + 4 * LORA
N_PAD = 21 * 512
VMEM_LIMIT = 56 * 1024 * 1024


def _cparams(sem):
    return pltpu.CompilerParams(dimension_semantics=sem, vmem_limit_bytes=VMEM_LIMIT)


def _silu(x):
    return x * jax.nn.sigmoid(x)


def _mod_kernel(c_ref, w_ref, b_ref, o_ref):
    s = _silu(c_ref[...])
    o_ref[...] = jnp.dot(s, w_ref[...], preferred_element_type=F32, precision=HIGHEST) + b_ref[...]


def _modulation(cond, w_ada, b_ada):
    rows = cond.shape[0]
    return pl.pallas_call(
        _mod_kernel,
        out_shape=jax.ShapeDtypeStruct((rows, 3 * D_MODEL), F32),
        grid=(3,),
        in_specs=[pl.BlockSpec((rows, D_MODEL), lambda j: (0, 0)),
                  pl.BlockSpec((D_MODEL, D_MODEL), lambda j: (0, j)),
                  pl.BlockSpec((1, D_MODEL), lambda j: (0, j))],
        out_specs=pl.BlockSpec((rows, D_MODEL), lambda j: (0, j)),
        compiler_params=_cparams(("arbitrary",)),
        name="ada_modulation",
    )(cond, w_ada, b_ada.reshape(1, 3 * D_MODEL))


def _inproj_kernel(x_ref, sh_ref, sc_ref, w_ref, o_ref, h_scr):
    @pl.when(pl.program_id(2) == 0)
    def _():
        x = x_ref[...]
        mu = jnp.mean(x, axis=-1, keepdims=True)
        xc = x - mu
        var = jnp.mean(xc * xc, axis=-1, keepdims=True)
        y = xc * lax.rsqrt(var + LN_EPS)
        h_scr[...] = (y * (1.0 + sc_ref[...]) + sh_ref[...]).astype(BF16)

    o_ref[...] = jnp.dot(h_scr[...], w_ref[...], preferred_element_type=F32).astype(BF16)


def _in_projection(x, mod3, w_bf, row_of_batch, tm, tn):
    B, T, _ = x.shape
    return pl.pallas_call(
        _inproj_kernel,
        out_shape=jax.ShapeDtypeStruct((B, T, N_PAD), BF16),
        grid=(B, T // tm, N_PAD // tn),
        in_specs=[pl.BlockSpec((None, tm, D_MODEL), lambda b, i, j: (b, i, 0)),
                  pl.BlockSpec((None, 1, D_MODEL), lambda b, i, j: (row_of_batch(b), 0, 0)),
                  pl.BlockSpec((None, 1, D_MODEL), lambda b, i, j: (row_of_batch(b), 0, 1)),
                  pl.BlockSpec((D_MODEL, tn), lambda b, i, j: (0, j))],
        out_specs=pl.BlockSpec((None, tm, tn), lambda b, i, j: (b, i, j)),
        scratch_shapes=[pltpu.VMEM((tm, D_MODEL), BF16)],
        compiler_params=_cparams(("arbitrary", "arbitrary", "arbitrary")),
        name="ln_mod_in_proj",
    )(x, mod3, mod3, w_bf)


NA_HEADS_PER_STEP = GROUP // HEAD_DIM
NA_ROWS_PER_ITER = 8
NA_ROW_GROUPS = 2


def _na_kernel(q_ref, k_ref, v_ref, z_ref, kc_ref, vc_ref, bias_ref, o_ref, *, rows):
    kh = min(NA_MAX_ROWS, rows)
    band = kh * GRID_W
    nh = NA_HEADS_PER_STEP
    lane = lax.broadcasted_iota(jnp.int32, (1, GROUP), 1)
    head_lanes = [(lane // HEAD_DIM) == h for h in range(nh)]
    kc = kc_ref[...]
    vc = vc_ref[...]
    nt = (((1,), (1,)), ((), ()))

    n_rows = NA_ROWS_PER_ITER if rows % NA_ROWS_PER_ITER == 0 else 1
    stack = nh * GRID_W

    def row_group(first_row, count, delay):
        for _ in range(delay):
            yield
        qs, s, vb, q0s = [], [], [], []
        for r in range(count):
            i = first_row + r
            rs = jnp.clip(i - kh // 2, 0, rows - kh)
            q0 = pl.multiple_of(i * GRID_W, GRID_W)
            k0 = pl.multiple_of(rs * GRID_W, GRID_W)
            q = q_ref[pl.ds(q0, GRID_W), :] * jnp.asarray(HEAD_DIM ** -0.5, BF16)
            q_st = jnp.concatenate([jnp.where(head_lanes[h], q, jnp.zeros_like(q)) for h in range(nh)], axis=0)
            s.append(lax.dot_general(q_st, k_ref[pl.ds(k0, band), :], nt, preferred_element_type=F32)
                     + bias_ref[:, i - rs].reshape(stack, band))
            vb.append(v_ref[pl.ds(k0, band), :])
            qs.append(q_st)
            q0s.append(q0)
        sc = lax.dot_general(jnp.concatenate(qs, axis=0), kc, nt, preferred_element_type=F32)
        yield
        p, pcs, denom = [], [], []
        for r in range(count):
            sc_r = sc[r * stack:(r + 1) * stack]
            m = jnp.maximum(jnp.max(s[r], axis=-1, keepdims=True), jnp.max(sc_r, axis=-1, keepdims=True))
            p_r = jnp.exp(s[r] - m)
            pc_r = jnp.exp(sc_r - m)
            denom.append(jnp.sum(p_r, axis=-1, keepdims=True) + jnp.sum(pc_r, axis=-1, keepdims=True))
            p.append(p_r.astype(BF16))
            pcs.append(pc_r.astype(BF16))
        yield
        o_ctx = jnp.dot(jnp.concatenate(pcs, axis=0), vc, preferred_element_type=F32)
        for r in range(count):
            o_all = (jnp.dot(p[r], vb[r], preferred_element_type=F32)
                     + o_ctx[r * stack:(r + 1) * stack]) / denom[r]
            o = o_all[(nh - 1) * GRID_W:]
            for h in range(nh - 2, -1, -1):
                o = jnp.where(head_lanes[h], o_all[h * GRID_W:(h + 1) * GRID_W], o)
            z = z_ref[pl.ds(q0s[r], GRID_W), :].astype(F32)
            o_ref[pl.ds(q0s[r], GRID_W), :] = (o * _silu(z)).astype(BF16)

    def rows_body(it, carry):
        per_group = n_rows // NA_ROW_GROUPS if n_rows % NA_ROW_GROUPS == 0 else n_rows
        groups = n_rows // per_group
        _interleave(*[row_group(it * n_rows + g * per_group, per_group, g) for g in range(groups)])
        return carry

    lax.fori_loop(0, rows // n_rows, rows_body, 0)


def _na_bias_table(rpb, rows):
    kh = min(NA_MAX_ROWS, rows)
    nh = NA_HEADS_PER_STEP
    n_dj = 2 * NA_COLS - 1
    cols = np.arange(GRID_W)
    cstart = np.clip(cols - NA_COLS // 2, 0, GRID_W - NA_COLS)
    col_mask = (cols[None, :] >= cstart[:, None]) & (cols[None, :] < cstart[:, None] + NA_COLS)
    dj = np.clip(cols[None, :] - cols[:, None] + NA_COLS - 1, 0, n_dj - 1)
    onehot = (np.arange(n_dj)[:, None] == dj.reshape(1, -1)).astype(np.float32)
    exp = jnp.dot(rpb.reshape(-1, n_dj), jnp.asarray(onehot), precision=HIGHEST)
    exp = exp.reshape(HEADS, 2 * NA_MAX_ROWS - 1, GRID_W, GRID_W)
    exp = jnp.where(col_mask[None, None], exp, -jnp.inf).transpose(0, 2, 1, 3)
    lo = NA_MAX_ROWS - 1
    tab = jnp.stack([exp[:, :, lo - o: lo - o + kh].reshape(HEADS, GRID_W, kh * GRID_W) for o in range(kh)],
                    axis=1)
    return tab.reshape(HEADS // nh, nh, kh, GRID_W, kh * GRID_W)


def _neighbourhood_attention(u, u_ctx, bias_tab):
    B, T, _ = u.shape
    L = u_ctx.shape[1]
    rows = T // GRID_W
    kh = min(NA_MAX_ROWS, rows)
    blk = D_MODEL // GROUP
    first = COL_NA // GROUP

    def col(part):
        return pl.BlockSpec((None, T, GROUP), lambda b, g: (b, 0, first + part * blk + g))

    def col_ctx(part):
        return pl.BlockSpec((None, L, GROUP), lambda b, g: (b, 0, first + part * blk + g))

    return pl.pallas_call(
        functools.partial(_na_kernel, rows=rows),
        out_shape=jax.ShapeDtypeStruct((B, T, D_MODEL), BF16),
        grid=(B, blk),
        in_specs=[col(0), col(1), col(2), col(3), col_ctx(1), col_ctx(2),
                  pl.BlockSpec((None, NA_HEADS_PER_STEP, kh, GRID_W, kh * GRID_W), lambda b, g: (g, 0, 0, 0, 0))],
        out_specs=pl.BlockSpec((None, T, GROUP), lambda b, g: (b, 0, g)),
        compiler_params=_cparams(("arbitrary", "arbitrary")),
        name="neighbourhood_attention",
    )(u, u, u, u, u_ctx, u_ctx, bias_tab)


def _block_diag_masks():
    r = lax.broadcasted_iota(jnp.int32, (GROUP, GROUP), 0)
    c = lax.broadcasted_iota(jnp.int32, (GROUP, GROUP), 1)
    return (r // HEAD_DIM) == (c // HEAD_DIM)


def _head_sum(x, ones_bd):
    rows = x.shape[0]
    xb = x.astype(BF16)
    stacked = jnp.concatenate([xb[:, g * GROUP:(g + 1) * GROUP] for g in range(N_GROUPS)], axis=0)
    sums = jnp.dot(stacked, ones_bd, preferred_element_type=F32)
    return jnp.concatenate([sums[g * rows:(g + 1) * rows] for g in range(N_GROUPS)], axis=1)


def _rwkv_prepare(d, local, cidx, n_chunks, refs, prm, ones_bd, rotary, ops_ref, pc_ref):
    rkv_ref, prev_ref, next_ref, lo_ref, rope_ref, bonus_ref = refs
    mu_ref, w0_ref, wup_ref, a0_ref, aup_ref, kk_ref, ka_ref, rk_ref = prm
    C = CHUNK
    halo = prev_ref.shape[0]
    block = rkv_ref.shape[0]
    rows = pl.ds(pl.multiple_of(local * C, C), C)
    rows_before = pl.ds(pl.multiple_of(jnp.maximum(local * C - halo, 0), halo), halo)
    rows_after = pl.ds(pl.multiple_of(jnp.minimum(local * C + C, block - halo), halo), halo)
    chunk_id = jnp.full((halo, D_MODEL), local, jnp.int32)
    at_block_start = chunk_id == 0
    at_block_end = chunk_id == block // C - 1

    def store(name, value):
        ops_ref[d * len(OPERANDS) + OPERANDS.index(name)] = value.astype(BF16)

    tt = lax.broadcasted_iota(jnp.int32, (C, C + 2 * halo), 0)
    ss = lax.broadcasted_iota(jnp.int32, (C, C + 2 * halo), 1) - halo
    first = jnp.where(cidx > 0, -1, 0)
    last = jnp.where(cidx < n_chunks - 1, C, C - 1)
    nb = (jnp.abs(ss - tt) == 1) & (ss >= first) & (ss <= last)
    nb = jnp.where(nb, 1.0, 0.0).astype(BF16)

    def shifted(i):
        lanes = slice(i * D_MODEL, (i + 1) * D_MODEL)
        x = rkv_ref[rows, lanes]
        before = jnp.where(at_block_start, prev_ref[:, lanes], rkv_ref[rows_before, lanes])
        after = jnp.where(at_block_end, next_ref[:, lanes], rkv_ref[rows_after, lanes])
        both = jnp.dot(nb, jnp.concatenate([before, x, after], axis=0), preferred_element_type=F32)
        mu = mu_ref[i:i + 1, :]
        return (1.0 - mu) * x.astype(F32) + (0.5 * mu) * both

    r_s = shifted(0)
    yield
    k_s = shifted(1)
    yield
    v_s = shifted(2)
    store("v", v_s)
    yield

    if rotary:
        lane = lax.broadcasted_iota(jnp.int32, (1, D_MODEL), 1)
        low = (lane % 32) < 16
        cos_t = jnp.tile(rope_ref[rows, :LANES], (1, D_MODEL // LANES))
        sin_t = jnp.tile(rope_ref[rows, LANES:], (1, D_MODEL // LANES))

        def rope(x):
            partner = jnp.where(low, pltpu.roll(x, D_MODEL - 16, axis=1), pltpu.roll(x, 16, axis=1))
            return x * cos_t + partner * sin_t

        r_s = rope(r_s)
        yield
        k_s = rope(k_s)
        yield

    lo = lo_ref[rows, :]
    lw = w0_ref[d:d + 1, :] + jnp.dot(jnp.tanh(lo[:, :2 * LORA].astype(F32)).astype(BF16), wup_ref[d],
                                     preferred_element_type=F32)
    ld = (-np.exp(-0.5)) * jax.nn.sigmoid(lw)
    yield
    a = jax.nn.sigmoid(a0_ref[d:d + 1, :] + jnp.dot(lo[:, 2 * LORA:], aup_ref[d], preferred_element_type=F32))
    yield

    kk = k_s * kk_ref[...]
    kk = kk * jnp.minimum(lax.rsqrt(_head_sum(kk * kk, ones_bd)), 1e12)
    yield
    k_dir = k_s * (1.0 + (a - 1.0) * ka_ref[...])
    b_vec = kk * a
    bonus_ref[rows, :] = (_head_sum(r_s * rk_ref[...] * k_dir, ones_bd) * v_s).astype(bonus_ref.dtype)
    yield

    tt = lax.broadcasted_iota(jnp.int32, (C, C), 0)
    ss = lax.broadcasted_iota(jnp.int32, (C, C), 1)
    tri = jnp.where((ss <= tt) if d == 0 else (ss >= tt), 1.0, 0.0).astype(BF16)
    ld_hi = ld.astype(BF16)
    ld_lo = (ld - ld_hi.astype(F32)).astype(BF16)
    cl = (jnp.dot(tri, ld_hi, preferred_element_type=F32) + jnp.dot(tri, ld_lo, preferred_element_type=F32))
    cl_tot = cl[C - 1:C, :] if d == 0 else cl[0:1, :]
    yield
    e_neg = jnp.exp(-cl)
    p_c = jnp.exp(cl_tot)
    pc_ref[d] = jnp.broadcast_to(p_c, pc_ref.shape[1:])
    b_t = b_vec * e_neg
    store("b", b_t)
    store("bh", b_t * p_c)
    yield
    k_t = k_dir * e_neg
    store("k", k_t)
    store("kh", k_t * p_c)
    yield
    store("a", -kk * jnp.exp(cl - ld))
    store("r", r_s * jnp.exp(cl))


OPERANDS = ("a", "r", "b", "k", "bh", "kh", "v")


PREPARE_STEPS = 2


def _interleave(*gens, steps=None):
    live = [[gen, 1 if steps is None else steps[i]] for i, gen in enumerate(gens)]
    while live:
        for entry in list(live):
            for _ in range(entry[1]):
                try:
                    next(entry[0])
                except StopIteration:
                    live.remove(entry)
                    break


def _rwkv_chunk_matmuls(ops_ref, pc_ref, h_scr, yf_ref, yb_ref, local_f, local_b, same_head, emit_y):
    C = CHUNK
    n_ops = len(OPERANDS)
    n_g = 2 * N_GROUPS
    dirs = [g // N_GROUPS for g in range(n_g)]

    def lanes_of(g):
        return slice((g % N_GROUPS) * GROUP, (g % N_GROUPS + 1) * GROUP)

    def grp(name):
        i = OPERANDS.index(name)
        return [ops_ref[dirs[g] * n_ops + i, :, lanes_of(g)] for g in range(n_g)]

    t_i = lax.broadcasted_iota(jnp.int32, (C, GROUP), 0)
    s_i = lax.broadcasted_iota(jnp.int32, (C, GROUP), 1) % C
    before = [s_i < t_i, s_i > t_i]
    upto = [s_i <= t_i, s_i >= t_i]
    eye = jnp.where(t_i == s_i, 1.0, 0.0).astype(F32)
    nt = (((1,), (1,)), ((), ()))
    tn = (((0,), (0,)), ((), ()))

    def bd(x):
        return jnp.where(same_head, jnp.tile(x, (HEADS_PER_GROUP, 1)), jnp.zeros((), x.dtype))

    def hdot(a_side, x):
        return jnp.dot(a_side, bd(x), preferred_element_type=F32)

    def masked(m, keep):
        return jnp.where(keep, m, 0.0).astype(BF16)

    at, rt, bt = grp("a"), grp("r"), grp("b")
    lhs = [jnp.concatenate([at[g], rt[g]], axis=0) for g in range(n_g)] if emit_y else at
    a_ab, a_rb, a_ak, a_rk = [], [], [], []
    for g in range(n_g):
        m_b = lax.dot_general(lhs[g], bd(bt[g]), nt, preferred_element_type=F32)
        a_ab.append(masked(m_b[:C], before[dirs[g]]))
        if emit_y:
            a_rb.append(masked(m_b[C:], upto[dirs[g]]))
    yield
    kt = grp("k")
    for g in range(n_g):
        m_k = lax.dot_general(lhs[g], bd(kt[g]), nt, preferred_element_type=F32)
        a_ak.append(masked(m_k[:C], before[dirs[g]]))
        if emit_y:
            a_rk.append(masked(m_k[C:], upto[dirs[g]]))
    yield
    vv = grp("v")
    w_1, y_acc = [], []
    for g in range(n_g):
        if emit_y:
            w_v = hdot(jnp.concatenate([a_ak[g], a_rk[g]], axis=0), vv[g])
            y_acc.append(w_v[C:])
        else:
            w_v = hdot(a_ak[g], vv[g])
        w_1.append(w_v[:C])
    yield

    pw = [hdot(a_ab[g], a_ab[g]).astype(BF16) for g in range(n_g)]
    inv = [eye + a_ab[g].astype(F32) for g in range(n_g)]
    yield
    n_factors = C.bit_length() - 1
    for j in range(1, n_factors):
        last = j == n_factors - 1
        for g in range(n_g):
            inv_bf = inv[g].astype(BF16)
            prod = hdot(inv_bf if last else jnp.concatenate([inv_bf, pw[g]], axis=0), pw[g])
            inv[g] = inv[g] + prod[:C]
            if not last:
                pw[g] = prod[C:].astype(BF16)
        yield
    inv = [inv[g].astype(BF16) for g in range(n_g)]

    w_all, h_dec = [], []
    for g in range(n_g):
        decay_rows = (eye * pc_ref[dirs[g], 0:1, lanes_of(g)]).astype(BF16)
        stack = jnp.concatenate([at[g]] + ([rt[g]] if emit_y else []) + [decay_rows], axis=0)
        s1 = hdot(stack, h_scr[:, g * GROUP:(g + 1) * GROUP].astype(BF16))
        w_all.append((s1[:C] + w_1[g]).astype(BF16))
        if emit_y:
            y_acc[g] = y_acc[g] + s1[C:2 * C]
        h_dec.append(s1[-C:])
    yield
    u = [hdot(inv[g], w_all[g]).astype(BF16) for g in range(n_g)]
    yield
    if emit_y:
        out_rows = [pl.ds(pl.multiple_of(local * C, C), C) for local in (local_f, local_b)]
        for g in range(n_g):
            y_ref = yf_ref if dirs[g] == 0 else yb_ref
            y_ref[out_rows[dirs[g]], lanes_of(g)] = (y_acc[g] + hdot(a_rb[g], u[g])).astype(y_ref.dtype)
    yield

    bh, kh = grp("bh"), grp("kh")
    lane_head = lax.broadcasted_iota(jnp.int32, (HEAD_DIM, GROUP), 1) // HEAD_DIM
    for g in range(n_g):
        full = lax.dot_general(jnp.concatenate([bh[g], kh[g]], axis=0),
                               jnp.concatenate([u[g], vv[g]], axis=0), tn, preferred_element_type=F32)
        upd = full[(HEADS_PER_GROUP - 1) * HEAD_DIM:]
        for j in range(HEADS_PER_GROUP - 2, -1, -1):
            upd = jnp.where(lane_head == j, full[j * HEAD_DIM:(j + 1) * HEAD_DIM], upd)
        h_scr[:, g * GROUP:(g + 1) * GROUP] = h_dec[g] + upd


def _rwkv_kernel(*refs, n_blocks, rotary, emit_y):
    per_dir = 5
    in_f, in_b = refs[:per_dir], refs[per_dir:2 * per_dir]
    prm = refs[2 * per_dir:2 * per_dir + 8]
    s0_ref = refs[2 * per_dir + 8]
    yf_ref, yb_ref, bonf_ref, bonb_ref, sout_ref, h_scr, ops_scr, pc_scr = refs[2 * per_dir + 9:]
    j = pl.program_id(1)
    per_block = in_f[0].shape[0] // CHUNK
    n_chunks = n_blocks * per_block

    @pl.when(j == 0)
    def _():
        h_scr[:, :D_MODEL] = s0_ref[0]
        h_scr[:, D_MODEL:] = s0_ref[1]

    if not emit_y:
        yf_ref[...] = jnp.zeros(yf_ref.shape, yf_ref.dtype)
        yb_ref[...] = jnp.zeros(yb_ref.shape, yb_ref.dtype)

    same_head = _block_diag_masks()
    ones_bd = jnp.where(same_head, 1.0, 0.0).astype(BF16)

    def local(d, s):
        return s if d == 0 else per_block - 1 - s

    def prepare(d, s, buf):
        block = j if d == 0 else n_blocks - 1 - j
        ins, bonus_ref = (in_f, bonf_ref) if d == 0 else (in_b, bonb_ref)
        return _rwkv_prepare(d, local(d, s), block * per_block + local(d, s), n_chunks, ins + (bonus_ref,), prm,
                             ones_bd, rotary, ops_scr.at[buf], pc_scr.at[buf])

    def matmuls(s, buf):
        return _rwkv_chunk_matmuls(ops_scr.at[buf], pc_scr.at[buf], h_scr, yf_ref, yb_ref,
                                   local(0, s), local(1, s), same_head, emit_y)

    _interleave(prepare(0, 0, 0), prepare(1, 0, 0))

    def chunk_pair(i, carry):
        s = 2 * i
        ahead = jnp.minimum(s + 2, per_block - 1)
        _interleave(matmuls(s, 0), itertools.chain(prepare(0, s + 1, 1), prepare(1, s + 1, 1)),
                    steps=(1, PREPARE_STEPS))
        _interleave(matmuls(s + 1, 1), itertools.chain(prepare(0, ahead, 0), prepare(1, ahead, 0)),
                    steps=(1, PREPARE_STEPS))
        return carry

    lax.fori_loop(0, per_block // 2, chunk_pair, 0)

    @pl.when(j == n_blocks - 1)
    def _():
        sout_ref[0] = h_scr[:, :D_MODEL]
        sout_ref[1] = h_scr[:, D_MODEL:]


def _rope_tables(T):
    half = HEAD_DIM // 2
    n_freq = half // 2
    t = np.arange(T)[:, None]
    lane = np.arange(LANES)[None, :]
    inv_freq = ROPE_THETA ** (-(np.arange(n_freq, dtype=np.float32)) / n_freq)
    pos = np.where((lane % HEAD_DIM) < half, t // GRID_W, t % GRID_W).astype(np.float32)
    ang = (pos * inv_freq[lane % n_freq].astype(np.float32)).astype(np.float32)
    sign = np.where((lane % half) < n_freq, -1.0, 1.0)
    return jnp.asarray(np.cos(ang), F32), jnp.asarray(np.sin(ang) * sign, F32)


def _rwkv_scan(u, state0, params, rotary, emit_y):
    B, T, _ = u.shape
    block = min(T, RWKV_BLOCK_CHUNKS * CHUNK)
    n_blocks = T // block
    assert T % block == 0 and (block // CHUNK) % 2 == 0
    halo = 16
    per = block // halo

    def block_of(d, j):
        return j if d == 0 else n_blocks - 1 - j

    def specs(d):
        rkv = COL_RKV // (3 * D_MODEL)
        main = pl.BlockSpec((None, block, 3 * D_MODEL), lambda b, j: (b, block_of(d, j), rkv))
        prev = pl.BlockSpec((None, halo, 3 * D_MODEL),
                            lambda b, j: (b, jnp.maximum(block_of(d, j) * per - 1, 0), rkv))
        nxt = pl.BlockSpec((None, halo, 3 * D_MODEL),
                           lambda b, j: (b, jnp.minimum((block_of(d, j) + 1) * per, T // halo - 1), rkv))
        lora = pl.BlockSpec((None, block, 4 * LORA), lambda b, j: (b, block_of(d, j), COL_LORA // (4 * LORA)))
        tab = pl.BlockSpec((block, 2 * LANES), lambda b, j: (block_of(d, j), 0))
        return [main, prev, nxt, lora, tab]

    def whole(shape):
        return pl.BlockSpec(shape, lambda b, c: (0,) * len(shape))

    rope_tab = jnp.concatenate(_rope_tables(T), axis=1)
    state = pl.BlockSpec((None, 2, HEAD_DIM, D_MODEL), lambda b, c: (b, 0, 0, 0))
    param_specs = [whole((3, D_MODEL)), whole((2, D_MODEL)), whole((2, 2 * LORA, D_MODEL)),
                   whole((2, D_MODEL)), whole((2, 2 * LORA, D_MODEL)),
                   whole((1, D_MODEL)), whole((1, D_MODEL)), whole((1, D_MODEL))]
    seq = jax.ShapeDtypeStruct((B, T, D_MODEL), BF16)
    out_f = pl.BlockSpec((None, block, D_MODEL), lambda b, j: (b, block_of(0, j), 0))
    out_b = pl.BlockSpec((None, block, D_MODEL), lambda b, j: (b, block_of(1, j), 0))
    seq_in = (u,) * 4 + (rope_tab,)
    return pl.pallas_call(
        functools.partial(_rwkv_kernel, n_blocks=n_blocks, rotary=rotary, emit_y=emit_y),
        out_shape=(seq, seq, seq, seq, jax.ShapeDtypeStruct((B, 2, HEAD_DIM, D_MODEL), F32)),
        grid=(B, n_blocks),
        in_specs=specs(0) + specs(1) + param_specs + [state],
        out_specs=(out_f, out_b, out_f, out_b, state),
        scratch_shapes=[pltpu.VMEM((HEAD_DIM, 2 * D_MODEL), F32),
                        pltpu.VMEM((2, 2 * len(OPERANDS), CHUNK, D_MODEL), BF16),
                        pltpu.VMEM((2, 2, 8, D_MODEL), F32)],
        compiler_params=_cparams(("arbitrary", "arbitrary")),
        name="rwkv7_chunk_scan_rot" if rotary else "rwkv7_chunk_scan_ctx",
    )(*seq_in, *seq_in, *params, state0)


def _final_kernel(x_ref, ya_ref, yf_ref, yb_ref, bf_ref, bb_ref, z_ref, ga_ref, gb_ref, gate_ref,
                  gng_ref, gnb_ref, wa_ref, wb_ref, wo_ref, lng_ref, lnb_ref, o_ref):
    same_head = _block_diag_masks()
    ones_bd = jnp.where(same_head, 1.0, 0.0).astype(BF16)
    y = yf_ref[...].astype(F32) + yb_ref[...].astype(F32)
    mu = _head_sum(y, ones_bd) * (1.0 / HEAD_DIM)
    yc = y - mu
    var = _head_sum(yc * yc, ones_bd) * (1.0 / HEAD_DIM)
    yn = yc * lax.rsqrt(var + GN_EPS) * gng_ref[...] + gnb_ref[...]
    yn = yn + bf_ref[...].astype(F32) + bb_ref[...].astype(F32)
    y_b = (yn * _silu(z_ref[...].astype(F32))).astype(BF16)
    p_a = jnp.dot(ya_ref[...], wa_ref[...], preferred_element_type=F32)
    p_b = jnp.dot(y_b, wb_ref[...], preferred_element_type=F32)
    merged = (jax.nn.sigmoid(ga_ref[...].astype(F32)) * p_a
              + jax.nn.sigmoid(gb_ref[...].astype(F32)) * p_b)
    out = jnp.dot(merged.astype(BF16), wo_ref[...], preferred_element_type=F32)
    t = ALPHA * x_ref[...] + gate_ref[...] * out
    m = jnp.mean(t, axis=-1, keepdims=True)
    tc = t - m
    v = jnp.mean(tc * tc, axis=-1, keepdims=True)
    o_ref[...] = tc * lax.rsqrt(v + LN_EPS) * lng_ref[...] + lnb_ref[...]


def _final_stage(x, y_a, y_f, y_b, bonus_f, bonus_b, u, mod3, gn_g, gn_b, wa, wb, wo, ln_g, ln_b, tm):
    B, T, _ = x.shape

    def tok():
        return pl.BlockSpec((None, tm, D_MODEL), lambda b, i: (b, i, 0))

    def ucol(blk):
        return pl.BlockSpec((None, tm, D_MODEL), lambda b, i: (b, i, blk))

    def vec():
        return pl.BlockSpec((1, D_MODEL), lambda b, i: (0, 0))

    def mat():
        return pl.BlockSpec((D_MODEL, D_MODEL), lambda b, i: (0, 0))

    return pl.pallas_call(
        _final_kernel,
        out_shape=jax.ShapeDtypeStruct((B, T, D_MODEL), F32),
        grid=(B, T // tm),
        in_specs=[tok(), tok(), tok(), tok(), tok(), tok(), ucol(7), ucol(8), ucol(9),
                  pl.BlockSpec((None, 1, D_MODEL), lambda b, i: (b, 0, 2)),
                  vec(), vec(), mat(), mat(), mat(), vec(), vec()],
        out_specs=tok(),
        compiler_params=_cparams(("arbitrary", "arbitrary")),
        name="readout_merge_out_proj",
    )(x, y_a, y_f, y_b, bonus_f, bonus_b, u, u, u, mod3, gn_g, gn_b, wa, wb, wo, ln_g, ln_b)


def _layer(x, c, ctx, c_ctx, w_ada, b_ada, w_in, na_rpb, rw_mu, rw_w0, rw_w_up, rw_a0, rw_a_up,
           rw_k_k, rw_k_a, rw_r_k, rw_gn_g, rw_gn_b, w_branch_a, w_branch_b, w_out, ln_g, ln_b):
    B, T, _ = x.shape
    L = ctx.shape[1]
    rows = T // GRID_W

    n_cond = -(-(B + 1) // 8) * 8
    cond = jnp.zeros((n_cond, D_MODEL), F32).at[:B].set(c).at[B].set(c_ctx)
    mod3 = _modulation(cond, w_ada, b_ada).reshape(n_cond, 1, 3 * D_MODEL)

    w_bf = jnp.concatenate(
        [w_in[:, 4 * D_MODEL:7 * D_MODEL], w_in[:, :4 * D_MODEL], w_in[:, 7 * D_MODEL:N_MAIN],
         w_in[:, N_MAIN + 4 * LORA:], w_in[:, N_MAIN:N_MAIN + 4 * LORA],
         jnp.zeros((D_MODEL, N_PAD - N_PROJ), w_in.dtype)], axis=1).astype(BF16)
    u = _in_projection(x, mod3, w_bf, lambda b: b, tm=min(T, 2048), tn=1536)
    u_ctx = _in_projection(ctx, mod3, w_bf, lambda b: B, tm=L, tn=1536)

    y_a = _neighbourhood_attention(u, u_ctx, _na_bias_table(na_rpb, rows))

    def lora_pad(w_up):
        out = jnp.zeros((2, 2 * LORA, D_MODEL), F32)
        for d in range(2):
            out = out.at[d, d * LORA:(d + 1) * LORA].set(w_up[d])
        return out.astype(BF16)

    params = (rw_mu, rw_w0, lora_pad(rw_w_up), rw_a0, lora_pad(rw_a_up), rw_k_k.reshape(1, D_MODEL),
              rw_k_a.reshape(1, D_MODEL), rw_r_k.reshape(1, D_MODEL))
    state0 = jnp.zeros((B, 2, HEAD_DIM, D_MODEL), F32)
    state_c = _rwkv_scan(u_ctx, state0, params, rotary=False, emit_y=False)[-1]
    y_f, y_b, bonus_f, bonus_b, _ = _rwkv_scan(u, state_c, params, rotary=True, emit_y=True)

    return _final_stage(x, y_a, y_f, y_b, bonus_f, bonus_b, u, mod3, rw_gn_g.reshape(1, D_MODEL),
                        rw_gn_b.reshape(1, D_MODEL), w_branch_a.astype(BF16), w_branch_b.astype(BF16),
                        w_out.astype(BF16), ln_g.reshape(1, D_MODEL), ln_b.reshape(1, D_MODEL),
                        tm=min(T, 512))


def kernel(x, c, ctx, c_ctx, w_ada, b_ada, w_in, na_rpb, rw_mu, rw_w0, rw_w_up, rw_a0, rw_a_up, rw_k_k, rw_k_a, rw_r_k, rw_gn_g, rw_gn_b, w_branch_a, w_branch_b, w_out, ln_g, ln_b):
    assert w_ada.shape[0] == DEPTH
    return _layer(x, c, ctx, c_ctx, w_ada[0], b_ada[0], w_in[0], na_rpb[0], rw_mu[0], rw_w0[0],
                  rw_w_up[0], rw_a0[0], rw_a_up[0], rw_k_k[0], rw_k_a[0], rw_r_k[0], rw_gn_g[0],
                  rw_gn_b[0], w_branch_a[0], w_branch_b[0], w_out[0], ln_g[0], ln_b[0])
```

```python
import functools
import itertools

import numpy as np
import jax
import jax.numpy as jnp
from jax import lax
from jax.experimental import pallas as pl
from jax.experimental.pallas import tpu as pltpu

F32 = jnp.float32
BF16 = jnp.bfloat16
HIGHEST = lax.Precision.HIGHEST

D_MODEL = 1024
GRID_W = 64
HEADS = 16
HEAD_DIM = 64
NA_MAX_ROWS = 8
NA_COLS = 16
LORA = 64
DEPTH = 1
ROPE_THETA = 10000.0
LN_EPS = 1e-5
GN_EPS = 64e-5
ALPHA = (2 * DEPTH) ** 0.25

LANES = 128
GROUP = 256
HEADS_PER_GROUP = GROUP // HEAD_DIM
N_GROUPS = D_MODEL // GROUP
CHUNK = 64
RWKV_BLOCK_CHUNKS = 8

COL_RKV = 0
COL_NA = 3 * D_MODEL
COL_ZB = 7 * D_MODEL
N_MAIN = 8 * D_MODEL
COL_MG = N_MAIN
COL_LORA = N_MAIN + 2 * D_MODEL
N_PROJ = COL_LORA + 4 * LORA
N_PAD = 21 * 512
VMEM_LIMIT = 56 * 1024 * 1024


def _cparams(sem):
    return pltpu.CompilerParams(dimension_semantics=sem, vmem_limit_bytes=VMEM_LIMIT)


def _silu(x):
    return x * jax.nn.sigmoid(x)


def _mod_kernel(c_ref, w_ref, b_ref, o_ref):
    s = _silu(c_ref[...])
    o_ref[...] = jnp.dot(s, w_ref[...], preferred_element_type=F32, precision=HIGHEST) + b_ref[...]


def _modulation(cond, w_ada, b_ada):
    rows = cond.shape[0]
    return pl.pallas_call(
        _mod_kernel,
        out_shape=jax.ShapeDtypeStruct((rows, 3 * D_MODEL), F32),
        grid=(3,),
        in_specs=[pl.BlockSpec((rows, D_MODEL), lambda j: (0, 0)),
                  pl.BlockSpec((D_MODEL, D_MODEL), lambda j: (0, j)),
                  pl.BlockSpec((1, D_MODEL), lambda j: (0, j))],
        out_specs=pl.BlockSpec((rows, D_MODEL), lambda j: (0, j)),
        compiler_params=_cparams(("arbitrary",)),
        name="ada_modulation",
    )(cond, w_ada, b_ada.reshape(1, 3 * D_MODEL))


def _inproj_kernel(x_ref, sh_ref, sc_ref, w_ref, o_ref, h_scr):
    @pl.when(pl.program_id(2) == 0)
    def _():
        x = x_ref[...]
        mu = jnp.mean(x, axis=-1, keepdims=True)
        xc = x - mu
        var = jnp.mean(xc * xc, axis=-1, keepdims=True)
        y = xc * lax.rsqrt(var + LN_EPS)
        h_scr[...] = (y * (1.0 + sc_ref[...]) + sh_ref[...]).astype(BF16)

    o_ref[...] = jnp.dot(h_scr[...], w_ref[...], preferred_element_type=F32).astype(BF16)


def _in_projection(x, mod3, w_bf, row_of_batch, tm, tn):
    B, T, _ = x.shape
    return pl.pallas_call(
        _inproj_kernel,
        out_shape=jax.ShapeDtypeStruct((B, T, N_PAD), BF16),
        grid=(B, T // tm, N_PAD // tn),
        in_specs=[pl.BlockSpec((None, tm, D_MODEL), lambda b, i, j: (b, i, 0)),
                  pl.BlockSpec((None, 1, D_MODEL), lambda b, i, j: (row_of_batch(b), 0, 0)),
                  pl.BlockSpec((None, 1, D_MODEL), lambda b, i, j: (row_of_batch(b), 0, 1)),
                  pl.BlockSpec((D_MODEL, tn), lambda b, i, j: (0, j))],
        out_specs=pl.BlockSpec((None, tm, tn), lambda b, i, j: (b, i, j)),
        scratch_shapes=[pltpu.VMEM((tm, D_MODEL), BF16)],
        compiler_params=_cparams(("arbitrary", "arbitrary", "arbitrary")),
        name="ln_mod_in_proj",
    )(x, mod3, mod3, w_bf)


NA_HEADS_PER_STEP = GROUP // HEAD_DIM
NA_ROWS_PER_ITER = 8
NA_ROW_GROUPS = 2


def _na_kernel(q_ref, k_ref, v_ref, z_ref, kc_ref, vc_ref, bias_ref, o_ref, *, rows):
    kh = min(NA_MAX_ROWS, rows)
    band = kh * GRID_W
    nh = NA_HEADS_PER_STEP
    lane = lax.broadcasted_iota(jnp.int32, (1, GROUP), 1)
    head_lanes = [(lane // HEAD_DIM) == h for h in range(nh)]
    kc = kc_ref[...]
    vc = vc_ref[...]
    nt = (((1,), (1,)), ((), ()))

    n_rows = NA_ROWS_PER_ITER if rows % NA_ROWS_PER_ITER == 0 else 1
    stack = nh * GRID_W

    def row_group(first_row, count, delay):
        for _ in range(delay):
            yield
        qs, s, vb, q0s = [], [], [], []
        for r in range(count):
            i = first_row + r
            rs = jnp.clip(i - kh // 2, 0, rows - kh)
            q0 = pl.multiple_of(i * GRID_W, GRID_W)
            k0 = pl.multiple_of(rs * GRID_W, GRID_W)
            q = q_ref[pl.ds(q0, GRID_W), :] * jnp.asarray(HEAD_DIM ** -0.5, BF16)
            q_st = jnp.concatenate([jnp.where(head_lanes[h], q, jnp.zeros_like(q)) for h in range(nh)], axis=0)
            s.append(lax.dot_general(q_st, k_ref[pl.ds(k0, band), :], nt, preferred_element_type=F32)
                     + bias_ref[:, i - rs].reshape(stack, band))
            vb.append(v_ref[pl.ds(k0, band), :])
            qs.append(q_st)
            q0s.append(q0)
        sc = lax.dot_general(jnp.concatenate(qs, axis=0), kc, nt, preferred_element_type=F32)
        yield
        p, pcs, denom = [], [], []
        for r in range(count):
            sc_r = sc[r * stack:(r + 1) * stack]
            m = jnp.maximum(jnp.max(s[r], axis=-1, keepdims=True), jnp.max(sc_r, axis=-1, keepdims=True))
            p_r = jnp.exp(s[r] - m)
            pc_r = jnp.exp(sc_r - m)
            denom.append(jnp.sum(p_r, axis=-1, keepdims=True) + jnp.sum(pc_r, axis=-1, keepdims=True))
            p.append(p_r.astype(BF16))
            pcs.append(pc_r.astype(BF16))
        yield
        o_ctx = jnp.dot(jnp.concatenate(pcs, axis=0), vc, preferred_element_type=F32)
        for r in range(count):
            o_all = (jnp.dot(p[r], vb[r], preferred_element_type=F32)
                     + o_ctx[r * stack:(r + 1) * stack]) / denom[r]
            o = o_all[(nh - 1) * GRID_W:]
            for h in range(nh - 2, -1, -1):
                o = jnp.where(head_lanes[h], o_all[h * GRID_W:(h + 1) * GRID_W], o)
            z = z_ref[pl.ds(q0s[r], GRID_W), :].astype(F32)
            o_ref[pl.ds(q0s[r], GRID_W), :] = (o * _silu(z)).astype(BF16)

    def rows_body(it, carry):
        per_group = n_rows // NA_ROW_GROUPS if n_rows % NA_ROW_GROUPS == 0 else n_rows
        groups = n_rows // per_group
        _interleave(*[row_group(it * n_rows + g * per_group, per_group, g) for g in range(groups)])
        return carry

    lax.fori_loop(0, rows // n_rows, rows_body, 0)


def _na_bias_table(rpb, rows):
    kh = min(NA_MAX_ROWS, rows)
    nh = NA_HEADS_PER_STEP
    n_dj = 2 * NA_COLS - 1
    cols = np.arange(GRID_W)
    cstart = np.clip(cols - NA_COLS // 2, 0, GRID_W - NA_COLS)
    col_mask = (cols[None, :] >= cstart[:, None]) & (cols[None, :] < cstart[:, None] + NA_COLS)
    dj = np.clip(cols[None, :] - cols[:, None] + NA_COLS - 1, 0, n_dj - 1)
    onehot = (np.arange(n_dj)[:, None] == dj.reshape(1, -1)).astype(np.float32)
    exp = jnp.dot(rpb.reshape(-1, n_dj), jnp.asarray(onehot), precision=HIGHEST)
    exp = exp.reshape(HEADS, 2 * NA_MAX_ROWS - 1, GRID_W, GRID_W)
    exp = jnp.where(col_mask[None, None], exp, -jnp.inf).transpose(0, 2, 1, 3)
    lo = NA_MAX_ROWS - 1
    tab = jnp.stack([exp[:, :, lo - o: lo - o + kh].reshape(HEADS, GRID_W, kh * GRID_W) for o in range(kh)],
                    axis=1)
    return tab.reshape(HEADS // nh, nh, kh, GRID_W, kh * GRID_W)


def _neighbourhood_attention(u, u_ctx, bias_tab):
    B, T, _ = u.shape
    L = u_ctx.shape[1]
    rows = T // GRID_W
    kh = min(NA_MAX_ROWS, rows)
    blk = D_MODEL // GROUP
    first = COL_NA // GROUP

    def col(part):
        return pl.BlockSpec((None, T, GROUP), lambda b, g: (b, 0, first + part * blk + g))

    def col_ctx(part):
        return pl.BlockSpec((None, L, GROUP), lambda b, g: (b, 0, first + part * blk + g))

    return pl.pallas_call(
        functools.partial(_na_kernel, rows=rows),
        out_shape=jax.ShapeDtypeStruct((B, T, D_MODEL), BF16),
        grid=(B, blk),
        in_specs=[col(0), col(1), col(2), col(3), col_ctx(1), col_ctx(2),
                  pl.BlockSpec((None, NA_HEADS_PER_STEP, kh, GRID_W, kh * GRID_W), lambda b, g: (g, 0, 0, 0, 0))],
        out_specs=pl.BlockSpec((None, T, GROUP), lambda b, g: (b, 0, g)),
        compiler_params=_cparams(("arbitrary", "arbitrary")),
        name="neighbourhood_attention",
    )(u, u, u, u, u_ctx, u_ctx, bias_tab)


def _block_diag_masks():
    r = lax.broadcasted_iota(jnp.int32, (GROUP, GROUP), 0)
    c = lax.broadcasted_iota(jnp.int32, (GROUP, GROUP), 1)
    return (r // HEAD_DIM) == (c // HEAD_DIM)


def _head_sum(x, ones_bd):
    rows = x.shape[0]
    xb = x.astype(BF16)
    stacked = jnp.concatenate([xb[:, g * GROUP:(g + 1) * GROUP] for g in range(N_GROUPS)], axis=0)
    sums = jnp.dot(stacked, ones_bd, preferred_element_type=F32)
    return jnp.concatenate([sums[g * rows:(g + 1) * rows] for g in range(N_GROUPS)], axis=1)


def _rwkv_prepare(d, local, cidx, n_chunks, refs, prm, ones_bd, rotary, ops_ref, pc_ref):
    rkv_ref, prev_ref, next_ref, lo_ref, rope_ref, bonus_ref = refs
    mu_ref, w0_ref, wup_ref, a0_ref, aup_ref, kk_ref, ka_ref, rk_ref = prm
    C = CHUNK
    halo = prev_ref.shape[0]
    block = rkv_ref.shape[0]
    rows = pl.ds(pl.multiple_of(local * C, C), C)
    rows_before = pl.ds(pl.multiple_of(jnp.maximum(local * C - halo, 0), halo), halo)
    rows_after = pl.ds(pl.multiple_of(jnp.minimum(local * C + C, block - halo), halo), halo)
    chunk_id = jnp.full((halo, D_MODEL), local, jnp.int32)
    at_block_start = chunk_id == 0
    at_block_end = chunk_id == block // C - 1

    def store(name, value):
        ops_ref[d * len(OPERANDS) + OPERANDS.index(name)] = value.astype(BF16)

    tt = lax.broadcasted_iota(jnp.int32, (C, C + 2 * halo), 0)
    ss = lax.broadcasted_iota(jnp.int32, (C, C + 2 * halo), 1) - halo
    first = jnp.where(cidx > 0, -1, 0)
    last = jnp.where(cidx < n_chunks - 1, C, C - 1)
    nb = (jnp.abs(ss - tt) == 1) & (ss >= first) & (ss <= last)
    nb = jnp.where(nb, 1.0, 0.0).astype(BF16)

    def shifted(i):
        lanes = slice(i * D_MODEL, (i + 1) * D_MODEL)
        x = rkv_ref[rows, lanes]
        before = jnp.where(at_block_start, prev_ref[:, lanes], rkv_ref[rows_before, lanes])
        after = jnp.where(at_block_end, next_ref[:, lanes], rkv_ref[rows_after, lanes])
        both = jnp.dot(nb, jnp.concatenate([before, x, after], axis=0), preferred_element_type=F32)
        mu = mu_ref[i:i + 1, :]
        return (1.0 - mu) * x.astype(F32) + (0.5 * mu) * both

    r_s = shifted(0)
    yield
    k_s = shifted(1)
    yield
    v_s = shifted(2)
    store("v", v_s)
    yield

    if rotary:
        lane = lax.broadcasted_iota(jnp.int32, (1, D_MODEL), 1)
        low = (lane % 32) < 16
        cos_t = jnp.tile(rope_ref[rows, :LANES], (1, D_MODEL // LANES))
        sin_t = jnp.tile(rope_ref[rows, LANES:], (1, D_MODEL // LANES))

        def rope(x):
            partner = jnp.where(low, pltpu.roll(x, D_MODEL - 16, axis=1), pltpu.roll(x, 16, axis=1))
            return x * cos_t + partner * sin_t

        r_s = rope(r_s)
        yield
        k_s = rope(k_s)
        yield

    lo = lo_ref[rows, :]
    lw = w0_ref[d:d + 1, :] + jnp.dot(jnp.tanh(lo[:, :2 * LORA].astype(F32)).astype(BF16), wup_ref[d],
                                     preferred_element_type=F32)
    ld = (-np.exp(-0.5)) * jax.nn.sigmoid(lw)
    yield
    a = jax.nn.sigmoid(a0_ref[d:d + 1, :] + jnp.dot(lo[:, 2 * LORA:], aup_ref[d], preferred_element_type=F32))
    yield

    kk = k_s * kk_ref[...]
    kk = kk * jnp.minimum(lax.rsqrt(_head_sum(kk * kk, ones_bd)), 1e12)
    yield
    k_dir = k_s * (1.0 + (a - 1.0) * ka_ref[...])
    b_vec = kk * a
    bonus_ref[rows, :] = (_head_sum(r_s * rk_ref[...] * k_dir, ones_bd) * v_s).astype(bonus_ref.dtype)
    yield

    tt = lax.broadcasted_iota(jnp.int32, (C, C), 0)
    ss = lax.broadcasted_iota(jnp.int32, (C, C), 1)
    tri = jnp.where((ss <= tt) if d == 0 else (ss >= tt), 1.0, 0.0).astype(BF16)
    ld_hi = ld.astype(BF16)
    ld_lo = (ld - ld_hi.astype(F32)).astype(BF16)
    cl = (jnp.dot(tri, ld_hi, preferred_element_type=F32) + jnp.dot(tri, ld_lo, preferred_element_type=F32))
    cl_tot = cl[C - 1:C, :] if d == 0 else cl[0:1, :]
    yield
    e_neg = jnp.exp(-cl)
    p_c = jnp.exp(cl_tot)
    pc_ref[d] = jnp.broadcast_to(p_c, pc_ref.shape[1:])
    b_t = b_vec * e_neg
    store("b", b_t)
    store("bh", b_t * p_c)
    yield
    k_t = k_dir * e_neg
    store("k", k_t)
    store("kh", k_t * p_c)
    yield
    store("a", -kk * jnp.exp(cl - ld))
    store("r", r_s * jnp.exp(cl))


OPERANDS = ("a", "r", "b", "k", "bh", "kh", "v")


PREPARE_STEPS = 2


def _interleave(*gens, steps=None):
    live = [[gen, 1 if steps is None else steps[i]] for i, gen in enumerate(gens)]
    while live:
        for entry in list(live):
            for _ in range(entry[1]):
                try:
                    next(entry[0])
                except StopIteration:
                    live.remove(entry)
                    break


def _rwkv_chunk_matmuls(ops_ref, pc_ref, h_scr, yf_ref, yb_ref, local_f, local_b, same_head, emit_y):
    C = CHUNK
    n_ops = len(OPERANDS)
    n_g = 2 * N_GROUPS
    dirs = [g // N_GROUPS for g in range(n_g)]

    def lanes_of(g):
        return slice((g % N_GROUPS) * GROUP, (g % N_GROUPS + 1) * GROUP)

    def grp(name):
        i = OPERANDS.index(name)
        return [ops_ref[dirs[g] * n_ops + i, :, lanes_of(g)] for g in range(n_g)]

    t_i = lax.broadcasted_iota(jnp.int32, (C, GROUP), 0)
    s_i = lax.broadcasted_iota(jnp.int32, (C, GROUP), 1) % C
    before = [s_i < t_i, s_i > t_i]
    upto = [s_i <= t_i, s_i >= t_i]
    eye = jnp.where(t_i == s_i, 1.0, 0.0).astype(F32)
    nt = (((1,), (1,)), ((), ()))
    tn = (((0,), (0,)), ((), ()))

    lane_in_tile = lax.broadcasted_iota(jnp.int32, (C, LANES), 1)
    half_of_tile = [lane_in_tile < HEAD_DIM, lane_in_tile >= HEAD_DIM]

    def bd(x):
        zero_tile = jnp.zeros((C, LANES), x.dtype)
        row_blocks = []
        for h in range(HEADS_PER_GROUP):
            tile = h * HEAD_DIM // LANES
            kept = jnp.where(half_of_tile[h % (LANES // HEAD_DIM)], x[:, tile * LANES:(tile + 1) * LANES], zero_tile)
            row_blocks.append(jnp.concatenate(
                [kept if t == tile else zero_tile for t in range(GROUP // LANES)], axis=1))
        return jnp.concatenate(row_blocks, axis=0)

    def hdot(a_side, x):
        return jnp.dot(a_side, bd(x), preferred_element_type=F32)

    def masked(m, keep):
        return jnp.where(keep, m, 0.0).astype(BF16)

    at, rt, bt = grp("a"), grp("r"), grp("b")
    lhs = [jnp.concatenate([at[g], rt[g]], axis=0) for g in range(n_g)] if emit_y else at
    a_ab, a_rb, a_ak, a_rk = [], [], [], []
    for g in range(n_g):
        m_b = lax.dot_general(lhs[g], bd(bt[g]), nt, preferred_element_type=F32)
        a_ab.append(masked(m_b[:C], before[dirs[g]]))
        if emit_y:
            a_rb.append(masked(m_b[C:], upto[dirs[g]]))
    yield
    kt = grp("k")
    for g in range(n_g):
        m_k = lax.dot_general(lhs[g], bd(kt[g]), nt, preferred_element_type=F32)
        a_ak.append(masked(m_k[:C], before[dirs[g]]))
        if emit_y:
            a_rk.append(masked(m_k[C:], upto[dirs[g]]))
    yield
    vv = grp("v")
    w_1, y_acc = [], []
    for g in range(n_g):
        if emit_y:
            w_v = hdot(jnp.concatenate([a_ak[g], a_rk[g]], axis=0), vv[g])
            y_acc.append(w_v[C:])
        else:
            w_v = hdot(a_ak[g], vv[g])
        w_1.append(w_v[:C])
    yield

    pw = [hdot(a_ab[g], a_ab[g]).astype(BF16) for g in range(n_g)]
    inv = [eye + a_ab[g].astype(F32) for g in range(n_g)]
    yield
    n_factors = C.bit_length() - 1
    for j in range(1, n_factors):
        last = j == n_factors - 1
        for g in range(n_g):
            inv_bf = inv[g].astype(BF16)
            prod = hdot(inv_bf if last else jnp.concatenate([inv_bf, pw[g]], axis=0), pw[g])
            inv[g] = inv[g] + prod[:C]
            if not last:
                pw[g] = prod[C:].astype(BF16)
        yield
    inv = [inv[g].astype(BF16) for g in range(n_g)]

    w_all, h_dec = [], []
    for g in range(n_g):
        decay_rows = (eye * pc_ref[dirs[g], 0:1, lanes_of(g)]).astype(BF16)
        stack = jnp.concatenate([at[g]] + ([rt[g]] if emit_y else []) + [decay_rows], axis=0)
        s1 = hdot(stack, h_scr[:, g * GROUP:(g + 1) * GROUP].astype(BF16))
        w_all.append((s1[:C] + w_1[g]).astype(BF16))
        if emit_y:
            y_acc[g] = y_acc[g] + s1[C:2 * C]
        h_dec.append(s1[-C:])
    yield
    u = [hdot(inv[g], w_all[g]).astype(BF16) for g in range(n_g)]
    yield
    if emit_y:
        out_rows = [pl.ds(pl.multiple_of(local * C, C), C) for local in (local_f, local_b)]
        for g in range(n_g):
            y_ref = yf_ref if dirs[g] == 0 else yb_ref
            y_ref[out_rows[dirs[g]], lanes_of(g)] = (y_acc[g] + hdot(a_rb[g], u[g])).astype(y_ref.dtype)
    yield

    bh, kh = grp("bh"), grp("kh")
    lane_head = lax.broadcasted_iota(jnp.int32, (HEAD_DIM, GROUP), 1) // HEAD_DIM
    for g in range(n_g):
        full = lax.dot_general(jnp.concatenate([bh[g], kh[g]], axis=0),
                               jnp.concatenate([u[g], vv[g]], axis=0), tn, preferred_element_type=F32)
        upd = full[(HEADS_PER_GROUP - 1) * HEAD_DIM:]
        for j in range(HEADS_PER_GROUP - 2, -1, -1):
            upd = jnp.where(lane_head == j, full[j * HEAD_DIM:(j + 1) * HEAD_DIM], upd)
        h_scr[:, g * GROUP:(g + 1) * GROUP] = h_dec[g] + upd


def _rwkv_kernel(*refs, n_blocks, rotary, emit_y):
    per_dir = 5
    in_f, in_b = refs[:per_dir], refs[per_dir:2 * per_dir]
    prm = refs[2 * per_dir:2 * per_dir + 8]
    s0_ref = refs[2 * per_dir + 8]
    yf_ref, yb_ref, bonf_ref, bonb_ref, sout_ref, h_scr, ops_scr, pc_scr = refs[2 * per_dir + 9:]
    j = pl.program_id(1)
    per_block = in_f[0].shape[0] // CHUNK
    n_chunks = n_blocks * per_block

    @pl.when(j == 0)
    def _():
        h_scr[:, :D_MODEL] = s0_ref[0]
        h_scr[:, D_MODEL:] = s0_ref[1]

    if not emit_y:
        yf_ref[...] = jnp.zeros(yf_ref.shape, yf_ref.dtype)
        yb_ref[...] = jnp.zeros(yb_ref.shape, yb_ref.dtype)

    same_head = _block_diag_masks()
    ones_bd = jnp.where(same_head, 1.0, 0.0).astype(BF16)

    def local(d, s):
        return s if d == 0 else per_block - 1 - s

    def prepare(d, s, buf):
        block = j if d == 0 else n_blocks - 1 - j
        ins, bonus_ref = (in_f, bonf_ref) if d == 0 else (in_b, bonb_ref)
        return _rwkv_prepare(d, local(d, s), block * per_block + local(d, s), n_chunks, ins + (bonus_ref,), prm,
                             ones_bd, rotary, ops_scr.at[buf], pc_scr.at[buf])

    def matmuls(s, buf):
        return _rwkv_chunk_matmuls(ops_scr.at[buf], pc_scr.at[buf], h_scr, yf_ref, yb_ref,
                                   local(0, s), local(1, s), same_head, emit_y)

    _interleave(prepare(0, 0, 0), prepare(1, 0, 0))

    def chunk_pair(i, carry):
        s = 2 * i
        ahead = jnp.minimum(s + 2, per_block - 1)
        _interleave(matmuls(s, 0), itertools.chain(prepare(0, s + 1, 1), prepare(1, s + 1, 1)),
                    steps=(1, PREPARE_STEPS))
        _interleave(matmuls(s + 1, 1), itertools.chain(prepare(0, ahead, 0), prepare(1, ahead, 0)),
                    steps=(1, PREPARE_STEPS))
        return carry

    lax.fori_loop(0, per_block // 2, chunk_pair, 0)

    @pl.when(j == n_blocks - 1)
    def _():
        sout_ref[0] = h_scr[:, :D_MODEL]
        sout_ref[1] = h_scr[:, D_MODEL:]


def _rope_tables(T):
    half = HEAD_DIM // 2
    n_freq = half // 2
    t = np.arange(T)[:, None]
    lane = np.arange(LANES)[None, :]
    inv_freq = ROPE_THETA ** (-(np.arange(n_freq, dtype=np.float32)) / n_freq)
    pos = np.where((lane % HEAD_DIM) < half, t // GRID_W, t % GRID_W).astype(np.float32)
    ang = (pos * inv_freq[lane % n_freq].astype(np.float32)).astype(np.float32)
    sign = np.where((lane % half) < n_freq, -1.0, 1.0)
    return jnp.asarray(np.cos(ang), F32), jnp.asarray(np.sin(ang) * sign, F32)


def _rwkv_scan(u, state0, params, rotary, emit_y):
    B, T, _ = u.shape
    block = min(T, RWKV_BLOCK_CHUNKS * CHUNK)
    n_blocks = T // block
    assert T % block == 0 and (block // CHUNK) % 2 == 0
    halo = 16
    per = block // halo

    def block_of(d, j):
        return j if d == 0 else n_blocks - 1 - j

    def specs(d):
        rkv = COL_RKV // (3 * D_MODEL)
        main = pl.BlockSpec((None, block, 3 * D_MODEL), lambda b, j: (b, block_of(d, j), rkv))
        prev = pl.BlockSpec((None, halo, 3 * D_MODEL),
                            lambda b, j: (b, jnp.maximum(block_of(d, j) * per - 1, 0), rkv))
        nxt = pl.BlockSpec((None, halo, 3 * D_MODEL),
                           lambda b, j: (b, jnp.minimum((block_of(d, j) + 1) * per, T // halo - 1), rkv))
        lora = pl.BlockSpec((None, block, 4 * LORA), lambda b, j: (b, block_of(d, j), COL_LORA // (4 * LORA)))
        tab = pl.BlockSpec((block, 2 * LANES), lambda b, j: (block_of(d, j), 0))
        return [main, prev, nxt, lora, tab]

    def whole(shape):
        return pl.BlockSpec(shape, lambda b, c: (0,) * len(shape))

    rope_tab = jnp.concatenate(_rope_tables(T), axis=1)
    state = pl.BlockSpec((None, 2, HEAD_DIM, D_MODEL), lambda b, c: (b, 0, 0, 0))
    param_specs = [whole((3, D_MODEL)), whole((2, D_MODEL)), whole((2, 2 * LORA, D_MODEL)),
                   whole((2, D_MODEL)), whole((2, 2 * LORA, D_MODEL)),
                   whole((1, D_MODEL)), whole((1, D_MODEL)), whole((1, D_MODEL))]
    seq = jax.ShapeDtypeStruct((B, T, D_MODEL), BF16)
    out_f = pl.BlockSpec((None, block, D_MODEL), lambda b, j: (b, block_of(0, j), 0))
    out_b = pl.BlockSpec((None, block, D_MODEL), lambda b, j: (b, block_of(1, j), 0))
    seq_in = (u,) * 4 + (rope_tab,)
    return pl.pallas_call(
        functools.partial(_rwkv_kernel, n_blocks=n_blocks, rotary=rotary, emit_y=emit_y),
        out_shape=(seq, seq, seq, seq, jax.ShapeDtypeStruct((B, 2, HEAD_DIM, D_MODEL), F32)),
        grid=(B, n_blocks),
        in_specs=specs(0) + specs(1) + param_specs + [state],
        out_specs=(out_f, out_b, out_f, out_b, state),
        scratch_shapes=[pltpu.VMEM((HEAD_DIM, 2 * D_MODEL), F32),
                        pltpu.VMEM((2, 2 * len(OPERANDS), CHUNK, D_MODEL), BF16),
                        pltpu.VMEM((2, 2, 8, D_MODEL), F32)],
        compiler_params=_cparams(("arbitrary", "arbitrary")),
        name="rwkv7_chunk_scan_rot" if rotary else "rwkv7_chunk_scan_ctx",
    )(*seq_in, *seq_in, *params, state0)


def _final_kernel(x_ref, ya_ref, yf_ref, yb_ref, bf_ref, bb_ref, z_ref, ga_ref, gb_ref, gate_ref,
                  gng_ref, gnb_ref, wa_ref, wb_ref, wo_ref, lng_ref, lnb_ref, o_ref):
    same_head = _block_diag_masks()
    ones_bd = jnp.where(same_head, 1.0, 0.0).astype(BF16)
    y = yf_ref[...].astype(F32) + yb_ref[...].astype(F32)
    mu = _head_sum(y, ones_bd) * (1.0 / HEAD_DIM)
    yc = y - mu
    var = _head_sum(yc * yc, ones_bd) * (1.0 / HEAD_DIM)
    yn = yc * lax.rsqrt(var + GN_EPS) * gng_ref[...] + gnb_ref[...]
    yn = yn + bf_ref[...].astype(F32) + bb_ref[...].astype(F32)
    y_b = (yn * _silu(z_ref[...].astype(F32))).astype(BF16)
    p_a = jnp.dot(ya_ref[...], wa_ref[...], preferred_element_type=F32)
    p_b = jnp.dot(y_b, wb_ref[...], preferred_element_type=F32)
    merged = (jax.nn.sigmoid(ga_ref[...].astype(F32)) * p_a
              + jax.nn.sigmoid(gb_ref[...].astype(F32)) * p_b)
    out = jnp.dot(merged.astype(BF16), wo_ref[...], preferred_element_type=F32)
    t = ALPHA * x_ref[...] + gate_ref[...] * out
    m = jnp.mean(t, axis=-1, keepdims=True)
    tc = t - m
    v = jnp.mean(tc * tc, axis=-1, keepdims=True)
    o_ref[...] = tc * lax.rsqrt(v + LN_EPS) * lng_ref[...] + lnb_ref[...]


def _final_stage(x, y_a, y_f, y_b, bonus_f, bonus_b, u, mod3, gn_g, gn_b, wa, wb, wo, ln_g, ln_b, tm):
    B, T, _ = x.shape

    def tok():
        return pl.BlockSpec((None, tm, D_MODEL), lambda b, i: (b, i, 0))

    def ucol(blk):
        return pl.BlockSpec((None, tm, D_MODEL), lambda b, i: (b, i, blk))

    def vec():
        return pl.BlockSpec((1, D_MODEL), lambda b, i: (0, 0))

    def mat():
        return pl.BlockSpec((D_MODEL, D_MODEL), lambda b, i: (0, 0))

    return pl.pallas_call(
        _final_kernel,
        out_shape=jax.ShapeDtypeStruct((B, T, D_MODEL), F32),
        grid=(B, T // tm),
        in_specs=[tok(), tok(), tok(), tok(), tok(), tok(), ucol(7), ucol(8), ucol(9),
                  pl.BlockSpec((None, 1, D_MODEL), lambda b, i: (b, 0, 2)),
                  vec(), vec(), mat(), mat(), mat(), vec(), vec()],
        out_specs=tok(),
        compiler_params=_cparams(("arbitrary", "arbitrary")),
        name="readout_merge_out_proj",
    )(x, y_a, y_f, y_b, bonus_f, bonus_b, u, u, u, mod3, gn_g, gn_b, wa, wb, wo, ln_g, ln_b)


def _layer(x, c, ctx, c_ctx, w_ada, b_ada, w_in, na_rpb, rw_mu, rw_w0, rw_w_up, rw_a0, rw_a_up,
           rw_k_k, rw_k_a, rw_r_k, rw_gn_g, rw_gn_b, w_branch_a, w_branch_b, w_out, ln_g, ln_b):
    B, T, _ = x.shape
    L = ctx.shape[1]
    rows = T // GRID_W

    n_cond = -(-(B + 1) // 8) * 8
    cond = jnp.zeros((n_cond, D_MODEL), F32).at[:B].set(c).at[B].set(c_ctx)
    mod3 = _modulation(cond, w_ada, b_ada).reshape(n_cond, 1, 3 * D_MODEL)

    w_bf = jnp.concatenate(
        [w_in[:, 4 * D_MODEL:7 * D_MODEL], w_in[:, :4 * D_MODEL], w_in[:, 7 * D_MODEL:N_MAIN],
         w_in[:, N_MAIN + 4 * LORA:], w_in[:, N_MAIN:N_MAIN + 4 * LORA],
         jnp.zeros((D_MODEL, N_PAD - N_PROJ), w_in.dtype)], axis=1).astype(BF16)
    u = _in_projection(x, mod3, w_bf, lambda b: b, tm=min(T, 2048), tn=1536)
    u_ctx = _in_projection(ctx, mod3, w_bf, lambda b: B, tm=L, tn=1536)

    y_a = _neighbourhood_attention(u, u_ctx, _na_bias_table(na_rpb, rows))

    def lora_pad(w_up):
        out = jnp.zeros((2, 2 * LORA, D_MODEL), F32)
        for d in range(2):
            out = out.at[d, d * LORA:(d + 1) * LORA].set(w_up[d])
        return out.astype(BF16)

    params = (rw_mu, rw_w0, lora_pad(rw_w_up), rw_a0, lora_pad(rw_a_up), rw_k_k.reshape(1, D_MODEL),
              rw_k_a.reshape(1, D_MODEL), rw_r_k.reshape(1, D_MODEL))
    state0 = jnp.zeros((B, 2, HEAD_DIM, D_MODEL), F32)
    state_c = _rwkv_scan(u_ctx, state0, params, rotary=False, emit_y=False)[-1]
    y_f, y_b, bonus_f, bonus_b, _ = _rwkv_scan(u, state_c, params, rotary=True, emit_y=True)

    return _final_stage(x, y_a, y_f, y_b, bonus_f, bonus_b, u, mod3, rw_gn_g.reshape(1, D_MODEL),
                        rw_gn_b.reshape(1, D_MODEL), w_branch_a.astype(BF16), w_branch_b.astype(BF16),
                        w_out.astype(BF16), ln_g.reshape(1, D_MODEL), ln_b.reshape(1, D_MODEL),
                        tm=min(T, 512))


def kernel(x, c, ctx, c_ctx, w_ada, b_ada, w_in, na_rpb, rw_mu, rw_w0, rw_w_up, rw_a0, rw_a_up, rw_k_k, rw_k_a, rw_r_k, rw_gn_g, rw_gn_b, w_branch_a, w_branch_b, w_out, ln_g, ln_b):
    assert w_ada.shape[0] == DEPTH
    return _layer(x, c, ctx, c_ctx, w_ada[0], b_ada[0], w_in[0], na_rpb[0], rw_mu[0], rw_w0[0],
                  rw_w_up[0], rw_a0[0], rw_a_up[0], rw_k_k[0], rw_k_a[0], rw_r_k[0], rw_gn_g[0],
                  rw_gn_b[0], w_branch_a[0], w_branch_b[0], w_out[0], ln_g[0], ln_b[0])
```

```python
import functools
import itertools

import numpy as np
import jax
import jax.numpy as jnp
from jax import lax
from jax.experimental import pallas as pl
from jax.experimental.pallas import tpu as pltpu

F32 = jnp.float32
BF16 = jnp.bfloat16
HIGHEST = lax.Precision.HIGHEST

D_MODEL = 1024
GRID_W = 64
HEADS = 16
HEAD_DIM = 64
NA_MAX_ROWS = 8
NA_COLS = 16
LORA = 64
DEPTH = 1
ROPE_THETA = 10000.0
LN_EPS = 1e-5
GN_EPS = 64e-5
ALPHA = (2 * DEPTH) ** 0.25
LOG2_E = float(np.log2(np.e))

LANES = 128
GROUP = 256
HEADS_PER_GROUP = GROUP // HEAD_DIM
N_GROUPS = D_MODEL // GROUP
CHUNK = 64
RWKV_BLOCK_CHUNKS = 8

COL_RKV = 0
COL_NA = 3 * D_MODEL
COL_ZB = 7 * D_MODEL
N_MAIN = 8 * D_MODEL
COL_MG = N_MAIN
COL_LORA = N_MAIN + 2 * D_MODEL
N_PROJ = COL_LORA + 4 * LORA
N_PAD = 21 * 512
VMEM_LIMIT = 56 * 1024 * 1024


def _cparams(sem):
    return pltpu.CompilerParams(dimension_semantics=sem, vmem_limit_bytes=VMEM_LIMIT)


def _silu(x):
    return x * jax.nn.sigmoid(x)


def _mod_kernel(c_ref, w_ref, b_ref, o_ref):
    s = _silu(c_ref[...])
    o_ref[...] = jnp.dot(s, w_ref[...], preferred_element_type=F32, precision=HIGHEST) + b_ref[...]


def _modulation(cond, w_ada, b_ada):
    rows = cond.shape[0]
    return pl.pallas_call(
        _mod_kernel,
        out_shape=jax.ShapeDtypeStruct((rows, 3 * D_MODEL), F32),
        grid=(3,),
        in_specs=[pl.BlockSpec((rows, D_MODEL), lambda j: (0, 0)),
                  pl.BlockSpec((D_MODEL, D_MODEL), lambda j: (0, j)),
                  pl.BlockSpec((1, D_MODEL), lambda j: (0, j))],
        out_specs=pl.BlockSpec((rows, D_MODEL), lambda j: (0, j)),
        compiler_params=_cparams(("arbitrary",)),
        name="ada_modulation",
    )(cond, w_ada, b_ada.reshape(1, 3 * D_MODEL))


def _inproj_kernel(x_ref, sh_ref, sc_ref, w_ref, o_ref, h_scr):
    @pl.when(pl.program_id(2) == 0)
    def _():
        x = x_ref[...]
        mu = jnp.mean(x, axis=-1, keepdims=True)
        xc = x - mu
        var = jnp.mean(xc * xc, axis=-1, keepdims=True)
        y = xc * lax.rsqrt(var + LN_EPS)
        h_scr[...] = (y * (1.0 + sc_ref[...]) + sh_ref[...]).astype(BF16)

    o_ref[...] = jnp.dot(h_scr[...], w_ref[...], preferred_element_type=F32).astype(BF16)


def _in_projection(x, mod3, w_bf, row_of_batch, tm, tn):
    B, T, _ = x.shape
    return pl.pallas_call(
        _inproj_kernel,
        out_shape=jax.ShapeDtypeStruct((B, T, N_PAD), BF16),
        grid=(B, T // tm, N_PAD // tn),
        in_specs=[pl.BlockSpec((None, tm, D_MODEL), lambda b, i, j: (b, i, 0)),
                  pl.BlockSpec((None, 1, D_MODEL), lambda b, i, j: (row_of_batch(b), 0, 0)),
                  pl.BlockSpec((None, 1, D_MODEL), lambda b, i, j: (row_of_batch(b), 0, 1)),
                  pl.BlockSpec((D_MODEL, tn), lambda b, i, j: (0, j))],
        out_specs=pl.BlockSpec((None, tm, tn), lambda b, i, j: (b, i, j)),
        scratch_shapes=[pltpu.VMEM((tm, D_MODEL), BF16)],
        compiler_params=_cparams(("arbitrary", "arbitrary", "arbitrary")),
        name="ln_mod_in_proj",
    )(x, mod3, mod3, w_bf)


NA_HEADS_PER_STEP = GROUP // HEAD_DIM
NA_ROWS_PER_ITER = 8
NA_ROW_GROUPS = 2


def _na_kernel(q_ref, k_ref, v_ref, z_ref, kc_ref, vc_ref, bias_ref, o_ref, *, rows):
    kh = min(NA_MAX_ROWS, rows)
    band = kh * GRID_W
    nh = NA_HEADS_PER_STEP
    lane = lax.broadcasted_iota(jnp.int32, (1, GROUP), 1)
    head_lanes = [(lane // HEAD_DIM) == h for h in range(nh)]
    kc = kc_ref[...]
    vc = vc_ref[...]
    nt = (((1,), (1,)), ((), ()))

    n_rows = NA_ROWS_PER_ITER if rows % NA_ROWS_PER_ITER == 0 else 1
    stack = nh * GRID_W

    def row_group(first_row, count, delay):
        for _ in range(delay):
            yield
        qs, s, vb, q0s = [], [], [], []
        for r in range(count):
            i = first_row + r
            rs = jnp.clip(i - kh // 2, 0, rows - kh)
            q0 = pl.multiple_of(i * GRID_W, GRID_W)
            k0 = pl.multiple_of(rs * GRID_W, GRID_W)
            q = q_ref[pl.ds(q0, GRID_W), :] * jnp.asarray(HEAD_DIM ** -0.5, BF16)
            q_st = jnp.concatenate([jnp.where(head_lanes[h], q, jnp.zeros_like(q)) for h in range(nh)], axis=0)
            s.append(lax.dot_general(q_st, k_ref[pl.ds(k0, band), :], nt, preferred_element_type=F32)
                     + bias_ref[:, i - rs].reshape(stack, band))
            vb.append(v_ref[pl.ds(k0, band), :])
            qs.append(q_st)
            q0s.append(q0)
        sc = lax.dot_general(jnp.concatenate(qs, axis=0), kc, nt, preferred_element_type=F32)
        yield
        p, pcs, denom = [], [], []
        for r in range(count):
            sc_r = sc[r * stack:(r + 1) * stack]
            m = jnp.maximum(jnp.max(s[r], axis=-1, keepdims=True), jnp.max(sc_r, axis=-1, keepdims=True))
            p_r = jnp.exp2(s[r] - m)
            pc_r = jnp.exp2(sc_r - m)
            denom.append(jnp.sum(p_r, axis=-1, keepdims=True) + jnp.sum(pc_r, axis=-1, keepdims=True))
            p.append(p_r.astype(BF16))
            pcs.append(pc_r.astype(BF16))
        yield
        o_ctx = jnp.dot(jnp.concatenate(pcs, axis=0), vc, preferred_element_type=F32)
        for r in range(count):
            o_all = (jnp.dot(p[r], vb[r], preferred_element_type=F32)
                     + o_ctx[r * stack:(r + 1) * stack]) / denom[r]
            o = o_all[(nh - 1) * GRID_W:]
            for h in range(nh - 2, -1, -1):
                o = jnp.where(head_lanes[h], o_all[h * GRID_W:(h + 1) * GRID_W], o)
            z = z_ref[pl.ds(q0s[r], GRID_W), :].astype(F32)
            o_ref[pl.ds(q0s[r], GRID_W), :] = (o * _silu(z)).astype(BF16)

    def rows_body(it, carry):
        per_group = n_rows // NA_ROW_GROUPS if n_rows % NA_ROW_GROUPS == 0 else n_rows
        groups = n_rows // per_group
        _interleave(*[row_group(it * n_rows + g * per_group, per_group, g) for g in range(groups)])
        return carry

    lax.fori_loop(0, rows // n_rows, rows_body, 0)


def _na_bias_table(rpb, rows):
    kh = min(NA_MAX_ROWS, rows)
    nh = NA_HEADS_PER_STEP
    n_dj = 2 * NA_COLS - 1
    cols = np.arange(GRID_W)
    cstart = np.clip(cols - NA_COLS // 2, 0, GRID_W - NA_COLS)
    col_mask = (cols[None, :] >= cstart[:, None]) & (cols[None, :] < cstart[:, None] + NA_COLS)
    dj = np.clip(cols[None, :] - cols[:, None] + NA_COLS - 1, 0, n_dj - 1)
    onehot = (np.arange(n_dj)[:, None] == dj.reshape(1, -1)).astype(np.float32)
    exp = jnp.dot(rpb.reshape(-1, n_dj), jnp.asarray(onehot), precision=HIGHEST)
    exp = exp.reshape(HEADS, 2 * NA_MAX_ROWS - 1, GRID_W, GRID_W)
    exp = jnp.where(col_mask[None, None], exp * LOG2_E, -jnp.inf).transpose(0, 2, 1, 3)
    lo = NA_MAX_ROWS - 1
    tab = jnp.stack([exp[:, :, lo - o: lo - o + kh].reshape(HEADS, GRID_W, kh * GRID_W) for o in range(kh)],
                    axis=1)
    return tab.reshape(HEADS // nh, nh, kh, GRID_W, kh * GRID_W)


def _neighbourhood_attention(u, u_ctx, bias_tab):
    B, T, _ = u.shape
    L = u_ctx.shape[1]
    rows = T // GRID_W
    kh = min(NA_MAX_ROWS, rows)
    blk = D_MODEL // GROUP
    first = COL_NA // GROUP

    def col(part):
        return pl.BlockSpec((None, T, GROUP), lambda b, g: (b, 0, first + part * blk + g))

    def col_ctx(part):
        return pl.BlockSpec((None, L, GROUP), lambda b, g: (b, 0, first + part * blk + g))

    return pl.pallas_call(
        functools.partial(_na_kernel, rows=rows),
        out_shape=jax.ShapeDtypeStruct((B, T, D_MODEL), BF16),
        grid=(B, blk),
        in_specs=[col(0), col(1), col(2), col(3), col_ctx(1), col_ctx(2),
                  pl.BlockSpec((None, NA_HEADS_PER_STEP, kh, GRID_W, kh * GRID_W), lambda b, g: (g, 0, 0, 0, 0))],
        out_specs=pl.BlockSpec((None, T, GROUP), lambda b, g: (b, 0, g)),
        compiler_params=_cparams(("arbitrary", "arbitrary")),
        name="neighbourhood_attention",
    )(u, u, u, u, u_ctx, u_ctx, bias_tab)


def _block_diag_masks():
    r = lax.broadcasted_iota(jnp.int32, (GROUP, GROUP), 0)
    c = lax.broadcasted_iota(jnp.int32, (GROUP, GROUP), 1)
    return (r // HEAD_DIM) == (c // HEAD_DIM)


def _head_sum(x, ones_bd):
    rows = x.shape[0]
    xb = x.astype(BF16)
    stacked = jnp.concatenate([xb[:, g * GROUP:(g + 1) * GROUP] for g in range(N_GROUPS)], axis=0)
    sums = jnp.dot(stacked, ones_bd, preferred_element_type=F32)
    return jnp.concatenate([sums[g * rows:(g + 1) * rows] for g in range(N_GROUPS)], axis=1)


def _rwkv_prepare(d, local, cidx, n_chunks, refs, prm, ones_bd, rotary, ops_ref, pc_ref):
    rkv_ref, prev_ref, next_ref, lo_ref, rope_ref, bonus_ref = refs
    mu_ref, w0_ref, wup_ref, a0_ref, aup_ref, kk_ref, ka_ref, rk_ref = prm
    C = CHUNK
    halo = prev_ref.shape[0]
    block = rkv_ref.shape[0]
    rows = pl.ds(pl.multiple_of(local * C, C), C)
    rows_before = pl.ds(pl.multiple_of(jnp.maximum(local * C - halo, 0), halo), halo)
    rows_after = pl.ds(pl.multiple_of(jnp.minimum(local * C + C, block - halo), halo), halo)
    chunk_id = jnp.full((halo, D_MODEL), local, jnp.int32)
    at_block_start = chunk_id == 0
    at_block_end = chunk_id == block // C - 1

    def store(name, value):
        ops_ref[d * len(OPERANDS) + OPERANDS.index(name)] = value.astype(BF16)

    tt = lax.broadcasted_iota(jnp.int32, (C, C + 2 * halo), 0)
    ss = lax.broadcasted_iota(jnp.int32, (C, C + 2 * halo), 1) - halo
    first = jnp.where(cidx > 0, -1, 0)
    last = jnp.where(cidx < n_chunks - 1, C, C - 1)
    nb = (jnp.abs(ss - tt) == 1) & (ss >= first) & (ss <= last)
    nb = jnp.where(nb, 1.0, 0.0).astype(BF16)

    def shifted(i):
        lanes = slice(i * D_MODEL, (i + 1) * D_MODEL)
        x = rkv_ref[rows, lanes]
        before = jnp.where(at_block_start, prev_ref[:, lanes], rkv_ref[rows_before, lanes])
        after = jnp.where(at_block_end, next_ref[:, lanes], rkv_ref[rows_after, lanes])
        both = jnp.dot(nb, jnp.concatenate([before, x, after], axis=0), preferred_element_type=F32)
        mu = mu_ref[i:i + 1, :]
        return (1.0 - mu) * x.astype(F32) + (0.5 * mu) * both

    r_s = shifted(0)
    yield
    k_s = shifted(1)
    yield
    v_s = shifted(2)
    store("v", v_s)
    yield

    if rotary:
        lane = lax.broadcasted_iota(jnp.int32, (1, D_MODEL), 1)
        low = (lane % 32) < 16
        cos_t = jnp.tile(rope_ref[rows, :LANES], (1, D_MODEL // LANES))
        sin_t = jnp.tile(rope_ref[rows, LANES:], (1, D_MODEL // LANES))

        def rope(x):
            partner = jnp.where(low, pltpu.roll(x, D_MODEL - 16, axis=1), pltpu.roll(x, 16, axis=1))
            return x * cos_t + partner * sin_t

        r_s = rope(r_s)
        yield
        k_s = rope(k_s)
        yield

    lo = lo_ref[rows, :]
    lw = w0_ref[d:d + 1, :] + jnp.dot(jnp.tanh(lo[:, :2 * LORA].astype(F32)).astype(BF16), wup_ref[d],
                                     preferred_element_type=F32)
    ld = (-np.exp(-0.5) * LOG2_E) * jax.nn.sigmoid(lw)
    yield
    a = jax.nn.sigmoid(a0_ref[d:d + 1, :] + jnp.dot(lo[:, 2 * LORA:], aup_ref[d], preferred_element_type=F32))
    yield

    kk = k_s * kk_ref[...]
    kk = kk * jnp.minimum(lax.rsqrt(_head_sum(kk * kk, ones_bd)), 1e12)
    yield
    k_dir = k_s * (1.0 + (a - 1.0) * ka_ref[...])
    b_vec = kk * a
    bonus_ref[rows, :] = (_head_sum(r_s * rk_ref[...] * k_dir, ones_bd) * v_s).astype(bonus_ref.dtype)
    yield

    tt = lax.broadcasted_iota(jnp.int32, (C, C), 0)
    ss = lax.broadcasted_iota(jnp.int32, (C, C), 1)
    tri = jnp.where((ss <= tt) if d == 0 else (ss >= tt), 1.0, 0.0).astype(BF16)
    ld_hi = ld.astype(BF16)
    ld_lo = (ld - ld_hi.astype(F32)).astype(BF16)
    cl = (jnp.dot(tri, ld_hi, preferred_element_type=F32) + jnp.dot(tri, ld_lo, preferred_element_type=F32))
    cl_tot = cl[C - 1:C, :] if d == 0 else cl[0:1, :]
    yield
    e_neg = jnp.exp2(-cl)
    p_c = jnp.exp2(cl_tot)
    pc_ref[d] = jnp.broadcast_to(p_c, pc_ref.shape[1:])
    b_t = b_vec * e_neg
    store("b", b_t)
    store("bh", b_t * p_c)
    yield
    k_t = k_dir * e_neg
    store("k", k_t)
    store("kh", k_t * p_c)
    yield
    store("a", -kk * jnp.exp2(cl - ld))
    store("r", r_s * jnp.exp2(cl))


OPERANDS = ("a", "r", "b", "k", "bh", "kh", "v")


PREPARE_STEPS = 2


def _interleave(*gens, steps=None):
    live = [[gen, 1 if steps is None else steps[i]] for i, gen in enumerate(gens)]
    while live:
        for entry in list(live):
            for _ in range(entry[1]):
                try:
                    next(entry[0])
                except StopIteration:
                    live.remove(entry)
                    break


def _rwkv_chunk_matmuls(ops_ref, pc_ref, h_scr, yf_ref, yb_ref, local_f, local_b, same_head, emit_y):
    C = CHUNK
    n_ops = len(OPERANDS)
    n_g = 2 * N_GROUPS
    dirs = [g // N_GROUPS for g in range(n_g)]

    def lanes_of(g):
        return slice((g % N_GROUPS) * GROUP, (g % N_GROUPS + 1) * GROUP)

    def grp(name):
        i = OPERANDS.index(name)
        return [ops_ref[dirs[g] * n_ops + i, :, lanes_of(g)] for g in range(n_g)]

    t_i = lax.broadcasted_iota(jnp.int32, (C, GROUP), 0)
    s_i = lax.broadcasted_iota(jnp.int32, (C, GROUP), 1) % C
    before = [s_i < t_i, s_i > t_i]
    upto = [s_i <= t_i, s_i >= t_i]
    eye = jnp.where(t_i == s_i, 1.0, 0.0).astype(F32)
    nt = (((1,), (1,)), ((), ()))
    tn = (((0,), (0,)), ((), ()))

    lane_in_tile = lax.broadcasted_iota(jnp.int32, (C, LANES), 1)
    half_of_tile = [lane_in_tile < HEAD_DIM, lane_in_tile >= HEAD_DIM]

    def bd(x):
        zero_tile = jnp.zeros((C, LANES), x.dtype)
        row_blocks = []
        for h in range(HEADS_PER_GROUP):
            tile = h * HEAD_DIM // LANES
            kept = jnp.where(half_of_tile[h % (LANES // HEAD_DIM)], x[:, tile * LANES:(tile + 1) * LANES], zero_tile)
            row_blocks.append(jnp.concatenate(
                [kept if t == tile else zero_tile for t in range(GROUP // LANES)], axis=1))
        return jnp.concatenate(row_blocks, axis=0)

    def hdot(a_side, x):
        return jnp.dot(a_side, bd(x), preferred_element_type=F32)

    def masked(m, keep):
        return jnp.where(keep, m, 0.0).astype(BF16)

    at, rt, bt = grp("a"), grp("r"), grp("b")
    lhs = [jnp.concatenate([at[g], rt[g]], axis=0) for g in range(n_g)] if emit_y else at
    a_ab, a_rb, a_ak, a_rk = [], [], [], []
    for g in range(n_g):
        m_b = lax.dot_general(lhs[g], bd(bt[g]), nt, preferred_element_type=F32)
        a_ab.append(masked(m_b[:C], before[dirs[g]]))
        if emit_y:
            a_rb.append(masked(m_b[C:], upto[dirs[g]]))
    yield
    kt = grp("k")
    for g in range(n_g):
        m_k = lax.dot_general(lhs[g], bd(kt[g]), nt, preferred_element_type=F32)
        a_ak.append(masked(m_k[:C], before[dirs[g]]))
        if emit_y:
            a_rk.append(masked(m_k[C:], upto[dirs[g]]))
    yield
    vv = grp("v")
    w_1, y_acc = [], []
    for g in range(n_g):
        if emit_y:
            w_v = hdot(jnp.concatenate([a_ak[g], a_rk[g]], axis=0), vv[g])
            y_acc.append(w_v[C:])
        else:
            w_v = hdot(a_ak[g], vv[g])
        w_1.append(w_v[:C])
    yield

    pw = [hdot(a_ab[g], a_ab[g]).astype(BF16) for g in range(n_g)]
    inv = [eye + a_ab[g].astype(F32) for g in range(n_g)]
    yield
    n_factors = C.bit_length() - 1
    for j in range(1, n_factors):
        last = j == n_factors - 1
        for g in range(n_g):
            inv_bf = inv[g].astype(BF16)
            prod = hdot(inv_bf if last else jnp.concatenate([inv_bf, pw[g]], axis=0), pw[g])
            inv[g] = inv[g] + prod[:C]
            if not last:
                pw[g] = prod[C:].astype(BF16)
        yield
    inv = [inv[g].astype(BF16) for g in range(n_g)]

    w_all, h_dec = [], []
    for g in range(n_g):
        decay_rows = (eye * pc_ref[dirs[g], 0:1, lanes_of(g)]).astype(BF16)
        stack = jnp.concatenate([at[g]] + ([rt[g]] if emit_y else []) + [decay_rows], axis=0)
        s1 = hdot(stack, h_scr[:, g * GROUP:(g + 1) * GROUP].astype(BF16))
        w_all.append((s1[:C] + w_1[g]).astype(BF16))
        if emit_y:
            y_acc[g] = y_acc[g] + s1[C:2 * C]
        h_dec.append(s1[-C:])
    yield
    u = [hdot(inv[g], w_all[g]).astype(BF16) for g in range(n_g)]
    yield
    if emit_y:
        out_rows = [pl.ds(pl.multiple_of(local * C, C), C) for local in (local_f, local_b)]
        for g in range(n_g):
            y_ref = yf_ref if dirs[g] == 0 else yb_ref
            y_ref[out_rows[dirs[g]], lanes_of(g)] = (y_acc[g] + hdot(a_rb[g], u[g])).astype(y_ref.dtype)
    yield

    bh, kh = grp("bh"), grp("kh")
    lane_head = lax.broadcasted_iota(jnp.int32, (HEAD_DIM, GROUP), 1) // HEAD_DIM
    for g in range(n_g):
        full = lax.dot_general(jnp.concatenate([bh[g], kh[g]], axis=0),
                               jnp.concatenate([u[g], vv[g]], axis=0), tn, preferred_element_type=F32)
        upd = full[(HEADS_PER_GROUP - 1) * HEAD_DIM:]
        for j in range(HEADS_PER_GROUP - 2, -1, -1):
            upd = jnp.where(lane_head == j, full[j * HEAD_DIM:(j + 1) * HEAD_DIM], upd)
        h_scr[:, g * GROUP:(g + 1) * GROUP] = h_dec[g] + upd


def _rwkv_kernel(*refs, n_blocks, rotary, emit_y):
    per_dir = 5
    in_f, in_b = refs[:per_dir], refs[per_dir:2 * per_dir]
    prm = refs[2 * per_dir:2 * per_dir + 8]
    s0_ref = refs[2 * per_dir + 8]
    yf_ref, yb_ref, bonf_ref, bonb_ref, sout_ref, h_scr, ops_scr, pc_scr = refs[2 * per_dir + 9:]
    j = pl.program_id(1)
    per_block = in_f[0].shape[0] // CHUNK
    n_chunks = n_blocks * per_block

    @pl.when(j == 0)
    def _():
        h_scr[:, :D_MODEL] = s0_ref[0]
        h_scr[:, D_MODEL:] = s0_ref[1]

    if not emit_y:
        yf_ref[...] = jnp.zeros(yf_ref.shape, yf_ref.dtype)
        yb_ref[...] = jnp.zeros(yb_ref.shape, yb_ref.dtype)

    same_head = _block_diag_masks()
    ones_bd = jnp.where(same_head, 1.0, 0.0).astype(BF16)

    def local(d, s):
        return s if d == 0 else per_block - 1 - s

    def prepare(d, s, buf):
        block = j if d == 0 else n_blocks - 1 - j
        ins, bonus_ref = (in_f, bonf_ref) if d == 0 else (in_b, bonb_ref)
        return _rwkv_prepare(d, local(d, s), block * per_block + local(d, s), n_chunks, ins + (bonus_ref,), prm,
                             ones_bd, rotary, ops_scr.at[buf], pc_scr.at[buf])

    def matmuls(s, buf):
        return _rwkv_chunk_matmuls(ops_scr.at[buf], pc_scr.at[buf], h_scr, yf_ref, yb_ref,
                                   local(0, s), local(1, s), same_head, emit_y)

    _interleave(prepare(0, 0, 0), prepare(1, 0, 0))

    def chunk_pair(i, carry):
        s = 2 * i
        ahead = jnp.minimum(s + 2, per_block - 1)
        _interleave(matmuls(s, 0), itertools.chain(prepare(0, s + 1, 1), prepare(1, s + 1, 1)),
                    steps=(1, PREPARE_STEPS))
        _interleave(matmuls(s + 1, 1), itertools.chain(prepare(0, ahead, 0), prepare(1, ahead, 0)),
                    steps=(1, PREPARE_STEPS))
        return carry

    lax.fori_loop(0, per_block // 2, chunk_pair, 0)

    @pl.when(j == n_blocks - 1)
    def _():
        sout_ref[0] = h_scr[:, :D_MODEL]
        sout_ref[1] = h_scr[:, D_MODEL:]


def _rope_tables(T):
    half = HEAD_DIM // 2
    n_freq = half // 2
    t = np.arange(T)[:, None]
    lane = np.arange(LANES)[None, :]
    inv_freq = ROPE_THETA ** (-(np.arange(n_freq, dtype=np.float32)) / n_freq)
    pos = np.where((lane % HEAD_DIM) < half, t // GRID_W, t % GRID_W).astype(np.float32)
    ang = (pos * inv_freq[lane % n_freq].astype(np.float32)).astype(np.float32)
    sign = np.where((lane % half) < n_freq, -1.0, 1.0)
    return jnp.asarray(np.cos(ang), F32), jnp.asarray(np.sin(ang) * sign, F32)


def _rwkv_scan(u, state0, params, rotary, emit_y):
    B, T, _ = u.shape
    block = min(T, RWKV_BLOCK_CHUNKS * CHUNK)
    n_blocks = T // block
    assert T % block == 0 and (block // CHUNK) % 2 == 0
    halo = 16
    per = block // halo

    def block_of(d, j):
        return j if d == 0 else n_blocks - 1 - j

    def specs(d):
        rkv = COL_RKV // (3 * D_MODEL)
        main = pl.BlockSpec((None, block, 3 * D_MODEL), lambda b, j: (b, block_of(d, j), rkv))
        prev = pl.BlockSpec((None, halo, 3 * D_MODEL),
                            lambda b, j: (b, jnp.maximum(block_of(d, j) * per - 1, 0), rkv))
        nxt = pl.BlockSpec((None, halo, 3 * D_MODEL),
                           lambda b, j: (b, jnp.minimum((block_of(d, j) + 1) * per, T // halo - 1), rkv))
        lora = pl.BlockSpec((None, block, 4 * LORA), lambda b, j: (b, block_of(d, j), COL_LORA // (4 * LORA)))
        tab = pl.BlockSpec((block, 2 * LANES), lambda b, j: (block_of(d, j), 0))
        return [main, prev, nxt, lora, tab]

    def whole(shape):
        return pl.BlockSpec(shape, lambda b, c: (0,) * len(shape))

    rope_tab = jnp.concatenate(_rope_tables(T), axis=1)
    state = pl.BlockSpec((None, 2, HEAD_DIM, D_MODEL), lambda b, c: (b, 0, 0, 0))
    param_specs = [whole((3, D_MODEL)), whole((2, D_MODEL)), whole((2, 2 * LORA, D_MODEL)),
                   whole((2, D_MODEL)), whole((2, 2 * LORA, D_MODEL)),
                   whole((1, D_MODEL)), whole((1, D_MODEL)), whole((1, D_MODEL))]
    seq = jax.ShapeDtypeStruct((B, T, D_MODEL), BF16)
    out_f = pl.BlockSpec((None, block, D_MODEL), lambda b, j: (b, block_of(0, j), 0))
    out_b = pl.BlockSpec((None, block, D_MODEL), lambda b, j: (b, block_of(1, j), 0))
    seq_in = (u,) * 4 + (rope_tab,)
    return pl.pallas_call(
        functools.partial(_rwkv_kernel, n_blocks=n_blocks, rotary=rotary, emit_y=emit_y),
        out_shape=(seq, seq, seq, seq, jax.ShapeDtypeStruct((B, 2, HEAD_DIM, D_MODEL), F32)),
        grid=(B, n_blocks),
        in_specs=specs(0) + specs(1) + param_specs + [state],
        out_specs=(out_f, out_b, out_f, out_b, state),
        scratch_shapes=[pltpu.VMEM((HEAD_DIM, 2 * D_MODEL), F32),
                        pltpu.VMEM((2, 2 * len(OPERANDS), CHUNK, D_MODEL), BF16),
                        pltpu.VMEM((2, 2, 8, D_MODEL), F32)],
        compiler_params=_cparams(("arbitrary", "arbitrary")),
        name="rwkv7_chunk_scan_rot" if rotary else "rwkv7_chunk_scan_ctx",
    )(*seq_in, *seq_in, *params, state0)


def _final_kernel(x_ref, ya_ref, yf_ref, yb_ref, bf_ref, bb_ref, z_ref, ga_ref, gb_ref, gate_ref,
                  gng_ref, gnb_ref, wa_ref, wb_ref, wo_ref, lng_ref, lnb_ref, o_ref):
    same_head = _block_diag_masks()
    ones_bd = jnp.where(same_head, 1.0, 0.0).astype(BF16)
    y = yf_ref[...].astype(F32) + yb_ref[...].astype(F32)
    mu = _head_sum(y, ones_bd) * (1.0 / HEAD_DIM)
    yc = y - mu
    var = _head_sum(yc * yc, ones_bd) * (1.0 / HEAD_DIM)
    yn = yc * lax.rsqrt(var + GN_EPS) * gng_ref[...] + gnb_ref[...]
    yn = yn + bf_ref[...].astype(F32) + bb_ref[...].astype(F32)
    y_b = (yn * _silu(z_ref[...].astype(F32))).astype(BF16)
    p_a = jnp.dot(ya_ref[...], wa_ref[...], preferred_element_type=F32)
    p_b = jnp.dot(y_b, wb_ref[...], preferred_element_type=F32)
    merged = (jax.nn.sigmoid(ga_ref[...].astype(F32)) * p_a
              + jax.nn.sigmoid(gb_ref[...].astype(F32)) * p_b)
    out = jnp.dot(merged.astype(BF16), wo_ref[...], preferred_element_type=F32)
    t = ALPHA * x_ref[...] + gate_ref[...] * out
    m = jnp.mean(t, axis=-1, keepdims=True)
    tc = t - m
    v = jnp.mean(tc * tc, axis=-1, keepdims=True)
    o_ref[...] = tc * lax.rsqrt(v + LN_EPS) * lng_ref[...] + lnb_ref[...]


def _final_stage(x, y_a, y_f, y_b, bonus_f, bonus_b, u, mod3, gn_g, gn_b, wa, wb, wo, ln_g, ln_b, tm):
    B, T, _ = x.shape

    def tok():
        return pl.BlockSpec((None, tm, D_MODEL), lambda b, i: (b, i, 0))

    def ucol(blk):
        return pl.BlockSpec((None, tm, D_MODEL), lambda b, i: (b, i, blk))

    def vec():
        return pl.BlockSpec((1, D_MODEL), lambda b, i: (0, 0))

    def mat():
        return pl.BlockSpec((D_MODEL, D_MODEL), lambda b, i: (0, 0))

    return pl.pallas_call(
        _final_kernel,
        out_shape=jax.ShapeDtypeStruct((B, T, D_MODEL), F32),
        grid=(B, T // tm),
        in_specs=[tok(), tok(), tok(), tok(), tok(), tok(), ucol(7), ucol(8), ucol(9),
                  pl.BlockSpec((None, 1, D_MODEL), lambda b, i: (b, 0, 2)),
                  vec(), vec(), mat(), mat(), mat(), vec(), vec()],
        out_specs=tok(),
        compiler_params=_cparams(("arbitrary", "arbitrary")),
        name="readout_merge_out_proj",
    )(x, y_a, y_f, y_b, bonus_f, bonus_b, u, u, u, mod3, gn_g, gn_b, wa, wb, wo, ln_g, ln_b)


def _layer(x, c, ctx, c_ctx, w_ada, b_ada, w_in, na_rpb, rw_mu, rw_w0, rw_w_up, rw_a0, rw_a_up,
           rw_k_k, rw_k_a, rw_r_k, rw_gn_g, rw_gn_b, w_branch_a, w_branch_b, w_out, ln_g, ln_b):
    B, T, _ = x.shape
    L = ctx.shape[1]
    rows = T // GRID_W

    n_cond = -(-(B + 1) // 8) * 8
    cond = jnp.zeros((n_cond, D_MODEL), F32).at[:B].set(c).at[B].set(c_ctx)
    mod3 = _modulation(cond, w_ada, b_ada).reshape(n_cond, 1, 3 * D_MODEL)

    w_bf = jnp.concatenate(
        [w_in[:, 4 * D_MODEL:7 * D_MODEL], w_in[:, :D_MODEL], w_in[:, D_MODEL:2 * D_MODEL] * LOG2_E,
         w_in[:, 2 * D_MODEL:4 * D_MODEL], w_in[:, 7 * D_MODEL:N_MAIN],
         w_in[:, N_MAIN + 4 * LORA:], w_in[:, N_MAIN:N_MAIN + 4 * LORA],
         jnp.zeros((D_MODEL, N_PAD - N_PROJ), w_in.dtype)], axis=1).astype(BF16)
    u = _in_projection(x, mod3, w_bf, lambda b: b, tm=min(T, 2048), tn=1536)
    u_ctx = _in_projection(ctx, mod3, w_bf, lambda b: B, tm=L, tn=1536)

    y_a = _neighbourhood_attention(u, u_ctx, _na_bias_table(na_rpb, rows))

    def lora_pad(w_up):
        out = jnp.zeros((2, 2 * LORA, D_MODEL), F32)
        for d in range(2):
            out = out.at[d, d * LORA:(d + 1) * LORA].set(w_up[d])
        return out.astype(BF16)

    params = (rw_mu, rw_w0, lora_pad(rw_w_up), rw_a0, lora_pad(rw_a_up), rw_k_k.reshape(1, D_MODEL),
              rw_k_a.reshape(1, D_MODEL), rw_r_k.reshape(1, D_MODEL))
    state0 = jnp.zeros((B, 2, HEAD_DIM, D_MODEL), F32)
    state_c = _rwkv_scan(u_ctx, state0, params, rotary=False, emit_y=False)[-1]
    y_f, y_b, bonus_f, bonus_b, _ = _rwkv_scan(u, state_c, params, rotary=True, emit_y=True)

    return _final_stage(x, y_a, y_f, y_b, bonus_f, bonus_b, u, mod3, rw_gn_g.reshape(1, D_MODEL),
                        rw_gn_b.reshape(1, D_MODEL), w_branch_a.astype(BF16), w_branch_b.astype(BF16),
                        w_out.astype(BF16), ln_g.reshape(1, D_MODEL), ln_b.reshape(1, D_MODEL),
                        tm=min(T, 512))


def kernel(x, c, ctx, c_ctx, w_ada, b_ada, w_in, na_rpb, rw_mu, rw_w0, rw_w_up, rw_a0, rw_a_up, rw_k_k, rw_k_a, rw_r_k, rw_gn_g, rw_gn_b, w_branch_a, w_branch_b, w_out, ln_g, ln_b):
    assert w_ada.shape[0] == DEPTH
    return _layer(x, c, ctx, c_ctx, w_ada[0], b_ada[0], w_in[0], na_rpb[0], rw_mu[0], rw_w0[0],
                  rw_w_up[0], rw_a0[0], rw_a_up[0], rw_k_k[0], rw_k_a[0], rw_r_k[0], rw_gn_g[0],
                  rw_gn_b[0], w_branch_a[0], w_branch_b[0], w_out[0], ln_g[0], ln_b[0])
```

```python
import functools
import itertools

import numpy as np
import jax
import jax.numpy as jnp
from jax import lax
from jax.experimental import pallas as pl
from jax.experimental.pallas import tpu as pltpu

F32 = jnp.float32
BF16 = jnp.bfloat16
HIGHEST = lax.Precision.HIGHEST

D_MODEL = 1024
GRID_W = 64
HEADS = 16
HEAD_DIM = 64
NA_MAX_ROWS = 8
NA_COLS = 16
LORA = 64
DEPTH = 1
ROPE_THETA = 10000.0
LN_EPS = 1e-5
GN_EPS = 64e-5
ALPHA = (2 * DEPTH) ** 0.25
LOG2_E = float(np.log2(np.e))

LANES = 128
GROUP = 256
HEADS_PER_GROUP = GROUP // HEAD_DIM
N_GROUPS = D_MODEL // GROUP
CHUNK = 64
RWKV_BLOCK_CHUNKS = 8

COL_RKV = 0
COL_NA = 3 * D_MODEL
COL_ZB = 7 * D_MODEL
N_MAIN = 8 * D_MODEL
COL_MG = N_MAIN
COL_LORA = N_MAIN + 2 * D_MODEL
N_PROJ = COL_LORA + 4 * LORA
N_PAD = 21 * 512
VMEM_LIMIT = 56 * 1024 * 1024


def _cparams(sem):
    return pltpu.CompilerParams(dimension_semantics=sem, vmem_limit_bytes=VMEM_LIMIT)


def _silu(x):
    return x * jax.nn.sigmoid(x)


def _mod_kernel(c_ref, w_ref, b_ref, o_ref):
    s = _silu(c_ref[...])
    o_ref[...] = jnp.dot(s, w_ref[...], preferred_element_type=F32, precision=HIGHEST) + b_ref[...]


def _modulation(cond, w_ada, b_ada):
    rows = cond.shape[0]
    return pl.pallas_call(
        _mod_kernel,
        out_shape=jax.ShapeDtypeStruct((rows, 3 * D_MODEL), F32),
        grid=(3,),
        in_specs=[pl.BlockSpec((rows, D_MODEL), lambda j: (0, 0)),
                  pl.BlockSpec((D_MODEL, D_MODEL), lambda j: (0, j)),
                  pl.BlockSpec((1, D_MODEL), lambda j: (0, j))],
        out_specs=pl.BlockSpec((rows, D_MODEL), lambda j: (0, j)),
        compiler_params=_cparams(("arbitrary",)),
        name="ada_modulation",
    )(cond, w_ada, b_ada.reshape(1, 3 * D_MODEL))


def _inproj_kernel(x_ref, sh_ref, sc_ref, w_ref, o_ref, h_scr):
    @pl.when(pl.program_id(2) == 0)
    def _():
        x = x_ref[...]
        mu = jnp.mean(x, axis=-1, keepdims=True)
        xc = x - mu
        var = jnp.mean(xc * xc, axis=-1, keepdims=True)
        y = xc * lax.rsqrt(var + LN_EPS)
        h_scr[...] = (y * (1.0 + sc_ref[...]) + sh_ref[...]).astype(BF16)

    o_ref[...] = jnp.dot(h_scr[...], w_ref[...], preferred_element_type=F32).astype(BF16)


def _in_projection(x, mod3, w_bf, row_of_batch, tm, tn):
    B, T, _ = x.shape
    return pl.pallas_call(
        _inproj_kernel,
        out_shape=jax.ShapeDtypeStruct((B, T, N_PAD), BF16),
        grid=(B, T // tm, N_PAD // tn),
        in_specs=[pl.BlockSpec((None, tm, D_MODEL), lambda b, i, j: (b, i, 0)),
                  pl.BlockSpec((None, 1, D_MODEL), lambda b, i, j: (row_of_batch(b), 0, 0)),
                  pl.BlockSpec((None, 1, D_MODEL), lambda b, i, j: (row_of_batch(b), 0, 1)),
                  pl.BlockSpec((D_MODEL, tn), lambda b, i, j: (0, j))],
        out_specs=pl.BlockSpec((None, tm, tn), lambda b, i, j: (b, i, j)),
        scratch_shapes=[pltpu.VMEM((tm, D_MODEL), BF16)],
        compiler_params=_cparams(("arbitrary", "arbitrary", "arbitrary")),
        name="ln_mod_in_proj",
    )(x, mod3, mod3, w_bf)


NA_HEADS_PER_STEP = GROUP // HEAD_DIM
NA_ROWS_PER_ITER = 8
NA_ROW_GROUPS = 2


def _na_kernel(q_ref, k_ref, v_ref, z_ref, kc_ref, vc_ref, bias_ref, o_ref, *, rows):
    kh = min(NA_MAX_ROWS, rows)
    band = kh * GRID_W
    nh = NA_HEADS_PER_STEP
    lane = lax.broadcasted_iota(jnp.int32, (1, GROUP), 1)
    head_lanes = [(lane // HEAD_DIM) == h for h in range(nh)]
    kc = kc_ref[...]
    vc = vc_ref[...]
    nt = (((1,), (1,)), ((), ()))

    n_rows = NA_ROWS_PER_ITER if rows % NA_ROWS_PER_ITER == 0 else 1
    stack = nh * GRID_W

    def row_group(first_row, count, delay):
        for _ in range(delay):
            yield
        qs, s, vb, q0s = [], [], [], []
        for r in range(count):
            i = first_row + r
            rs = jnp.clip(i - kh // 2, 0, rows - kh)
            q0 = pl.multiple_of(i * GRID_W, GRID_W)
            k0 = pl.multiple_of(rs * GRID_W, GRID_W)
            q = q_ref[pl.ds(q0, GRID_W), :] * jnp.asarray(HEAD_DIM ** -0.5, BF16)
            q_st = jnp.concatenate([jnp.where(head_lanes[h], q, jnp.zeros_like(q)) for h in range(nh)], axis=0)
            s.append(lax.dot_general(q_st, k_ref[pl.ds(k0, band), :], nt, preferred_element_type=F32)
                     + bias_ref[:, i - rs].reshape(stack, band))
            vb.append(v_ref[pl.ds(k0, band), :])
            qs.append(q_st)
            q0s.append(q0)
        sc = lax.dot_general(jnp.concatenate(qs, axis=0), kc, nt, preferred_element_type=F32)
        yield
        p, pcs, denom = [], [], []
        for r in range(count):
            sc_r = sc[r * stack:(r + 1) * stack]
            m = jnp.maximum(jnp.max(s[r], axis=-1, keepdims=True), jnp.max(sc_r, axis=-1, keepdims=True))
            p_r = jnp.exp2(s[r] - m)
            pc_r = jnp.exp2(sc_r - m)
            denom.append(jnp.sum(p_r, axis=-1, keepdims=True) + jnp.sum(pc_r, axis=-1, keepdims=True))
            p.append(p_r.astype(BF16))
            pcs.append(pc_r.astype(BF16))
        yield
        o_ctx = jnp.dot(jnp.concatenate(pcs, axis=0), vc, preferred_element_type=F32)
        for r in range(count):
            o_all = (jnp.dot(p[r], vb[r], preferred_element_type=F32)
                     + o_ctx[r * stack:(r + 1) * stack]) / denom[r]
            o = o_all[(nh - 1) * GRID_W:]
            for h in range(nh - 2, -1, -1):
                o = jnp.where(head_lanes[h], o_all[h * GRID_W:(h + 1) * GRID_W], o)
            z = z_ref[pl.ds(q0s[r], GRID_W), :].astype(F32)
            o_ref[pl.ds(q0s[r], GRID_W), :] = (o * _silu(z)).astype(BF16)

    def rows_body(it, carry):
        per_group = n_rows // NA_ROW_GROUPS if n_rows % NA_ROW_GROUPS == 0 else n_rows
        groups = n_rows // per_group
        _interleave(*[row_group(it * n_rows + g * per_group, per_group, g) for g in range(groups)])
        return carry

    lax.fori_loop(0, rows // n_rows, rows_body, 0)


def _na_bias_table(rpb, rows):
    kh = min(NA_MAX_ROWS, rows)
    nh = NA_HEADS_PER_STEP
    n_dj = 2 * NA_COLS - 1
    cols = np.arange(GRID_W)
    cstart = np.clip(cols - NA_COLS // 2, 0, GRID_W - NA_COLS)
    col_mask = (cols[None, :] >= cstart[:, None]) & (cols[None, :] < cstart[:, None] + NA_COLS)
    dj = np.clip(cols[None, :] - cols[:, None] + NA_COLS - 1, 0, n_dj - 1)
    onehot = (np.arange(n_dj)[:, None] == dj.reshape(1, -1)).astype(np.float32)
    exp = jnp.dot(rpb.reshape(-1, n_dj), jnp.asarray(onehot), precision=HIGHEST)
    exp = exp.reshape(HEADS, 2 * NA_MAX_ROWS - 1, GRID_W, GRID_W)
    exp = jnp.where(col_mask[None, None], exp * LOG2_E, -jnp.inf).transpose(0, 2, 1, 3)
    lo = NA_MAX_ROWS - 1
    tab = jnp.stack([exp[:, :, lo - o: lo - o + kh].reshape(HEADS, GRID_W, kh * GRID_W) for o in range(kh)],
                    axis=1)
    return tab.reshape(HEADS // nh, nh, kh, GRID_W, kh * GRID_W)


def _neighbourhood_attention(u, u_ctx, bias_tab):
    B, T, _ = u.shape
    L = u_ctx.shape[1]
    rows = T // GRID_W
    kh = min(NA_MAX_ROWS, rows)
    blk = D_MODEL // GROUP
    first = COL_NA // GROUP

    def col(part):
        return pl.BlockSpec((None, T, GROUP), lambda b, g: (b, 0, first + part * blk + g))

    def col_ctx(part):
        return pl.BlockSpec((None, L, GROUP), lambda b, g: (b, 0, first + part * blk + g))

    return pl.pallas_call(
        functools.partial(_na_kernel, rows=rows),
        out_shape=jax.ShapeDtypeStruct((B, T, D_MODEL), BF16),
        grid=(B, blk),
        in_specs=[col(0), col(1), col(2), col(3), col_ctx(1), col_ctx(2),
                  pl.BlockSpec((None, NA_HEADS_PER_STEP, kh, GRID_W, kh * GRID_W), lambda b, g: (g, 0, 0, 0, 0))],
        out_specs=pl.BlockSpec((None, T, GROUP), lambda b, g: (b, 0, g)),
        compiler_params=_cparams(("arbitrary", "arbitrary")),
        name="neighbourhood_attention",
    )(u, u, u, u, u_ctx, u_ctx, bias_tab)


def _block_diag_masks():
    r = lax.broadcasted_iota(jnp.int32, (GROUP, GROUP), 0)
    c = lax.broadcasted_iota(jnp.int32, (GROUP, GROUP), 1)
    return (r // HEAD_DIM) == (c // HEAD_DIM)


def _head_sum(x, ones_bd):
    rows = x.shape[0]
    xb = x.astype(BF16)
    stacked = jnp.concatenate([xb[:, g * GROUP:(g + 1) * GROUP] for g in range(N_GROUPS)], axis=0)
    sums = jnp.dot(stacked, ones_bd, preferred_element_type=F32)
    return jnp.concatenate([sums[g * rows:(g + 1) * rows] for g in range(N_GROUPS)], axis=1)


def _rwkv_prepare(d, local, cidx, n_chunks, refs, prm, ones_bd, rotary, ops_ref, pc_ref):
    rkv_ref, prev_ref, next_ref, lo_ref, rope_ref, bonus_ref = refs
    mu_ref, w0_ref, wup_ref, a0_ref, aup_ref, kk_ref, ka_ref, rk_ref = prm
    C = CHUNK
    halo = prev_ref.shape[0]
    block = rkv_ref.shape[0]
    rows = pl.ds(pl.multiple_of(local * C, C), C)
    rows_before = pl.ds(pl.multiple_of(jnp.maximum(local * C - halo, 0), halo), halo)
    rows_after = pl.ds(pl.multiple_of(jnp.minimum(local * C + C, block - halo), halo), halo)
    chunk_id = jnp.full((halo, D_MODEL), local, jnp.int32)
    at_block_start = chunk_id == 0
    at_block_end = chunk_id == block // C - 1

    def store(name, value):
        ops_ref[d * len(OPERANDS) + OPERANDS.index(name)] = value.astype(BF16)

    tt = lax.broadcasted_iota(jnp.int32, (C, C + 2 * halo), 0)
    ss = lax.broadcasted_iota(jnp.int32, (C, C + 2 * halo), 1) - halo
    first = jnp.where(cidx > 0, -1, 0)
    last = jnp.where(cidx < n_chunks - 1, C, C - 1)
    nb = (jnp.abs(ss - tt) == 1) & (ss >= first) & (ss <= last)
    nb = jnp.where(nb, 1.0, 0.0).astype(BF16)

    def shifted(i):
        lanes = slice(i * D_MODEL, (i + 1) * D_MODEL)
        x = rkv_ref[rows, lanes]
        before = jnp.where(at_block_start, prev_ref[:, lanes], rkv_ref[rows_before, lanes])
        after = jnp.where(at_block_end, next_ref[:, lanes], rkv_ref[rows_after, lanes])
        both = jnp.dot(nb, jnp.concatenate([before, x, after], axis=0), preferred_element_type=F32)
        mu = mu_ref[i:i + 1, :]
        return (1.0 - mu) * x.astype(F32) + (0.5 * mu) * both

    r_s = shifted(0)
    yield
    k_s = shifted(1)
    yield
    v_s = shifted(2)
    store("v", v_s)
    yield

    if rotary:
        lane = lax.broadcasted_iota(jnp.int32, (1, D_MODEL), 1)
        low = (lane % 32) < 16
        cos_t = jnp.tile(rope_ref[rows, :LANES], (1, D_MODEL // LANES))
        sin_t = jnp.tile(rope_ref[rows, LANES:], (1, D_MODEL // LANES))

        def rope(x):
            partner = jnp.where(low, pltpu.roll(x, D_MODEL - 16, axis=1), pltpu.roll(x, 16, axis=1))
            return x * cos_t + partner * sin_t

        r_s = rope(r_s)
        yield
        k_s = rope(k_s)
        yield

    lo = lo_ref[rows, :]
    lw = w0_ref[d:d + 1, :] + jnp.dot(jnp.tanh(lo[:, :2 * LORA].astype(F32)).astype(BF16), wup_ref[d],
                                     preferred_element_type=F32)
    ld = (-np.exp(-0.5) * LOG2_E) / (1.0 + jnp.exp2(lw))
    yield
    a = 1.0 / (1.0 + jnp.exp2(a0_ref[d:d + 1, :]
                              + jnp.dot(lo[:, 2 * LORA:], aup_ref[d], preferred_element_type=F32)))
    yield

    kk = k_s * kk_ref[...]
    kk = kk * jnp.minimum(lax.rsqrt(_head_sum(kk * kk, ones_bd)), 1e12)
    yield
    k_dir = k_s * ((1.0 - ka_ref[...]) + a * ka_ref[...])
    b_vec = kk * a
    bonus_ref[rows, :] = (_head_sum(r_s * rk_ref[...] * k_dir, ones_bd) * v_s).astype(bonus_ref.dtype)
    yield

    tt = lax.broadcasted_iota(jnp.int32, (C, C), 0)
    ss = lax.broadcasted_iota(jnp.int32, (C, C), 1)
    tri = jnp.where((ss <= tt) if d == 0 else (ss >= tt), 1.0, 0.0).astype(BF16)
    ld_hi = ld.astype(BF16)
    ld_lo = (ld - ld_hi.astype(F32)).astype(BF16)
    cl = (jnp.dot(tri, ld_hi, preferred_element_type=F32) + jnp.dot(tri, ld_lo, preferred_element_type=F32))
    cl_tot = cl[C - 1:C, :] if d == 0 else cl[0:1, :]
    yield
    e_neg = jnp.exp2(-cl)
    p_c = jnp.exp2(cl_tot)
    pc_ref[d] = jnp.broadcast_to(p_c, pc_ref.shape[1:])
    b_t = b_vec * e_neg
    store("b", b_t)
    store("bh", b_t * p_c)
    yield
    k_t = k_dir * e_neg
    store("k", k_t)
    store("kh", k_t * p_c)
    yield
    store("a", -kk * jnp.exp2(cl - ld))
    store("r", r_s * jnp.exp2(cl))


OPERANDS = ("a", "r", "b", "k", "bh", "kh", "v")


PREPARE_STEPS = 2


def _interleave(*gens, steps=None):
    live = [[gen, 1 if steps is None else steps[i]] for i, gen in enumerate(gens)]
    while live:
        for entry in list(live):
            for _ in range(entry[1]):
                try:
                    next(entry[0])
                except StopIteration:
                    live.remove(entry)
                    break


def _rwkv_chunk_matmuls(ops_ref, pc_ref, h_scr, yf_ref, yb_ref, local_f, local_b, same_head, emit_y):
    C = CHUNK
    n_ops = len(OPERANDS)
    n_g = 2 * N_GROUPS
    dirs = [g // N_GROUPS for g in range(n_g)]

    def lanes_of(g):
        return slice((g % N_GROUPS) * GROUP, (g % N_GROUPS + 1) * GROUP)

    def grp(name):
        i = OPERANDS.index(name)
        return [ops_ref[dirs[g] * n_ops + i, :, lanes_of(g)] for g in range(n_g)]

    t_i = lax.broadcasted_iota(jnp.int32, (C, GROUP), 0)
    s_i = lax.broadcasted_iota(jnp.int32, (C, GROUP), 1) % C
    before = [s_i < t_i, s_i > t_i]
    upto = [s_i <= t_i, s_i >= t_i]
    eye = jnp.where(t_i == s_i, 1.0, 0.0).astype(F32)
    nt = (((1,), (1,)), ((), ()))
    tn = (((0,), (0,)), ((), ()))

    lane_in_tile = lax.broadcasted_iota(jnp.int32, (C, LANES), 1)
    half_of_tile = [lane_in_tile < HEAD_DIM, lane_in_tile >= HEAD_DIM]

    def bd(x):
        zero_tile = jnp.zeros((C, LANES), x.dtype)
        row_blocks = []
        for h in range(HEADS_PER_GROUP):
            tile = h * HEAD_DIM // LANES
            kept = jnp.where(half_of_tile[h % (LANES // HEAD_DIM)], x[:, tile * LANES:(tile + 1) * LANES], zero_tile)
            row_blocks.append(jnp.concatenate(
                [kept if t == tile else zero_tile for t in range(GROUP // LANES)], axis=1))
        return jnp.concatenate(row_blocks, axis=0)

    def hdot(a_side, x):
        return jnp.dot(a_side, bd(x), preferred_element_type=F32)

    def masked(m, keep):
        return jnp.where(keep, m, 0.0).astype(BF16)

    at, rt, bt = grp("a"), grp("r"), grp("b")
    lhs = [jnp.concatenate([at[g], rt[g]], axis=0) for g in range(n_g)] if emit_y else at
    a_ab, a_rb, a_ak, a_rk = [], [], [], []
    for g in range(n_g):
        m_b = lax.dot_general(lhs[g], bd(bt[g]), nt, preferred_element_type=F32)
        a_ab.append(masked(m_b[:C], before[dirs[g]]))
        if emit_y:
            a_rb.append(masked(m_b[C:], upto[dirs[g]]))
    yield
    kt = grp("k")
    for g in range(n_g):
        m_k = lax.dot_general(lhs[g], bd(kt[g]), nt, preferred_element_type=F32)
        a_ak.append(masked(m_k[:C], before[dirs[g]]))
        if emit_y:
            a_rk.append(masked(m_k[C:], upto[dirs[g]]))
    yield
    vv = grp("v")
    w_1, y_acc = [], []
    for g in range(n_g):
        if emit_y:
            w_v = hdot(jnp.concatenate([a_ak[g], a_rk[g]], axis=0), vv[g])
            y_acc.append(w_v[C:])
        else:
            w_v = hdot(a_ak[g], vv[g])
        w_1.append(w_v[:C])
    yield

    pw = [hdot(a_ab[g], a_ab[g]).astype(BF16) for g in range(n_g)]
    inv = [eye + a_ab[g].astype(F32) for g in range(n_g)]
    yield
    n_factors = C.bit_length() - 1
    for j in range(1, n_factors):
        last = j == n_factors - 1
        for g in range(n_g):
            inv_bf = inv[g].astype(BF16)
            prod = hdot(inv_bf if last else jnp.concatenate([inv_bf, pw[g]], axis=0), pw[g])
            inv[g] = inv[g] + prod[:C]
            if not last:
                pw[g] = prod[C:].astype(BF16)
        yield
    inv = [inv[g].astype(BF16) for g in range(n_g)]

    w_all, h_dec = [], []
    for g in range(n_g):
        decay_rows = (eye * pc_ref[dirs[g], 0:1, lanes_of(g)]).astype(BF16)
        stack = jnp.concatenate([at[g]] + ([rt[g]] if emit_y else []) + [decay_rows], axis=0)
        s1 = hdot(stack, h_scr[:, g * GROUP:(g + 1) * GROUP].astype(BF16))
        w_all.append((s1[:C] + w_1[g]).astype(BF16))
        if emit_y:
            y_acc[g] = y_acc[g] + s1[C:2 * C]
        h_dec.append(s1[-C:])
    yield
    u = [hdot(inv[g], w_all[g]).astype(BF16) for g in range(n_g)]
    yield
    if emit_y:
        out_rows = [pl.ds(pl.multiple_of(local * C, C), C) for local in (local_f, local_b)]
        for g in range(n_g):
            y_ref = yf_ref if dirs[g] == 0 else yb_ref
            y_ref[out_rows[dirs[g]], lanes_of(g)] = (y_acc[g] + hdot(a_rb[g], u[g])).astype(y_ref.dtype)
    yield

    bh, kh = grp("bh"), grp("kh")
    lane_head = lax.broadcasted_iota(jnp.int32, (HEAD_DIM, GROUP), 1) // HEAD_DIM
    for g in range(n_g):
        full = lax.dot_general(jnp.concatenate([bh[g], kh[g]], axis=0),
                               jnp.concatenate([u[g], vv[g]], axis=0), tn, preferred_element_type=F32)
        upd = full[(HEADS_PER_GROUP - 1) * HEAD_DIM:]
        for j in range(HEADS_PER_GROUP - 2, -1, -1):
            upd = jnp.where(lane_head == j, full[j * HEAD_DIM:(j + 1) * HEAD_DIM], upd)
        h_scr[:, g * GROUP:(g + 1) * GROUP] = h_dec[g] + upd


def _rwkv_kernel(*refs, n_blocks, rotary, emit_y):
    per_dir = 5
    in_f, in_b = refs[:per_dir], refs[per_dir:2 * per_dir]
    prm = refs[2 * per_dir:2 * per_dir + 8]
    s0_ref = refs[2 * per_dir + 8]
    yf_ref, yb_ref, bonf_ref, bonb_ref, sout_ref, h_scr, ops_scr, pc_scr = refs[2 * per_dir + 9:]
    j = pl.program_id(1)
    per_block = in_f[0].shape[0] // CHUNK
    n_chunks = n_blocks * per_block

    @pl.when(j == 0)
    def _():
        h_scr[:, :D_MODEL] = s0_ref[0]
        h_scr[:, D_MODEL:] = s0_ref[1]

    if not emit_y:
        yf_ref[...] = jnp.zeros(yf_ref.shape, yf_ref.dtype)
        yb_ref[...] = jnp.zeros(yb_ref.shape, yb_ref.dtype)

    same_head = _block_diag_masks()
    ones_bd = jnp.where(same_head, 1.0, 0.0).astype(BF16)

    def local(d, s):
        return s if d == 0 else per_block - 1 - s

    def prepare(d, s, buf):
        block = j if d == 0 else n_blocks - 1 - j
        ins, bonus_ref = (in_f, bonf_ref) if d == 0 else (in_b, bonb_ref)
        return _rwkv_prepare(d, local(d, s), block * per_block + local(d, s), n_chunks, ins + (bonus_ref,), prm,
                             ones_bd, rotary, ops_scr.at[buf], pc_scr.at[buf])

    def matmuls(s, buf):
        return _rwkv_chunk_matmuls(ops_scr.at[buf], pc_scr.at[buf], h_scr, yf_ref, yb_ref,
                                   local(0, s), local(1, s), same_head, emit_y)

    _interleave(prepare(0, 0, 0), prepare(1, 0, 0))

    def chunk_pair(i, carry):
        s = 2 * i
        ahead = jnp.minimum(s + 2, per_block - 1)
        _interleave(matmuls(s, 0), itertools.chain(prepare(0, s + 1, 1), prepare(1, s + 1, 1)),
                    steps=(1, PREPARE_STEPS))
        _interleave(matmuls(s + 1, 1), itertools.chain(prepare(0, ahead, 0), prepare(1, ahead, 0)),
                    steps=(1, PREPARE_STEPS))
        return carry

    lax.fori_loop(0, per_block // 2, chunk_pair, 0)

    @pl.when(j == n_blocks - 1)
    def _():
        sout_ref[0] = h_scr[:, :D_MODEL]
        sout_ref[1] = h_scr[:, D_MODEL:]


def _rope_tables(T):
    half = HEAD_DIM // 2
    n_freq = half // 2
    t = np.arange(T)[:, None]
    lane = np.arange(LANES)[None, :]
    inv_freq = ROPE_THETA ** (-(np.arange(n_freq, dtype=np.float32)) / n_freq)
    pos = np.where((lane % HEAD_DIM) < half, t // GRID_W, t % GRID_W).astype(np.float32)
    ang = (pos * inv_freq[lane % n_freq].astype(np.float32)).astype(np.float32)
    sign = np.where((lane % half) < n_freq, -1.0, 1.0)
    return jnp.asarray(np.cos(ang), F32), jnp.asarray(np.sin(ang) * sign, F32)


def _rwkv_scan(u, state0, params, rotary, emit_y):
    B, T, _ = u.shape
    block = min(T, RWKV_BLOCK_CHUNKS * CHUNK)
    n_blocks = T // block
    assert T % block == 0 and (block // CHUNK) % 2 == 0
    halo = 16
    per = block // halo

    def block_of(d, j):
        return j if d == 0 else n_blocks - 1 - j

    def specs(d):
        rkv = COL_RKV // (3 * D_MODEL)
        main = pl.BlockSpec((None, block, 3 * D_MODEL), lambda b, j: (b, block_of(d, j), rkv))
        prev = pl.BlockSpec((None, halo, 3 * D_MODEL),
                            lambda b, j: (b, jnp.maximum(block_of(d, j) * per - 1, 0), rkv))
        nxt = pl.BlockSpec((None, halo, 3 * D_MODEL),
                           lambda b, j: (b, jnp.minimum((block_of(d, j) + 1) * per, T // halo - 1), rkv))
        lora = pl.BlockSpec((None, block, 4 * LORA), lambda b, j: (b, block_of(d, j), COL_LORA // (4 * LORA)))
        tab = pl.BlockSpec((block, 2 * LANES), lambda b, j: (block_of(d, j), 0))
        return [main, prev, nxt, lora, tab]

    def whole(shape):
        return pl.BlockSpec(shape, lambda b, c: (0,) * len(shape))

    rope_tab = jnp.concatenate(_rope_tables(T), axis=1)
    state = pl.BlockSpec((None, 2, HEAD_DIM, D_MODEL), lambda b, c: (b, 0, 0, 0))
    param_specs = [whole((3, D_MODEL)), whole((2, D_MODEL)), whole((2, 2 * LORA, D_MODEL)),
                   whole((2, D_MODEL)), whole((2, 2 * LORA, D_MODEL)),
                   whole((1, D_MODEL)), whole((1, D_MODEL)), whole((1, D_MODEL))]
    seq = jax.ShapeDtypeStruct((B, T, D_MODEL), BF16)
    out_f = pl.BlockSpec((None, block, D_MODEL), lambda b, j: (b, block_of(0, j), 0))
    out_b = pl.BlockSpec((None, block, D_MODEL), lambda b, j: (b, block_of(1, j), 0))
    seq_in = (u,) * 4 + (rope_tab,)
    return pl.pallas_call(
        functools.partial(_rwkv_kernel, n_blocks=n_blocks, rotary=rotary, emit_y=emit_y),
        out_shape=(seq, seq, seq, seq, jax.ShapeDtypeStruct((B, 2, HEAD_DIM, D_MODEL), F32)),
        grid=(B, n_blocks),
        in_specs=specs(0) + specs(1) + param_specs + [state],
        out_specs=(out_f, out_b, out_f, out_b, state),
        scratch_shapes=[pltpu.VMEM((HEAD_DIM, 2 * D_MODEL), F32),
                        pltpu.VMEM((2, 2 * len(OPERANDS), CHUNK, D_MODEL), BF16),
                        pltpu.VMEM((2, 2, 8, D_MODEL), F32)],
        compiler_params=_cparams(("arbitrary", "arbitrary")),
        name="rwkv7_chunk_scan_rot" if rotary else "rwkv7_chunk_scan_ctx",
    )(*seq_in, *seq_in, *params, state0)


def _final_kernel(x_ref, ya_ref, yf_ref, yb_ref, bf_ref, bb_ref, z_ref, ga_ref, gb_ref, gate_ref,
                  gng_ref, gnb_ref, wa_ref, wb_ref, wo_ref, lng_ref, lnb_ref, o_ref):
    same_head = _block_diag_masks()
    ones_bd = jnp.where(same_head, 1.0, 0.0).astype(BF16)
    y = yf_ref[...].astype(F32) + yb_ref[...].astype(F32)
    mu = _head_sum(y, ones_bd) * (1.0 / HEAD_DIM)
    yc = y - mu
    var = _head_sum(yc * yc, ones_bd) * (1.0 / HEAD_DIM)
    yn = yc * lax.rsqrt(var + GN_EPS) * gng_ref[...] + gnb_ref[...]
    yn = yn + bf_ref[...].astype(F32) + bb_ref[...].astype(F32)
    y_b = (yn * _silu(z_ref[...].astype(F32))).astype(BF16)
    p_a = jnp.dot(ya_ref[...], wa_ref[...], preferred_element_type=F32)
    p_b = jnp.dot(y_b, wb_ref[...], preferred_element_type=F32)
    merged = (jax.nn.sigmoid(ga_ref[...].astype(F32)) * p_a
              + jax.nn.sigmoid(gb_ref[...].astype(F32)) * p_b)
    out = jnp.dot(merged.astype(BF16), wo_ref[...], preferred_element_type=F32)
    t = ALPHA * x_ref[...] + gate_ref[...] * out
    m = jnp.mean(t, axis=-1, keepdims=True)
    tc = t - m
    v = jnp.mean(tc * tc, axis=-1, keepdims=True)
    o_ref[...] = tc * lax.rsqrt(v + LN_EPS) * lng_ref[...] + lnb_ref[...]


def _final_stage(x, y_a, y_f, y_b, bonus_f, bonus_b, u, mod3, gn_g, gn_b, wa, wb, wo, ln_g, ln_b, tm):
    B, T, _ = x.shape

    def tok():
        return pl.BlockSpec((None, tm, D_MODEL), lambda b, i: (b, i, 0))

    def ucol(blk):
        return pl.BlockSpec((None, tm, D_MODEL), lambda b, i: (b, i, blk))

    def vec():
        return pl.BlockSpec((1, D_MODEL), lambda b, i: (0, 0))

    def mat():
        return pl.BlockSpec((D_MODEL, D_MODEL), lambda b, i: (0, 0))

    return pl.pallas_call(
        _final_kernel,
        out_shape=jax.ShapeDtypeStruct((B, T, D_MODEL), F32),
        grid=(B, T // tm),
        in_specs=[tok(), tok(), tok(), tok(), tok(), tok(), ucol(7), ucol(8), ucol(9),
                  pl.BlockSpec((None, 1, D_MODEL), lambda b, i: (b, 0, 2)),
                  vec(), vec(), mat(), mat(), mat(), vec(), vec()],
        out_specs=tok(),
        compiler_params=_cparams(("arbitrary", "arbitrary")),
        name="readout_merge_out_proj",
    )(x, y_a, y_f, y_b, bonus_f, bonus_b, u, u, u, mod3, gn_g, gn_b, wa, wb, wo, ln_g, ln_b)


def _layer(x, c, ctx, c_ctx, w_ada, b_ada, w_in, na_rpb, rw_mu, rw_w0, rw_w_up, rw_a0, rw_a_up,
           rw_k_k, rw_k_a, rw_r_k, rw_gn_g, rw_gn_b, w_branch_a, w_branch_b, w_out, ln_g, ln_b):
    B, T, _ = x.shape
    L = ctx.shape[1]
    rows = T // GRID_W

    n_cond = -(-(B + 1) // 8) * 8
    cond = jnp.zeros((n_cond, D_MODEL), F32).at[:B].set(c).at[B].set(c_ctx)
    mod3 = _modulation(cond, w_ada, b_ada).reshape(n_cond, 1, 3 * D_MODEL)

    w_bf = jnp.concatenate(
        [w_in[:, 4 * D_MODEL:7 * D_MODEL], w_in[:, :D_MODEL], w_in[:, D_MODEL:2 * D_MODEL] * LOG2_E,
         w_in[:, 2 * D_MODEL:4 * D_MODEL], w_in[:, 7 * D_MODEL:N_MAIN],
         w_in[:, N_MAIN + 4 * LORA:], w_in[:, N_MAIN:N_MAIN + 4 * LORA],
         jnp.zeros((D_MODEL, N_PAD - N_PROJ), w_in.dtype)], axis=1).astype(BF16)
    u = _in_projection(x, mod3, w_bf, lambda b: b, tm=min(T, 2048), tn=1536)
    u_ctx = _in_projection(ctx.reshape(1, B * L, D_MODEL), mod3, w_bf, lambda b: B, tm=B * L,
                           tn=1536).reshape(B, L, N_PAD)

    y_a = _neighbourhood_attention(u, u_ctx, _na_bias_table(na_rpb, rows))

    def lora_pad(w_up):
        out = jnp.zeros((2, 2 * LORA, D_MODEL), F32)
        for d in range(2):
            out = out.at[d, d * LORA:(d + 1) * LORA].set(w_up[d])
        return (out * -LOG2_E).astype(BF16)

    params = (rw_mu, rw_w0 * -LOG2_E, lora_pad(rw_w_up), rw_a0 * -LOG2_E, lora_pad(rw_a_up),
              rw_k_k.reshape(1, D_MODEL),
              rw_k_a.reshape(1, D_MODEL), rw_r_k.reshape(1, D_MODEL))
    state0 = jnp.zeros((B, 2, HEAD_DIM, D_MODEL), F32)
    state_c = _rwkv_scan(u_ctx, state0, params, rotary=False, emit_y=False)[-1]
    y_f, y_b, bonus_f, bonus_b, _ = _rwkv_scan(u, state_c, params, rotary=True, emit_y=True)

    return _final_stage(x, y_a, y_f, y_b, bonus_f, bonus_b, u, mod3, rw_gn_g.reshape(1, D_MODEL),
                        rw_gn_b.reshape(1, D_MODEL), w_branch_a.astype(BF16), w_branch_b.astype(BF16),
                        w_out.astype(BF16), ln_g.reshape(1, D_MODEL), ln_b.reshape(1, D_MODEL),
                        tm=min(T, 512))


def kernel(x, c, ctx, c_ctx, w_ada, b_ada, w_in, na_rpb, rw_mu, rw_w0, rw_w_up, rw_a0, rw_a_up, rw_k_k, rw_k_a, rw_r_k, rw_gn_g, rw_gn_b, w_branch_a, w_branch_b, w_out, ln_g, ln_b):
    assert w_ada.shape[0] == DEPTH
    return _layer(x, c, ctx, c_ctx, w_ada[0], b_ada[0], w_in[0], na_rpb[0], rw_mu[0], rw_w0[0],
                  rw_w_up[0], rw_a0[0], rw_a_up[0], rw_k_k[0], rw_k_a[0], rw_r_k[0], rw_gn_g[0],
                  rw_gn_b[0], w_branch_a[0], w_branch_b[0], w_out[0], ln_g[0], ln_b[0])
```

```python
import functools
import itertools

import numpy as np
import jax
import jax.numpy as jnp
from jax import lax
from jax.experimental import pallas as pl
from jax.experimental.pallas import tpu as pltpu

F32 = jnp.float32
BF16 = jnp.bfloat16
HIGHEST = lax.Precision.HIGHEST

D_MODEL = 1024
GRID_W = 64
HEADS = 16
HEAD_DIM = 64
NA_MAX_ROWS = 8
NA_COLS = 16
LORA = 64
DEPTH = 1
ROPE_THETA = 10000.0
LN_EPS = 1e-5
GN_EPS = 64e-5
ALPHA = (2 * DEPTH) ** 0.25
LOG2_E = float(np.log2(np.e))

LANES = 128
GROUP = 256
HEADS_PER_GROUP = GROUP // HEAD_DIM
N_GROUPS = D_MODEL // GROUP
CHUNK = 64
RWKV_BLOCK_CHUNKS = 8

COL_RKV = 0
COL_NA = 3 * D_MODEL
COL_ZB = 7 * D_MODEL
N_MAIN = 8 * D_MODEL
COL_MG = N_MAIN
COL_LORA = N_MAIN + 2 * D_MODEL
N_PROJ = COL_LORA + 4 * LORA
N_PAD = 21 * 512
VMEM_LIMIT = 56 * 1024 * 1024


def _cparams(sem):
    return pltpu.CompilerParams(dimension_semantics=sem, vmem_limit_bytes=VMEM_LIMIT)


def _silu(x):
    return x * jax.nn.sigmoid(x)


def _mod_kernel(c_ref, w_ref, b_ref, o_ref):
    s = _silu(c_ref[...])
    o_ref[...] = jnp.dot(s, w_ref[...], preferred_element_type=F32, precision=HIGHEST) + b_ref[...]


def _modulation(cond, w_ada, b_ada):
    rows = cond.shape[0]
    return pl.pallas_call(
        _mod_kernel,
        out_shape=jax.ShapeDtypeStruct((rows, 3 * D_MODEL), F32),
        grid=(3,),
        in_specs=[pl.BlockSpec((rows, D_MODEL), lambda j: (0, 0)),
                  pl.BlockSpec((D_MODEL, D_MODEL), lambda j: (0, j)),
                  pl.BlockSpec((1, D_MODEL), lambda j: (0, j))],
        out_specs=pl.BlockSpec((rows, D_MODEL), lambda j: (0, j)),
        compiler_params=_cparams(("arbitrary",)),
        name="ada_modulation",
    )(cond, w_ada, b_ada.reshape(1, 3 * D_MODEL))


def _inproj_kernel(x_ref, sh_ref, sc_ref, w_ref, o_ref, h_scr):
    @pl.when(pl.program_id(2) == 0)
    def _():
        x = x_ref[...]
        mu = jnp.mean(x, axis=-1, keepdims=True)
        xc = x - mu
        var = jnp.mean(xc * xc, axis=-1, keepdims=True)
        y = xc * lax.rsqrt(var + LN_EPS)
        h_scr[...] = (y * (1.0 + sc_ref[...]) + sh_ref[...]).astype(BF16)

    o_ref[...] = jnp.dot(h_scr[...], w_ref[...], preferred_element_type=F32).astype(BF16)


def _in_projection(x, mod3, w_bf, row_of_batch, tm, tn):
    B, T, _ = x.shape
    return pl.pallas_call(
        _inproj_kernel,
        out_shape=jax.ShapeDtypeStruct((B, T, N_PAD), BF16),
        grid=(B, T // tm, N_PAD // tn),
        in_specs=[pl.BlockSpec((None, tm, D_MODEL), lambda b, i, j: (b, i, 0)),
                  pl.BlockSpec((None, 1, D_MODEL), lambda b, i, j: (row_of_batch(b), 0, 0)),
                  pl.BlockSpec((None, 1, D_MODEL), lambda b, i, j: (row_of_batch(b), 0, 1)),
                  pl.BlockSpec((D_MODEL, tn), lambda b, i, j: (0, j))],
        out_specs=pl.BlockSpec((None, tm, tn), lambda b, i, j: (b, i, j)),
        scratch_shapes=[pltpu.VMEM((tm, D_MODEL), BF16)],
        compiler_params=_cparams(("arbitrary", "arbitrary", "arbitrary")),
        name="ln_mod_in_proj",
    )(x, mod3, mod3, w_bf)


NA_HEADS_PER_STEP = GROUP // HEAD_DIM
NA_ROWS_PER_ITER = 8
NA_ROW_GROUPS = 2


def _na_kernel(q_ref, k_ref, v_ref, z_ref, kc_ref, vc_ref, bias_ref, o_ref, *, rows):
    kh = min(NA_MAX_ROWS, rows)
    band = kh * GRID_W
    nh = NA_HEADS_PER_STEP
    lane = lax.broadcasted_iota(jnp.int32, (1, GROUP), 1)
    head_lanes = [(lane // HEAD_DIM) == h for h in range(nh)]
    kc = kc_ref[...]
    vc = vc_ref[...]
    nt = (((1,), (1,)), ((), ()))

    n_rows = NA_ROWS_PER_ITER if rows % NA_ROWS_PER_ITER == 0 else 1
    stack = nh * GRID_W

    def row_group(first_row, count, delay):
        for _ in range(delay):
            yield
        qs, s, vb, q0s = [], [], [], []
        for r in range(count):
            i = first_row + r
            rs = jnp.clip(i - kh // 2, 0, rows - kh)
            q0 = pl.multiple_of(i * GRID_W, GRID_W)
            k0 = pl.multiple_of(rs * GRID_W, GRID_W)
            q = q_ref[pl.ds(q0, GRID_W), :] * jnp.asarray(HEAD_DIM ** -0.5, BF16)
            q_st = jnp.concatenate([jnp.where(head_lanes[h], q, jnp.zeros_like(q)) for h in range(nh)], axis=0)
            s.append(lax.dot_general(q_st, k_ref[pl.ds(k0, band), :], nt, preferred_element_type=F32)
                     + bias_ref[:, i - rs].reshape(stack, band))
            vb.append(v_ref[pl.ds(k0, band), :])
            qs.append(q_st)
            q0s.append(q0)
        sc = lax.dot_general(jnp.concatenate(qs, axis=0), kc, nt, preferred_element_type=F32)
        yield
        p, pcs, denom = [], [], []
        for r in range(count):
            sc_r = sc[r * stack:(r + 1) * stack]
            m = jnp.maximum(jnp.max(s[r], axis=-1, keepdims=True), jnp.max(sc_r, axis=-1, keepdims=True))
            p_r = jnp.exp2(s[r] - m)
            pc_r = jnp.exp2(sc_r - m)
            denom.append(jnp.sum(p_r, axis=-1, keepdims=True) + jnp.sum(pc_r, axis=-1, keepdims=True))
            p.append(p_r.astype(BF16))
            pcs.append(pc_r.astype(BF16))
        yield
        o_ctx = jnp.dot(jnp.concatenate(pcs, axis=0), vc, preferred_element_type=F32)
        for r in range(count):
            o_all = (jnp.dot(p[r], vb[r], preferred_element_type=F32)
                     + o_ctx[r * stack:(r + 1) * stack]) / denom[r]
            o = o_all[(nh - 1) * GRID_W:]
            for h in range(nh - 2, -1, -1):
                o = jnp.where(head_lanes[h], o_all[h * GRID_W:(h + 1) * GRID_W], o)
            z = z_ref[pl.ds(q0s[r], GRID_W), :].astype(F32)
            o_ref[pl.ds(q0s[r], GRID_W), :] = (o * _silu(z)).astype(BF16)

    def rows_body(it, carry):
        per_group = n_rows // NA_ROW_GROUPS if n_rows % NA_ROW_GROUPS == 0 else n_rows
        groups = n_rows // per_group
        _interleave(*[row_group(it * n_rows + g * per_group, per_group, g) for g in range(groups)])
        return carry

    lax.fori_loop(0, rows // n_rows, rows_body, 0)


def _na_bias_table(rpb, rows):
    kh = min(NA_MAX_ROWS, rows)
    nh = NA_HEADS_PER_STEP
    n_dj = 2 * NA_COLS - 1
    cols = np.arange(GRID_W)
    cstart = np.clip(cols - NA_COLS // 2, 0, GRID_W - NA_COLS)
    col_mask = (cols[None, :] >= cstart[:, None]) & (cols[None, :] < cstart[:, None] + NA_COLS)
    dj = np.clip(cols[None, :] - cols[:, None] + NA_COLS - 1, 0, n_dj - 1)
    onehot = (np.arange(n_dj)[:, None] == dj.reshape(1, -1)).astype(np.float32)
    exp = jnp.dot(rpb.reshape(-1, n_dj), jnp.asarray(onehot), precision=HIGHEST)
    exp = exp.reshape(HEADS, 2 * NA_MAX_ROWS - 1, GRID_W, GRID_W)
    exp = jnp.where(col_mask[None, None], exp * LOG2_E, -jnp.inf).transpose(0, 2, 1, 3)
    lo = NA_MAX_ROWS - 1
    tab = jnp.stack([exp[:, :, lo - o: lo - o + kh].reshape(HEADS, GRID_W, kh * GRID_W) for o in range(kh)],
                    axis=1)
    return tab.reshape(HEADS // nh, nh, kh, GRID_W, kh * GRID_W)


def _neighbourhood_attention(u, u_ctx, bias_tab):
    B, T, _ = u.shape
    L = u_ctx.shape[1]
    rows = T // GRID_W
    kh = min(NA_MAX_ROWS, rows)
    blk = D_MODEL // GROUP
    first = COL_NA // GROUP

    def col(part):
        return pl.BlockSpec((None, T, GROUP), lambda b, g: (b, 0, first + part * blk + g))

    def col_ctx(part):
        return pl.BlockSpec((None, L, GROUP), lambda b, g: (b, 0, first + part * blk + g))

    return pl.pallas_call(
        functools.partial(_na_kernel, rows=rows),
        out_shape=jax.ShapeDtypeStruct((B, T, D_MODEL), BF16),
        grid=(B, blk),
        in_specs=[col(0), col(1), col(2), col(3), col_ctx(1), col_ctx(2),
                  pl.BlockSpec((None, NA_HEADS_PER_STEP, kh, GRID_W, kh * GRID_W), lambda b, g: (g, 0, 0, 0, 0))],
        out_specs=pl.BlockSpec((None, T, GROUP), lambda b, g: (b, 0, g)),
        compiler_params=_cparams(("arbitrary", "arbitrary")),
        name="neighbourhood_attention",
    )(u, u, u, u, u_ctx, u_ctx, bias_tab)


def _block_diag_masks():
    r = lax.broadcasted_iota(jnp.int32, (GROUP, GROUP), 0)
    c = lax.broadcasted_iota(jnp.int32, (GROUP, GROUP), 1)
    return (r // HEAD_DIM) == (c // HEAD_DIM)


def _head_sum(x, ones_bd):
    rows = x.shape[0]
    xb = x.astype(BF16)
    stacked = jnp.concatenate([xb[:, g * GROUP:(g + 1) * GROUP] for g in range(N_GROUPS)], axis=0)
    sums = jnp.dot(stacked, ones_bd, preferred_element_type=F32)
    return jnp.concatenate([sums[g * rows:(g + 1) * rows] for g in range(N_GROUPS)], axis=1)


def _rwkv_prepare(d, local, cidx, n_chunks, refs, prm, ones_bd, rotary, ops_ref, pc_ref):
    rkv_ref, prev_ref, next_ref, lo_ref, rope_ref, bonus_ref = refs
    mu_ref, w0_ref, wup_ref, a0_ref, aup_ref, kk_ref, ka_ref, rk_ref = prm
    C = CHUNK
    halo = prev_ref.shape[0]
    block = rkv_ref.shape[0]
    rows = pl.ds(pl.multiple_of(local * C, C), C)
    rows_before = pl.ds(pl.multiple_of(jnp.maximum(local * C - halo, 0), halo), halo)
    rows_after = pl.ds(pl.multiple_of(jnp.minimum(local * C + C, block - halo), halo), halo)
    chunk_id = jnp.full((halo, D_MODEL), local, jnp.int32)
    at_block_start = chunk_id == 0
    at_block_end = chunk_id == block // C - 1

    def store(name, value):
        ops_ref[d * len(OPERANDS) + OPERANDS.index(name)] = value.astype(BF16)

    tt = lax.broadcasted_iota(jnp.int32, (C, C + 2 * halo), 0)
    ss = lax.broadcasted_iota(jnp.int32, (C, C + 2 * halo), 1) - halo
    first = jnp.where(cidx > 0, -1, 0)
    last = jnp.where(cidx < n_chunks - 1, C, C - 1)
    nb = (jnp.abs(ss - tt) == 1) & (ss >= first) & (ss <= last)
    nb = jnp.where(nb, 1.0, 0.0).astype(BF16)

    def shifted(i):
        lanes = slice(i * D_MODEL, (i + 1) * D_MODEL)
        x = rkv_ref[rows, lanes]
        before = jnp.where(at_block_start, prev_ref[:, lanes], rkv_ref[rows_before, lanes])
        after = jnp.where(at_block_end, next_ref[:, lanes], rkv_ref[rows_after, lanes])
        both = jnp.dot(nb, jnp.concatenate([before, x, after], axis=0), preferred_element_type=F32)
        mu = mu_ref[i:i + 1, :]
        return (1.0 - mu) * x.astype(F32) + (0.5 * mu) * both

    r_s = shifted(0)
    yield
    k_s = shifted(1)
    yield
    v_s = shifted(2)
    store("v", v_s)
    yield

    if rotary:
        lane = lax.broadcasted_iota(jnp.int32, (1, D_MODEL), 1)
        low = (lane % 32) < 16
        cos_t = jnp.tile(rope_ref[rows, :LANES], (1, D_MODEL // LANES))
        sin_t = jnp.tile(rope_ref[rows, LANES:], (1, D_MODEL // LANES))

        def rope(x):
            partner = jnp.where(low, pltpu.roll(x, D_MODEL - 16, axis=1), pltpu.roll(x, 16, axis=1))
            return x * cos_t + partner * sin_t

        r_s = rope(r_s)
        yield
        k_s = rope(k_s)
        yield

    lo = lo_ref[rows, :]
    lw = w0_ref[d:d + 1, :] + jnp.dot(jnp.tanh(lo[:, :2 * LORA].astype(F32)).astype(BF16), wup_ref[d],
                                     preferred_element_type=F32)
    ld = (-np.exp(-0.5) * LOG2_E) / (1.0 + jnp.exp2(lw))
    yield
    a = 1.0 / (1.0 + jnp.exp2(a0_ref[d:d + 1, :]
                              + jnp.dot(lo[:, 2 * LORA:], aup_ref[d], preferred_element_type=F32)))
    yield

    kk = k_s * kk_ref[...]
    kk = kk * jnp.minimum(lax.rsqrt(_head_sum(kk * kk, ones_bd)), 1e12)
    yield
    k_dir = k_s * ((1.0 - ka_ref[...]) + a * ka_ref[...])
    b_vec = kk * a
    bonus_ref[rows, :] = (_head_sum(r_s * rk_ref[...] * k_dir, ones_bd) * v_s).astype(bonus_ref.dtype)
    yield

    tt = lax.broadcasted_iota(jnp.int32, (C, C), 0)
    ss = lax.broadcasted_iota(jnp.int32, (C, C), 1)
    tri = jnp.where((ss <= tt) if d == 0 else (ss >= tt), 1.0, 0.0).astype(BF16)
    ld_hi = ld.astype(BF16)
    ld_lo = (ld - ld_hi.astype(F32)).astype(BF16)
    cl = (jnp.dot(tri, ld_hi, preferred_element_type=F32) + jnp.dot(tri, ld_lo, preferred_element_type=F32))
    cl_tot = cl[C - 1:C, :] if d == 0 else cl[0:1, :]
    yield
    e_neg = jnp.exp2(-cl)
    p_c = jnp.exp2(cl_tot)
    pc_ref[d] = jnp.broadcast_to(p_c, pc_ref.shape[1:])
    b_t = b_vec * e_neg
    store("b", b_t)
    store("bh", b_t * p_c)
    yield
    k_t = k_dir * e_neg
    store("k", k_t)
    store("kh", k_t * p_c)
    yield
    store("a", -kk * jnp.exp2(cl - ld))
    store("r", r_s * jnp.exp2(cl))


OPERANDS = ("a", "r", "b", "k", "bh", "kh", "v")


PREPARE_STEPS = 2


def _interleave(*gens, steps=None):
    live = [[gen, 1 if steps is None else steps[i]] for i, gen in enumerate(gens)]
    while live:
        for entry in list(live):
            for _ in range(entry[1]):
                try:
                    next(entry[0])
                except StopIteration:
                    live.remove(entry)
                    break


def _rwkv_chunk_matmuls(ops_ref, pc_ref, h_scr, yf_ref, yb_ref, local_f, local_b, same_head, emit_y):
    C = CHUNK
    n_ops = len(OPERANDS)
    n_g = 2 * N_GROUPS
    dirs = [g // N_GROUPS for g in range(n_g)]

    def lanes_of(g):
        return slice((g % N_GROUPS) * GROUP, (g % N_GROUPS + 1) * GROUP)

    def grp(name):
        i = OPERANDS.index(name)
        return [ops_ref[dirs[g] * n_ops + i, :, lanes_of(g)] for g in range(n_g)]

    t_i = lax.broadcasted_iota(jnp.int32, (C, GROUP), 0)
    s_i = lax.broadcasted_iota(jnp.int32, (C, GROUP), 1) % C
    before = [s_i < t_i, s_i > t_i]
    upto = [s_i <= t_i, s_i >= t_i]
    eye = jnp.where(t_i == s_i, 1.0, 0.0).astype(F32)
    nt = (((1,), (1,)), ((), ()))
    tn = (((0,), (0,)), ((), ()))

    lane_in_tile = lax.broadcasted_iota(jnp.int32, (C, LANES), 1)
    half_of_tile = [lane_in_tile < HEAD_DIM, lane_in_tile >= HEAD_DIM]

    def bd(x):
        zero_tile = jnp.zeros((C, LANES), x.dtype)
        row_blocks = []
        for h in range(HEADS_PER_GROUP):
            tile = h * HEAD_DIM // LANES
            kept = jnp.where(half_of_tile[h % (LANES // HEAD_DIM)], x[:, tile * LANES:(tile + 1) * LANES], zero_tile)
            row_blocks.append(jnp.concatenate(
                [kept if t == tile else zero_tile for t in range(GROUP // LANES)], axis=1))
        return jnp.concatenate(row_blocks, axis=0)

    def hdot(a_side, x):
        return jnp.dot(a_side, bd(x), preferred_element_type=F32)

    def masked(m, keep):
        return jnp.where(keep, m, 0.0).astype(BF16)

    at, rt, bt = grp("a"), grp("r"), grp("b")
    lhs = [jnp.concatenate([at[g], rt[g]], axis=0) for g in range(n_g)] if emit_y else at
    a_ab, a_rb, a_ak, a_rk = [], [], [], []
    for g in range(n_g):
        m_b = lax.dot_general(lhs[g], bd(bt[g]), nt, preferred_element_type=F32)
        a_ab.append(masked(m_b[:C], before[dirs[g]]))
        if emit_y:
            a_rb.append(masked(m_b[C:], upto[dirs[g]]))
    yield
    kt = grp("k")
    for g in range(n_g):
        m_k = lax.dot_general(lhs[g], bd(kt[g]), nt, preferred_element_type=F32)
        a_ak.append(masked(m_k[:C], before[dirs[g]]))
        if emit_y:
            a_rk.append(masked(m_k[C:], upto[dirs[g]]))
    yield
    vv = grp("v")
    w_1, y_acc = [], []
    for g in range(n_g):
        if emit_y:
            w_v = hdot(jnp.concatenate([a_ak[g], a_rk[g]], axis=0), vv[g])
            y_acc.append(w_v[C:])
        else:
            w_v = hdot(a_ak[g], vv[g])
        w_1.append(w_v[:C])
    yield

    pw = [hdot(a_ab[g], a_ab[g]).astype(BF16) for g in range(n_g)]
    inv = [eye + a_ab[g].astype(F32) for g in range(n_g)]
    yield
    n_factors = C.bit_length() - 1
    for j in range(1, n_factors):
        last = j == n_factors - 1
        for g in range(n_g):
            inv_bf = inv[g].astype(BF16)
            prod = hdot(inv_bf if last else jnp.concatenate([inv_bf, pw[g]], axis=0), pw[g])
            inv[g] = inv[g] + prod[:C]
            if not last:
                pw[g] = prod[C:].astype(BF16)
        yield
    inv = [inv[g].astype(BF16) for g in range(n_g)]

    w_all, h_dec = [], []
    for g in range(n_g):
        decay_rows = (eye * pc_ref[dirs[g], 0:1, lanes_of(g)]).astype(BF16)
        stack = jnp.concatenate([at[g]] + ([rt[g]] if emit_y else []) + [decay_rows], axis=0)
        s1 = hdot(stack, h_scr[:, g * GROUP:(g + 1) * GROUP].astype(BF16))
        w_all.append((s1[:C] + w_1[g]).astype(BF16))
        if emit_y:
            y_acc[g] = y_acc[g] + s1[C:2 * C]
        h_dec.append(s1[-C:])
    yield
    u = [hdot(inv[g], w_all[g]).astype(BF16) for g in range(n_g)]
    yield
    if emit_y:
        out_rows = [pl.ds(pl.multiple_of(local * C, C), C) for local in (local_f, local_b)]
        for g in range(n_g):
            y_ref = yf_ref if dirs[g] == 0 else yb_ref
            y_ref[out_rows[dirs[g]], lanes_of(g)] = (y_acc[g] + hdot(a_rb[g], u[g])).astype(y_ref.dtype)
    yield

    bh, kh = grp("bh"), grp("kh")
    lane_head = lax.broadcasted_iota(jnp.int32, (HEAD_DIM, GROUP), 1) // HEAD_DIM
    for g in range(n_g):
        full = lax.dot_general(jnp.concatenate([bh[g], kh[g]], axis=0),
                               jnp.concatenate([u[g], vv[g]], axis=0), tn, preferred_element_type=F32)
        upd = full[(HEADS_PER_GROUP - 1) * HEAD_DIM:]
        for j in range(HEADS_PER_GROUP - 2, -1, -1):
            upd = jnp.where(lane_head == j, full[j * HEAD_DIM:(j + 1) * HEAD_DIM], upd)
        h_scr[:, g * GROUP:(g + 1) * GROUP] = h_dec[g] + upd


def _rwkv_kernel(*refs, n_blocks, rotary, emit_y):
    per_dir = 5
    in_f, in_b = refs[:per_dir], refs[per_dir:2 * per_dir]
    prm = refs[2 * per_dir:2 * per_dir + 8]
    s0_ref = refs[2 * per_dir + 8]
    yf_ref, yb_ref, bonf_ref, bonb_ref, sout_ref, h_scr, ops_scr, pc_scr = refs[2 * per_dir + 9:]
    j = pl.program_id(1)
    per_block = in_f[0].shape[0] // CHUNK
    n_chunks = n_blocks * per_block

    @pl.when(j == 0)
    def _():
        h_scr[:, :D_MODEL] = s0_ref[0]
        h_scr[:, D_MODEL:] = s0_ref[1]

    if not emit_y:
        yf_ref[...] = jnp.zeros(yf_ref.shape, yf_ref.dtype)
        yb_ref[...] = jnp.zeros(yb_ref.shape, yb_ref.dtype)

    same_head = _block_diag_masks()
    ones_bd = jnp.where(same_head, 1.0, 0.0).astype(BF16)

    def local(d, s):
        return s if d == 0 else per_block - 1 - s

    def prepare(d, s, buf):
        block = j if d == 0 else n_blocks - 1 - j
        ins, bonus_ref = (in_f, bonf_ref) if d == 0 else (in_b, bonb_ref)
        return _rwkv_prepare(d, local(d, s), block * per_block + local(d, s), n_chunks, ins + (bonus_ref,), prm,
                             ones_bd, rotary, ops_scr.at[buf], pc_scr.at[buf])

    def matmuls(s, buf):
        return _rwkv_chunk_matmuls(ops_scr.at[buf], pc_scr.at[buf], h_scr, yf_ref, yb_ref,
                                   local(0, s), local(1, s), same_head, emit_y)

    _interleave(prepare(0, 0, 0), prepare(1, 0, 0))

    def chunk_pair(i, carry):
        s = 2 * i
        ahead = jnp.minimum(s + 2, per_block - 1)
        _interleave(matmuls(s, 0), itertools.chain(prepare(0, s + 1, 1), prepare(1, s + 1, 1)),
                    steps=(1, PREPARE_STEPS))
        _interleave(matmuls(s + 1, 1), itertools.chain(prepare(0, ahead, 0), prepare(1, ahead, 0)),
                    steps=(1, PREPARE_STEPS))
        return carry

    lax.fori_loop(0, per_block // 2, chunk_pair, 0)

    @pl.when(j == n_blocks - 1)
    def _():
        sout_ref[0] = h_scr[:, :D_MODEL]
        sout_ref[1] = h_scr[:, D_MODEL:]


def _rope_tables(T):
    half = HEAD_DIM // 2
    n_freq = half // 2
    t = np.arange(T)[:, None]
    lane = np.arange(LANES)[None, :]
    inv_freq = ROPE_THETA ** (-(np.arange(n_freq, dtype=np.float32)) / n_freq)
    pos = np.where((lane % HEAD_DIM) < half, t // GRID_W, t % GRID_W).astype(np.float32)
    ang = (pos * inv_freq[lane % n_freq].astype(np.float32)).astype(np.float32)
    sign = np.where((lane % half) < n_freq, -1.0, 1.0)
    return jnp.asarray(np.cos(ang), F32), jnp.asarray(np.sin(ang) * sign, F32)


def _rwkv_scan(u, state0, params, rotary, emit_y):
    B, T, _ = u.shape
    block = min(T, RWKV_BLOCK_CHUNKS * CHUNK)
    n_blocks = T // block
    assert T % block == 0 and (block // CHUNK) % 2 == 0
    halo = 16
    per = block // halo

    def block_of(d, j):
        return j if d == 0 else n_blocks - 1 - j

    def specs(d):
        rkv = COL_RKV // (3 * D_MODEL)
        main = pl.BlockSpec((None, block, 3 * D_MODEL), lambda b, j: (b, block_of(d, j), rkv))
        prev = pl.BlockSpec((None, halo, 3 * D_MODEL),
                            lambda b, j: (b, jnp.maximum(block_of(d, j) * per - 1, 0), rkv))
        nxt = pl.BlockSpec((None, halo, 3 * D_MODEL),
                           lambda b, j: (b, jnp.minimum((block_of(d, j) + 1) * per, T // halo - 1), rkv))
        lora = pl.BlockSpec((None, block, 4 * LORA), lambda b, j: (b, block_of(d, j), COL_LORA // (4 * LORA)))
        tab = pl.BlockSpec((block, 2 * LANES), lambda b, j: (block_of(d, j), 0))
        return [main, prev, nxt, lora, tab]

    def whole(shape):
        return pl.BlockSpec(shape, lambda b, c: (0,) * len(shape))

    rope_tab = jnp.concatenate(_rope_tables(T), axis=1)
    state = pl.BlockSpec((None, 2, HEAD_DIM, D_MODEL), lambda b, c: (b, 0, 0, 0))
    param_specs = [whole((3, D_MODEL)), whole((2, D_MODEL)), whole((2, 2 * LORA, D_MODEL)),
                   whole((2, D_MODEL)), whole((2, 2 * LORA, D_MODEL)),
                   whole((1, D_MODEL)), whole((1, D_MODEL)), whole((1, D_MODEL))]
    seq = jax.ShapeDtypeStruct((B, T, D_MODEL), BF16)
    out_f = pl.BlockSpec((None, block, D_MODEL), lambda b, j: (b, block_of(0, j), 0))
    out_b = pl.BlockSpec((None, block, D_MODEL), lambda b, j: (b, block_of(1, j), 0))
    seq_in = (u,) * 4 + (rope_tab,)
    return pl.pallas_call(
        functools.partial(_rwkv_kernel, n_blocks=n_blocks, rotary=rotary, emit_y=emit_y),
        out_shape=(seq, seq, seq, seq, jax.ShapeDtypeStruct((B, 2, HEAD_DIM, D_MODEL), F32)),
        grid=(B, n_blocks),
        in_specs=specs(0) + specs(1) + param_specs + [state],
        out_specs=(out_f, out_b, out_f, out_b, state),
        scratch_shapes=[pltpu.VMEM((HEAD_DIM, 2 * D_MODEL), F32),
                        pltpu.VMEM((2, 2 * len(OPERANDS), CHUNK, D_MODEL), BF16),
                        pltpu.VMEM((2, 2, 8, D_MODEL), F32)],
        compiler_params=_cparams(("arbitrary", "arbitrary")),
        name="rwkv7_chunk_scan_rot" if rotary else "rwkv7_chunk_scan_ctx",
    )(*seq_in, *seq_in, *params, state0)


FINAL_ROW_PARTS = 2


def _final_kernel(x_ref, ya_ref, yf_ref, yb_ref, bf_ref, bb_ref, z_ref, ga_ref, gb_ref, gate_ref,
                  gng_ref, gnb_ref, wa_ref, wb_ref, wo_ref, lng_ref, lnb_ref, o_ref):
    same_head = _block_diag_masks()
    ones_bd = jnp.where(same_head, 1.0, 0.0).astype(BF16)
    n_parts = FINAL_ROW_PARTS
    part = x_ref.shape[0] // n_parts

    def row_part(i):
        for _ in range(i):
            yield
        rows = slice(i * part, (i + 1) * part)
        y = yf_ref[rows, :].astype(F32) + yb_ref[rows, :].astype(F32)
        mu = _head_sum(y, ones_bd) * (1.0 / HEAD_DIM)
        yc = y - mu
        var = _head_sum(yc * yc, ones_bd) * (1.0 / HEAD_DIM)
        yn = yc * lax.rsqrt(var + GN_EPS) * gng_ref[...] + gnb_ref[...]
        yn = yn + bf_ref[rows, :].astype(F32) + bb_ref[rows, :].astype(F32)
        y_b = (yn * _silu(z_ref[rows, :].astype(F32))).astype(BF16)
        yield
        p_a = jnp.dot(ya_ref[rows, :], wa_ref[...], preferred_element_type=F32)
        p_b = jnp.dot(y_b, wb_ref[...], preferred_element_type=F32)
        merged = (jax.nn.sigmoid(ga_ref[rows, :].astype(F32)) * p_a
                  + jax.nn.sigmoid(gb_ref[rows, :].astype(F32)) * p_b)
        yield
        out = jnp.dot(merged.astype(BF16), wo_ref[...], preferred_element_type=F32)
        t = ALPHA * x_ref[rows, :] + gate_ref[...] * out
        m = jnp.mean(t, axis=-1, keepdims=True)
        tc = t - m
        v = jnp.mean(tc * tc, axis=-1, keepdims=True)
        o_ref[rows, :] = tc * lax.rsqrt(v + LN_EPS) * lng_ref[...] + lnb_ref[...]

    _interleave(*[row_part(i) for i in range(n_parts)])


def _final_stage(x, y_a, y_f, y_b, bonus_f, bonus_b, u, mod3, gn_g, gn_b, wa, wb, wo, ln_g, ln_b, tm):
    B, T, _ = x.shape

    def tok():
        return pl.BlockSpec((None, tm, D_MODEL), lambda b, i: (b, i, 0))

    def ucol(blk):
        return pl.BlockSpec((None, tm, D_MODEL), lambda b, i: (b, i, blk))

    def vec():
        return pl.BlockSpec((1, D_MODEL), lambda b, i: (0, 0))

    def mat():
        return pl.BlockSpec((D_MODEL, D_MODEL), lambda b, i: (0, 0))

    return pl.pallas_call(
        _final_kernel,
        out_shape=jax.ShapeDtypeStruct((B, T, D_MODEL), F32),
        grid=(B, T // tm),
        in_specs=[tok(), tok(), tok(), tok(), tok(), tok(), ucol(7), ucol(8), ucol(9),
                  pl.BlockSpec((None, 1, D_MODEL), lambda b, i: (b, 0, 2)),
                  vec(), vec(), mat(), mat(), mat(), vec(), vec()],
        out_specs=tok(),
        compiler_params=_cparams(("arbitrary", "arbitrary")),
        name="readout_merge_out_proj",
    )(x, y_a, y_f, y_b, bonus_f, bonus_b, u, u, u, mod3, gn_g, gn_b, wa, wb, wo, ln_g, ln_b)


def _layer(x, c, ctx, c_ctx, w_ada, b_ada, w_in, na_rpb, rw_mu, rw_w0, rw_w_up, rw_a0, rw_a_up,
           rw_k_k, rw_k_a, rw_r_k, rw_gn_g, rw_gn_b, w_branch_a, w_branch_b, w_out, ln_g, ln_b):
    B, T, _ = x.shape
    L = ctx.shape[1]
    rows = T // GRID_W

    n_cond = -(-(B + 1) // 8) * 8
    cond = jnp.zeros((n_cond, D_MODEL), F32).at[:B].set(c).at[B].set(c_ctx)
    mod3 = _modulation(cond, w_ada, b_ada).reshape(n_cond, 1, 3 * D_MODEL)

    w_bf = jnp.concatenate(
        [w_in[:, 4 * D_MODEL:7 * D_MODEL], w_in[:, :D_MODEL], w_in[:, D_MODEL:2 * D_MODEL] * LOG2_E,
         w_in[:, 2 * D_MODEL:4 * D_MODEL], w_in[:, 7 * D_MODEL:N_MAIN],
         w_in[:, N_MAIN + 4 * LORA:], w_in[:, N_MAIN:N_MAIN + 4 * LORA],
         jnp.zeros((D_MODEL, N_PAD - N_PROJ), w_in.dtype)], axis=1).astype(BF16)
    u = _in_projection(x, mod3, w_bf, lambda b: b, tm=min(T, 2048), tn=1536)
    u_ctx = _in_projection(ctx.reshape(1, B * L, D_MODEL), mod3, w_bf, lambda b: B, tm=B * L,
                           tn=1536).reshape(B, L, N_PAD)

    y_a = _neighbourhood_attention(u, u_ctx, _na_bias_table(na_rpb, rows))

    def lora_pad(w_up):
        out = jnp.zeros((2, 2 * LORA, D_MODEL), F32)
        for d in range(2):
            out = out.at[d, d * LORA:(d + 1) * LORA].set(w_up[d])
        return (out * -LOG2_E).astype(BF16)

    params = (rw_mu, rw_w0 * -LOG2_E, lora_pad(rw_w_up), rw_a0 * -LOG2_E, lora_pad(rw_a_up),
              rw_k_k.reshape(1, D_MODEL),
              rw_k_a.reshape(1, D_MODEL), rw_r_k.reshape(1, D_MODEL))
    state0 = jnp.zeros((B, 2, HEAD_DIM, D_MODEL), F32)
    state_c = _rwkv_scan(u_ctx, state0, params, rotary=False, emit_y=False)[-1]
    y_f, y_b, bonus_f, bonus_b, _ = _rwkv_scan(u, state_c, params, rotary=True, emit_y=True)

    return _final_stage(x, y_a, y_f, y_b, bonus_f, bonus_b, u, mod3, rw_gn_g.reshape(1, D_MODEL),
                        rw_gn_b.reshape(1, D_MODEL), w_branch_a.astype(BF16), w_branch_b.astype(BF16),
                        w_out.astype(BF16), ln_g.reshape(1, D_MODEL), ln_b.reshape(1, D_MODEL),
                        tm=min(T, 512))


def kernel(x, c, ctx, c_ctx, w_ada, b_ada, w_in, na_rpb, rw_mu, rw_w0, rw_w_up, rw_a0, rw_a_up, rw_k_k, rw_k_a, rw_r_k, rw_gn_g, rw_gn_b, w_branch_a, w_branch_b, w_out, ln_g, ln_b):
    assert w_ada.shape[0] == DEPTH
    return _layer(x, c, ctx, c_ctx, w_ada[0], b_ada[0], w_in[0], na_rpb[0], rw_mu[0], rw_w0[0],
                  rw_w_up[0], rw_a0[0], rw_a_up[0], rw_k_k[0], rw_k_a[0], rw_r_k[0], rw_gn_g[0],
                  rw_gn_b[0], w_branch_a[0], w_branch_b[0], w_out[0], ln_g[0], ln_b[0])
```
